```python
import jax, jax.numpy as jnp
from jax import lax
import numpy as np

D_MODEL = 4096
BATCH = 4
SEQ = 4096
DEPTH = 1

CHUNK = 64
N_MEM = 256
D_MIX = D_MODEL
D_GMLP = D_MIX // 2
D_HGRN = D_MIX - D_GMLP
GMLP_HEADS = 4
GMLP_HEAD_DIM = D_GMLP // GMLP_HEADS
GMLP_BLOCK = 128
HGRN_HEAD_DIM = 128
HGRN_HEADS = D_HGRN // HGRN_HEAD_DIM
X_HEADS = 4
X_HEAD_DIM = D_MODEL // X_HEADS
N_GROUPS = 4
EXPERTS_PER_GROUP = 8
N_EXPERTS = N_GROUPS * EXPERTS_PER_GROUP
TOP_K = 2
D_EXPERT = D_MODEL // 4
MOE_BLOCK = 128
EPS = 1e-6
IN_WIDTHS = (D_GMLP, D_GMLP, D_HGRN, D_HGRN, D_HGRN, D_HGRN)
D_IN = sum(IN_WIDTHS)

kernel_name = "hybrid_gmlp_hgrn2_xattn_hmoe_block"


def rms_norm(x, g):
    xf = x.astype(jnp.float32)
    y = xf * lax.rsqrt(jnp.mean(xf * xf, axis=-1, keepdims=True) + EPS)
    return (y * g.astype(jnp.float32)).astype(x.dtype)


def layer_norm(x, g, b):
    xf = x.astype(jnp.float32)
    mu = jnp.mean(xf, axis=-1, keepdims=True)
    xc = xf - mu
    var = jnp.mean(xc * xc, axis=-1, keepdims=True)
    y = xc * lax.rsqrt(var + EPS) * g.astype(jnp.float32) + b.astype(jnp.float32)
    return y.astype(x.dtype)


def gmlp_mixer(u, v, ln_g, ln_b, w_s, b_s):
    B, S, _ = u.shape
    nb = S // GMLP_BLOCK
    v = v.reshape(B, S, GMLP_HEADS, GMLP_HEAD_DIM)
    v = layer_norm(v, ln_g.reshape(GMLP_HEADS, GMLP_HEAD_DIM), ln_b.reshape(GMLP_HEADS, GMLP_HEAD_DIM))
    v = v.reshape(B, nb, GMLP_BLOCK, GMLP_HEADS, GMLP_HEAD_DIM)
    cpos = jnp.arange(GMLP_BLOCK) // CHUNK
    mask = cpos[None, :] <= cpos[:, None]
    w = jnp.where(mask[None], w_s, 0.0)
    s = jnp.einsum('hij,bnjhc->bnihc', w, v) + b_s.T[None, None, :, :, None]
    return u * s.reshape(B, S, D_GMLP)


def hgrn2_mixer(q, f, i, g, lb, norm_g):
    B, S, _ = q.shape
    H, K, C = HGRN_HEADS, HGRN_HEAD_DIM, CHUNK
    nc = S // C
    f32 = jnp.float32

    def heads(t):
        return t.reshape(B, nc, C, H, K).transpose(0, 3, 1, 2, 4)

    fg = lb + (1.0 - lb) * jax.nn.sigmoid(f.astype(f32))
    qh = heads(jax.nn.silu(q.astype(f32)))
    kh = heads(1.0 - fg)
    vh = heads(i.astype(f32))
    b = jnp.cumsum(heads(jnp.log(fg)), axis=3)
    b_last = b[:, :, :, -1:, :]
    b_mid = b[:, :, :, C // 2:C // 2 + 1, :]

    a = jnp.einsum('bhntk,bhnsk->bhnts', qh * jnp.exp(b - b_mid), kh * jnp.exp(b_mid - b))
    causal = jnp.tril(jnp.ones((C, C), dtype=bool))
    a = jnp.where(causal, a, 0.0)
    o_intra = jnp.einsum('bhnts,bhnsv->bhntv', a, vh)

    incr = jnp.einsum('bhnsk,bhnsv->bhnkv', kh * jnp.exp(b_last - b), vh)
    decay = jnp.exp(b_last[:, :, :, 0, :])

    def step(state, xs):
        inc_n, d_n = xs
        return d_n[..., None] * state + inc_n, state

    s0 = jnp.zeros((B, H, K, K), f32)
    _, s_start = lax.scan(step, s0, (jnp.moveaxis(incr, 2, 0), jnp.moveaxis(decay, 2, 0)))
    s_start = jnp.moveaxis(s_start, 0, 2)
    o_inter = jnp.einsum('bhntk,bhnkv->bhntv', qh * jnp.exp(b), s_start)

    o = (o_intra + o_inter).transpose(0, 2, 3, 1, 4).reshape(B, S, H, K)
    o = rms_norm(o, norm_g.reshape(H, K)).reshape(B, S, D_HGRN)
    o = o * jax.nn.silu(g.astype(f32))
    return o.astype(q.dtype)


def cross_attention(hn, mem_n, w_q, w_k, w_v, w_o):
    B, S, _ = hn.shape
    M = mem_n.shape[1]
    q = (hn @ w_q).reshape(B, S, X_HEADS, X_HEAD_DIM)
    k = (mem_n @ w_k).reshape(B, M, X_HEADS, X_HEAD_DIM)
    v = (mem_n @ w_v).reshape(B, M, X_HEADS, X_HEAD_DIM)
    s = jnp.einsum('bqhd,bmhd->bhqm', q, k).astype(jnp.float32) * (X_HEAD_DIM ** -0.5)
    p = jax.nn.softmax(s, axis=-1).astype(v.dtype)
    o = jnp.einsum('bhqm,bmhd->bqhd', p, v).reshape(B, S, D_MODEL)
    return o @ w_o


def hier_moe(xn, w_group, b_group, w_router, b_router, w1, w3, w2):
    B, S, D = xn.shape
    T = B * S
    f32 = jnp.float32
    xt = xn.reshape(T, D)
    g_logits = (xt @ w_group).astype(f32) + b_group.astype(f32)
    g_prob = jax.nn.softmax(g_logits, axis=-1)
    grp = jnp.argmax(g_logits, axis=-1).astype(jnp.int32)
    p_grp = jnp.take_along_axis(g_prob, grp[:, None], axis=-1)[:, 0]
    e_all = jnp.einsum('td,gde->tge', xt, w_router).astype(f32) + b_router.astype(f32)
    e_logits = jnp.take_along_axis(e_all, grp[:, None, None], axis=1)[:, 0]
    top_v, top_i = lax.top_k(e_logits, TOP_K)
    gate = p_grp[:, None] * jax.nn.softmax(top_v, axis=-1)
    eid = grp[:, None] * EXPERTS_PER_GROUP + top_i.astype(jnp.int32)

    A = T * TOP_K
    eid_f = eid.reshape(A)
    tok_f = jnp.repeat(jnp.arange(T, dtype=jnp.int32), TOP_K)
    gate_f = gate.reshape(A)
    order = jnp.argsort(eid_f, stable=True)
    se = eid_f[order]
    counts = jnp.bincount(eid_f, length=N_EXPERTS)
    start = jnp.cumsum(counts) - counts
    pcounts = (counts + MOE_BLOCK - 1) // MOE_BLOCK * MOE_BLOCK
    pend = jnp.cumsum(pcounts)
    pstart = pend - pcounts
    dest = pstart[se] + jnp.arange(A, dtype=jnp.int32) - start[se]
    n_blocks = -(-(A + N_EXPERTS * (MOE_BLOCK - 1)) // MOE_BLOCK)
    P = n_blocks * MOE_BLOCK
    slot_tok = jnp.full((P,), T, jnp.int32).at[dest].set(tok_f[order])
    slot_gate = jnp.zeros((P,), f32).at[dest].set(gate_f[order])
    block_eid = jnp.minimum(
        jnp.searchsorted(pend, jnp.arange(n_blocks) * MOE_BLOCK, side='right'), N_EXPERTS - 1)

    x_pad = jnp.concatenate([xt, jnp.zeros((1, D), xt.dtype)], axis=0)
    xb = x_pad[slot_tok].reshape(n_blocks, MOE_BLOCK, D)

    def expert_block(args):
        e, xblk = args
        hdn = jax.nn.silu(xblk @ w1[e]) * (xblk @ w3[e])
        return hdn @ w2[e]

    yb = lax.map(expert_block, (block_eid, xb)).reshape(P, D)
    out = jnp.zeros((T + 1, D), xn.dtype).at[slot_tok].add(yb * slot_gate[:, None].astype(yb.dtype))
    return out[:T].reshape(B, S, D)


def setup_inputs(seed: int = 0) -> dict:
    key = jax.random.key(seed)
    ks = jax.random.split(key, 32)
    L = DEPTH
    nrm = jax.random.normal
    f32 = jnp.float32

    def gain(k, shape):
        return 1.0 + 0.02 * nrm(k, shape, f32)

    return {
        "x": nrm(ks[0], (BATCH, SEQ, D_MODEL), f32),
        "mem": nrm(ks[1], (BATCH, N_MEM, D_MODEL), f32),
        "n_mix": gain(ks[2], (L, D_MODEL)),
        "w_in": nrm(ks[3], (L, D_MODEL, D_IN), f32) * D_MODEL ** -0.5,
        "gmlp_ln_g": gain(ks[4], (L, D_GMLP)),
        "gmlp_ln_b": 0.02 * nrm(ks[5], (L, D_GMLP), f32),
        "gmlp_w_s": nrm(ks[6], (L, GMLP_HEADS, GMLP_BLOCK, GMLP_BLOCK), f32) * GMLP_BLOCK ** -0.5,
        "gmlp_b_s": 1.0 + 0.1 * nrm(ks[7], (L, GMLP_HEADS, GMLP_BLOCK), f32),
        "hgrn_lower_bounds": 0.1 * nrm(ks[8], (L + 1, D_HGRN), f32),
        "hgrn_norm_g": gain(ks[9], (L, D_HGRN)),
        "w_out": nrm(ks[10], (L, D_MIX, D_MODEL), f32) * D_MIX ** -0.5,
        "n_cross": gain(ks[11], (L, D_MODEL)),
        "n_mem": gain(ks[12], (L, D_MODEL)),
        "w_q_x": nrm(ks[13], (L, D_MODEL, D_MODEL), f32) * D_MODEL ** -0.5,
        "w_k_x": nrm(ks[14], (L, D_MODEL, D_MODEL), f32) * D_MODEL ** -0.5,
        "w_v_x": nrm(ks[15], (L, D_MODEL, D_MODEL), f32) * D_MODEL ** -0.5,
        "w_o_x": nrm(ks[16], (L, D_MODEL, D_MODEL), f32) * D_MODEL ** -0.5,
        "n_moe": gain(ks[17], (L, D_MODEL)),
        "w_group": nrm(ks[18], (L, D_MODEL, N_GROUPS), f32) * D_MODEL ** -0.5,
        "b_group": 0.01 * nrm(ks[19], (L, N_GROUPS), f32),
        "w_router": nrm(ks[20], (L, N_GROUPS, D_MODEL, EXPERTS_PER_GROUP), f32) * D_MODEL ** -0.5,
        "b_router": 0.01 * nrm(ks[21], (L, N_GROUPS, EXPERTS_PER_GROUP), f32),
        "w1_e": nrm(ks[22], (L, N_EXPERTS, D_MODEL, D_EXPERT), f32) * D_MODEL ** -0.5,
        "w3_e": nrm(ks[23], (L, N_EXPERTS, D_MODEL, D_EXPERT), f32) * D_MODEL ** -0.5,
        "w2_e": nrm(ks[24], (L, N_EXPERTS, D_EXPERT, D_MODEL), f32) * D_EXPERT ** -0.5,
        "n_final": gain(ks[25], (D_MODEL,)),
    }


def reference(x, mem, n_mix, w_in, gmlp_ln_g, gmlp_ln_b, gmlp_w_s, gmlp_b_s, hgrn_lower_bounds,
              hgrn_norm_g, w_out, n_cross, n_mem, w_q_x, w_k_x, w_v_x, w_o_x, n_moe, w_group,
              b_group, w_router, b_router, w1_e, w3_e, w2_e, n_final):
    split_at = [int(c) for c in np.cumsum(IN_WIDTHS)[:-1]]
    lb_all = jnp.cumsum(jax.nn.softmax(hgrn_lower_bounds.astype(jnp.float32), axis=0), axis=0)
    h = x
    for l in range(DEPTH):
        hn = rms_norm(h, n_mix[l])
        z = hn @ w_in[l]
        u, v, q, f, i, g = jnp.split(z, split_at, axis=-1)
        y_a = gmlp_mixer(jax.nn.gelu(u, approximate=False), jax.nn.gelu(v, approximate=False),
                         gmlp_ln_g[l], gmlp_ln_b[l], gmlp_w_s[l], gmlp_b_s[l])
        y_b = hgrn2_mixer(q, f, i, g, lb_all[l], hgrn_norm_g[l])
        h = h + jnp.concatenate([y_a, y_b], axis=-1) @ w_out[l]
        h = h + cross_attention(rms_norm(h, n_cross[l]), rms_norm(mem, n_mem[l]),
                                w_q_x[l], w_k_x[l], w_v_x[l], w_o_x[l])
        h = h + hier_moe(rms_norm(h, n_moe[l]), w_group[l], b_group[l], w_router[l], b_router[l],
                         w1_e[l], w3_e[l], w2_e[l])
    return rms_norm(h, n_final)
```

```python
import functools

import jax
import jax.numpy as jnp
from jax import lax
from jax.experimental import pallas as pl
from jax.experimental.pallas import tpu as pltpu

f32 = jnp.float32
bf16 = jnp.bfloat16

EPS = 1e-6
D_MODEL = 4096
CHUNK = 64
GMLP_HEADS = 4
GMLP_HEAD_DIM = 512
GMLP_BLOCK = 128
D_GMLP = GMLP_HEADS * GMLP_HEAD_DIM
HGRN_HEAD_DIM = 128
HGRN_HEADS = 16
D_HGRN = HGRN_HEADS * HGRN_HEAD_DIM
X_HEADS = 4
X_HEAD_DIM = 1024
N_GROUPS = 4
EXPERTS_PER_GROUP = 8
N_EXPERTS = N_GROUPS * EXPERTS_PER_GROUP
TOP_K = 2
D_EXPERT = 1024

V7X_LANES = 128
V7X_VMEM_LIMIT_BYTES = 60000 * 1024

MOE_ROWS = 512
MOE_F_TILE = 256
GATHER_ROWS = 256
HGRN_GROUP = 4
HGRN_ROWS = 512


def _params(sem, est_bytes):
    limit = min(int(est_bytes) + (8 << 20), V7X_VMEM_LIMIT_BYTES)
    return pltpu.CompilerParams(dimension_semantics=sem, vmem_limit_bytes=limit)


def _rms_body(x_ref, g_ref, o_ref):
    x = x_ref[...].astype(f32)
    ms = jnp.mean(x * x, axis=-1, keepdims=True)
    o_ref[...] = (x * lax.rsqrt(ms + EPS) * g_ref[...]).astype(o_ref.dtype)


def _rmsnorm(x, g, out_dtype, tm=256):
    M, D = x.shape
    est = 2 * tm * D * (x.dtype.itemsize + jnp.dtype(out_dtype).itemsize) + 2 * tm * D * 4
    return pl.pallas_call(
        _rms_body,
        grid=(M // tm,),
        in_specs=[pl.BlockSpec((tm, D), lambda i: (i, 0)),
                  pl.BlockSpec((1, D), lambda i: (0, 0))],
        out_specs=pl.BlockSpec((tm, D), lambda i: (i, 0)),
        out_shape=jax.ShapeDtypeStruct((M, D), out_dtype),
        compiler_params=_params(("arbitrary",), est),
        name="rmsnorm",
    )(x, g.reshape(1, D).astype(f32))


def _mm_body(a_ref, w_ref, o_ref):
    o_ref[...] = jnp.dot(a_ref[...], w_ref[...], preferred_element_type=f32).astype(o_ref.dtype)


def _matmul(a, w, out_dtype, tm, tn, name):
    M, K = a.shape
    N = w.shape[1]
    osz = jnp.dtype(out_dtype).itemsize
    est = 2 * (tm * K * 2 + K * tn * 2 + tm * tn * osz) + tm * tn * 4
    return pl.pallas_call(
        _mm_body,
        grid=(N // tn, M // tm),
        in_specs=[pl.BlockSpec((tm, K), lambda j, i: (i, 0)),
                  pl.BlockSpec((K, tn), lambda j, i: (0, j))],
        out_specs=pl.BlockSpec((tm, tn), lambda j, i: (i, j)),
        out_shape=jax.ShapeDtypeStruct((M, N), out_dtype),
        compiler_params=_params(("arbitrary", "arbitrary"), est),
        name=name,
    )(a, w)


def _mm_res1_body(r_ref, a_ref, w_ref, o_ref):
    o_ref[...] = r_ref[...] + jnp.dot(a_ref[...], w_ref[...], preferred_element_type=f32)


def _mm_res2_body(r_ref, a1_ref, a2_ref, w1_ref, w2_ref, o_ref):
    acc = jnp.dot(a1_ref[...], w1_ref[...], preferred_element_type=f32)
    acc = acc + jnp.dot(a2_ref[...], w2_ref[...], preferred_element_type=f32)
    o_ref[...] = r_ref[...] + acc


def _matmul_res(res, a_list, w, tm, tn, name):
    M, N = res.shape
    n = len(a_list)
    Kp = a_list[0].shape[1]
    a_specs = [pl.BlockSpec((tm, Kp), lambda j, i: (i, 0)) for _ in range(n)]
    w_specs = [pl.BlockSpec((Kp, tn), functools.partial(lambda j, i, p: (p, j), p=p)) for p in range(n)]
    est = 2 * (n * tm * Kp * 2 + n * Kp * tn * 2 + 2 * tm * tn * 4) + 2 * tm * tn * 4
    body = _mm_res1_body if n == 1 else _mm_res2_body
    return pl.pallas_call(
        body,
        grid=(N // tn, M // tm),
        in_specs=[pl.BlockSpec((tm, tn), lambda j, i: (i, j))] + a_specs + w_specs,
        out_specs=pl.BlockSpec((tm, tn), lambda j, i: (i, j)),
        out_shape=jax.ShapeDtypeStruct((M, N), f32),
        compiler_params=_params(("arbitrary", "arbitrary"), est),
        name=name,
    )(res, *a_list, *([w] * n))


def _gelu(x):
    return 0.5 * x * (1.0 + lax.erf(x * 0.7071067811865476))


def _gmlp_body(u_ref, v_ref, lng_ref, lnb_ref, ws_ref, bs_ref, o_ref):
    rows = u_ref.shape[0]
    gv = _gelu(v_ref[...].astype(f32))
    mu = jnp.mean(gv, axis=-1, keepdims=True)
    xc = gv - mu
    var = jnp.mean(xc * xc, axis=-1, keepdims=True)
    vn = (xc * lax.rsqrt(var + EPS) * lng_ref[...] + lnb_ref[...]).astype(bf16)
    ci = lax.broadcasted_iota(jnp.int32, (GMLP_BLOCK, GMLP_BLOCK), 0) // CHUNK
    cj = lax.broadcasted_iota(jnp.int32, (GMLP_BLOCK, GMLP_BLOCK), 1) // CHUNK
    w = jnp.where(cj <= ci, ws_ref[0], 0.0).astype(bf16)
    bias = bs_ref[0]
    for blk in range(rows // GMLP_BLOCK):
        sl = slice(blk * GMLP_BLOCK, (blk + 1) * GMLP_BLOCK)
        s = jnp.dot(w, vn[sl], preferred_element_type=f32) + bias
        o_ref[sl, :] = (_gelu(u_ref[sl, :].astype(f32)) * s).astype(o_ref.dtype)


def _gmlp(z, ln_g, ln_b, w_s, b_s, tr=256):
    M = z.shape[0]
    C = GMLP_HEAD_DIM
    est = 2 * 3 * tr * C * 2 + 8 * tr * C * 4
    return pl.pallas_call(
        _gmlp_body,
        grid=(M // tr, GMLP_HEADS),
        in_specs=[pl.BlockSpec((tr, C), lambda i, h: (i, h)),
                  pl.BlockSpec((tr, C), lambda i, h: (i, GMLP_HEADS + h)),
                  pl.BlockSpec((1, C), lambda i, h: (0, h)),
                  pl.BlockSpec((1, C), lambda i, h: (0, h)),
                  pl.BlockSpec((1, GMLP_BLOCK, GMLP_BLOCK), lambda i, h: (h, 0, 0)),
                  pl.BlockSpec((1, GMLP_BLOCK, 1), lambda i, h: (h, 0, 0))],
        out_specs=pl.BlockSpec((tr, C), lambda i, h: (i, h)),
        out_shape=jax.ShapeDtypeStruct((M, D_GMLP), bf16),
        compiler_params=_params(("arbitrary", "arbitrary"), est),
        name="gmlp_mixer",
    )(z, z, ln_g.reshape(1, D_GMLP), ln_b.reshape(1, D_GMLP), w_s,
      b_s.reshape(GMLP_HEADS, GMLP_BLOCK, 1))


def _split3(x):
    hi = x.astype(bf16)
    r1 = x - hi.astype(f32)
    mid = r1.astype(bf16)
    lo = (r1 - mid.astype(f32)).astype(bf16)
    return hi, mid, lo


def _hgrn_body(q_ref, f_ref, i_ref, g_ref, lbr_ref, ng_ref, o_ref, st_ref):
    C, K = CHUNK, HGRN_HEAD_DIM

    @pl.when(pl.program_id(2) == 0)
    def _():
        st_ref[...] = jnp.zeros_like(st_ref)

    r0 = lbr_ref[0:1, :]
    r1 = lbr_ref[1:2, :]
    m = jnp.maximum(r0, r1)
    e0 = jnp.exp(r0 - m)
    e1 = jnp.exp(r1 - m)
    lb_all = e0 / (e0 + e1)
    ng_all = ng_ref[...]

    ti = lax.broadcasted_iota(jnp.int32, (C, C), 0)
    si = lax.broadcasted_iota(jnp.int32, (C, C), 1)
    causal = si <= ti
    tril = causal.astype(bf16)

    def chunk(c, carry):
        rs = pl.ds(pl.multiple_of(c * C, C), C)
        for h in range(HGRN_GROUP):
            ls = slice(h * K, (h + 1) * K)
            lb = lb_all[:, ls]
            fg = lb + (1.0 - lb) * jax.nn.sigmoid(f_ref[rs, ls].astype(f32))
            qh = jax.nn.silu(q_ref[rs, ls].astype(f32))
            kh = 1.0 - fg
            vh = i_ref[rs, ls].astype(bf16)
            hi, mid, lo = _split3(jnp.log(fg))
            b = (jnp.dot(tril, hi, preferred_element_type=f32)
                 + jnp.dot(tril, mid, preferred_element_type=f32)
                 + jnp.dot(tril, lo, preferred_element_type=f32))
            b_mid = b[C // 2:C // 2 + 1, :]
            b_last = b[C - 1:C, :]
            q_in = (qh * jnp.exp(b - b_mid)).astype(bf16)
            k_in = (kh * jnp.exp(b_mid - b)).astype(bf16)
            a = lax.dot_general(q_in, k_in, (((1,), (1,)), ((), ())), preferred_element_type=f32)
            a = jnp.where(causal, a, 0.0).astype(bf16)
            o = jnp.dot(a, vh, preferred_element_type=f32)
            st = st_ref[h]
            q_ex = (qh * jnp.exp(b)).astype(bf16)
            o = o + lax.dot_general(q_ex, st.astype(bf16), (((1,), (1,)), ((), ())),
                                    preferred_element_type=f32)
            k_dec = (kh * jnp.exp(b_last - b)).astype(bf16)
            incr = lax.dot_general(vh, k_dec, (((0,), (0,)), ((), ())), preferred_element_type=f32)
            st_ref[h] = st * jnp.exp(b_last) + incr
            ms = jnp.mean(o * o, axis=-1, keepdims=True)
            o = o * lax.rsqrt(ms + EPS) * ng_all[:, ls]
            o = o * jax.nn.silu(g_ref[rs, ls].astype(f32))
            o_ref[rs, ls] = o.astype(o_ref.dtype)
        return carry

    lax.fori_loop(0, q_ref.shape[0] // C, chunk, 0)


def _hgrn(z, lower_bounds, norm_g, batch, seq):
    M = z.shape[0]
    W = HGRN_GROUP * HGRN_HEAD_DIM
    ts = HGRN_ROWS
    n_s = seq // ts
    n_hg = D_HGRN // W
    col0 = 2 * D_GMLP // W

    def zspec(part):
        return pl.BlockSpec((ts, W), lambda b, hg, s, part=part: (b * n_s + s, col0 + part * n_hg + hg))

    est = 2 * 5 * ts * W * 2 + HGRN_GROUP * 128 * 128 * 4 + (4 << 20)
    return pl.pallas_call(
        _hgrn_body,
        grid=(batch, n_hg, n_s),
        in_specs=[zspec(0), zspec(1), zspec(2), zspec(3),
                  pl.BlockSpec((2, W), lambda b, hg, s: (0, hg)),
                  pl.BlockSpec((1, W), lambda b, hg, s: (0, hg))],
        out_specs=pl.BlockSpec((ts, W), lambda b, hg, s: (b * n_s + s, hg)),
        out_shape=jax.ShapeDtypeStruct((M, D_HGRN), bf16),
        scratch_shapes=[pltpu.VMEM((HGRN_GROUP, HGRN_HEAD_DIM, HGRN_HEAD_DIM), f32)],
        compiler_params=_params(("arbitrary", "arbitrary", "arbitrary"), est),
        name="hgrn2_mixer",
    )(z, z, z, z, lower_bounds.astype(f32), norm_g.reshape(1, D_HGRN).astype(f32))


def _xattn_body(h_ref, wq_ref, k_ref, v_ref, o_ref):
    q = jnp.dot(h_ref[...], wq_ref[...], preferred_element_type=f32).astype(bf16)
    s = lax.dot_general(q, k_ref[...], (((1,), (1,)), ((), ())), preferred_element_type=f32)
    s = s * (X_HEAD_DIM ** -0.5)
    s = s - jnp.max(s, axis=-1, keepdims=True)
    p = jnp.exp(s)
    p = p / jnp.sum(p, axis=-1, keepdims=True)
    o_ref[...] = jnp.dot(p.astype(bf16), v_ref[...], preferred_element_type=f32).astype(o_ref.dtype)


def _xattn(hc, wq, k, v, batch, seq, n_mem, tq=512):
    M, D = hc.shape
    n_s = seq // tq
    dh = X_HEAD_DIM
    est = 2 * (tq * D * 2 + D * dh * 2 + 2 * n_mem * dh * 2 + tq * dh * 2) + 3 * tq * dh * 4
    return pl.pallas_call(
        _xattn_body,
        grid=(X_HEADS, batch, n_s),
        in_specs=[pl.BlockSpec((tq, D), lambda h, b, s: (b * n_s + s, 0)),
                  pl.BlockSpec((D, dh), lambda h, b, s: (0, h)),
                  pl.BlockSpec((n_mem, dh), lambda h, b, s: (b, h)),
                  pl.BlockSpec((n_mem, dh), lambda h, b, s: (b, h))],
        out_specs=pl.BlockSpec((tq, dh), lambda h, b, s: (b * n_s + s, h)),
        out_shape=jax.ShapeDtypeStruct((M, D), bf16),
        compiler_params=_params(("arbitrary", "arbitrary", "arbitrary"), est),
        name="xattn_qproj",
    )(hc, wq, k, v)


def _router_body(h_ref, g_ref, wr_ref, br_ref, o_ref):
    x = h_ref[...]
    ms = jnp.mean(x * x, axis=-1, keepdims=True)
    xn = x * lax.rsqrt(ms + EPS) * g_ref[...]
    xh, xm, xl = _split3(xn)
    wh, wm, wl = _split3(wr_ref[...])
    dot = functools.partial(jnp.dot, preferred_element_type=f32)
    acc = dot(xl, wh) + dot(xh, wl) + dot(xm, wm)
    acc = acc + dot(xm, wh) + dot(xh, wm)
    acc = acc + dot(xh, wh)
    o_ref[...] = acc + br_ref[...]


def _router_logits(h, n_moe, w_group, b_group, w_router, b_router, tm=256):
    M, D = h.shape
    n_log = N_GROUPS + N_EXPERTS
    wr = jnp.concatenate([w_group] + [w_router[g] for g in range(N_GROUPS)], axis=1)
    wr = jnp.pad(wr.astype(f32), ((0, 0), (0, V7X_LANES - n_log)))
    br = jnp.pad(jnp.concatenate([b_group, b_router.reshape(-1)]).astype(f32), (0, V7X_LANES - n_log))
    est = 2 * tm * D * 4 + 2 * D * V7X_LANES * 4 + 6 * tm * D * 4
    return pl.pallas_call(
        _router_body,
        grid=(M // tm,),
        in_specs=[pl.BlockSpec((tm, D), lambda i: (i, 0)),
                  pl.BlockSpec((1, D), lambda i: (0, 0)),
                  pl.BlockSpec((D, V7X_LANES), lambda i: (0, 0)),
                  pl.BlockSpec((1, V7X_LANES), lambda i: (0, 0))],
        out_specs=pl.BlockSpec((tm, V7X_LANES), lambda i: (i, 0)),
        out_shape=jax.ShapeDtypeStruct((M, V7X_LANES), f32),
        compiler_params=_params(("arbitrary",), est),
        name="moe_router",
    )(h, n_moe.reshape(1, D).astype(f32), wr, br.reshape(1, V7X_LANES))


def _row_copy(src_hbm, row, dst, sem):
    return pltpu.make_async_copy(src_hbm.at[pl.ds(row, 1)], dst, sem)


def _gather_body(tok_ref, nrows_ref, h_hbm, g_ref, o_ref, buf, sem):
    R = GATHER_ROWS
    i = pl.program_id(0)
    n_steps = pl.num_programs(0)

    def start(step, slot):
        @pl.when(step * R < nrows_ref[0])
        def _():
            def issue(r, c):
                _row_copy(h_hbm, tok_ref[step * R + r], buf.at[slot, pl.ds(r, 1)], sem.at[slot]).start()
                return c
            lax.fori_loop(0, R, issue, 0)

    @pl.when(i == 0)
    def _():
        start(0, 0)

    @pl.when(i + 1 < n_steps)
    def _():
        start(i + 1, (i + 1) % 2)

    @pl.when(i * R < nrows_ref[0])
    def _():
        slot = i % 2
        pltpu.make_async_copy(h_hbm.at[pl.ds(0, R)], buf.at[slot], sem.at[slot]).wait()
        x = buf[slot]
        ms = jnp.mean(x * x, axis=-1, keepdims=True)
        o_ref[...] = (x * lax.rsqrt(ms + EPS) * g_ref[...]).astype(o_ref.dtype)


def _moe_gather(h, n_moe, slot_tok, n_rows_used, P):
    T, D = h.shape
    R = GATHER_ROWS
    n_steps = P // R

    def omap(i, tok, nrows):
        last = jnp.maximum(nrows[0] // R - 1, 0)
        return (jnp.minimum(i, last), 0)

    est = 2 * R * D * 4 + 2 * R * D * 2 + 3 * R * D * 4
    return pl.pallas_call(
        _gather_body,
        grid_spec=pltpu.PrefetchScalarGridSpec(
            num_scalar_prefetch=2,
            grid=(n_steps,),
            in_specs=[pl.BlockSpec(memory_space=pl.ANY),
                      pl.BlockSpec((1, D), lambda i, tok, nrows: (0, 0))],
            out_specs=pl.BlockSpec((R, D), omap),
            scratch_shapes=[pltpu.VMEM((2, R, D), f32), pltpu.SemaphoreType.DMA((2,))],
        ),
        out_shape=jax.ShapeDtypeStruct((P, D), bf16),
        compiler_params=_params(("arbitrary",), est),
        name="moe_gather",
    )(slot_tok, n_rows_used, h, n_moe.reshape(1, D).astype(f32))


def _moe_mlp_body(eid_ref, nblk_ref, x_ref, w1_ref, w3_ref, w2_ref, o_ref):
    b = pl.program_id(0)
    f = pl.program_id(1)

    @pl.when(b < nblk_ref[0])
    def _():
        x = x_ref[...]
        h1 = jnp.dot(x, w1_ref[...], preferred_element_type=f32)
        h3 = jnp.dot(x, w3_ref[...], preferred_element_type=f32)
        hid = (jax.nn.silu(h1) * h3).astype(bf16)
        n_col = 1024
        for c in range(D_MODEL // n_col):
            cs = slice(c * n_col, (c + 1) * n_col)
            part = jnp.dot(hid, w2_ref[:, cs], preferred_element_type=f32)

            @pl.when(f == 0)
            def _():
                o_ref[:, cs] = part

            @pl.when(f != 0)
            def _():
                o_ref[:, cs] += part


def _moe_mlp(xs, w1, w3, w2, block_eid, n_blocks_used):
    P, D = xs.shape
    Bm, Ft = MOE_ROWS, MOE_F_TILE
    nb = P // Bm
    nf = D_EXPERT // Ft

    def blk(b, nblk):
        return jnp.minimum(b, nblk[0] - 1)

    def ftile(b, f, nblk):
        return jnp.where(b < nblk[0], f, nf - 1)

    est = 2 * (Bm * D * 2 + 3 * D * Ft * 2 + Bm * D * 4) + 3 * Bm * 1024 * 4
    return pl.pallas_call(
        _moe_mlp_body,
        grid_spec=pltpu.PrefetchScalarGridSpec(
            num_scalar_prefetch=2,
            grid=(nb, nf),
            in_specs=[pl.BlockSpec((Bm, D), lambda b, f, eid, nblk: (blk(b, nblk), 0)),
                      pl.BlockSpec((None, D, Ft), lambda b, f, eid, nblk: (eid[blk(b, nblk)], 0, ftile(b, f, nblk))),
                      pl.BlockSpec((None, D, Ft), lambda b, f, eid, nblk: (eid[blk(b, nblk)], 0, ftile(b, f, nblk))),
                      pl.BlockSpec((None, Ft, D), lambda b, f, eid, nblk: (eid[blk(b, nblk)], ftile(b, f, nblk), 0))],
            out_specs=pl.BlockSpec((Bm, D), lambda b, f, eid, nblk: (blk(b, nblk), 0)),
        ),
        out_shape=jax.ShapeDtypeStruct((P, D), f32),
        compiler_params=_params(("arbitrary", "arbitrary"), est),
        name="moe_expert_mlp",
    )(block_eid, n_blocks_used, xs, w1, w3, w2)


def _combine_body(dest_ref, h_ref, gate_ref, g_ref, y_hbm, o_ref, buf, sem):
    R = GATHER_ROWS
    i = pl.program_id(0)
    n_steps = pl.num_programs(0)

    def start(step, slot):
        def issue(r, c):
            for k in range(TOP_K):
                _row_copy(y_hbm, dest_ref[(step * R + r) * TOP_K + k],
                          buf.at[slot, k, pl.ds(r, 1)], sem.at[slot]).start()
            return c
        lax.fori_loop(0, R, issue, 0)

    @pl.when(i == 0)
    def _():
        start(0, 0)

    @pl.when(i + 1 < n_steps)
    def _():
        start(i + 1, (i + 1) % 2)

    slot = i % 2
    for k in range(TOP_K):
        pltpu.make_async_copy(y_hbm.at[pl.ds(0, R)], buf.at[slot, k], sem.at[slot]).wait()
    gate = gate_ref[...]
    h = h_ref[...] + buf[slot, 0] * gate[:, 0:1] + buf[slot, 1] * gate[:, 1:2]
    ms = jnp.mean(h * h, axis=-1, keepdims=True)
    o_ref[...] = h * lax.rsqrt(ms + EPS) * g_ref[...]


def _moe_combine(h, y, dest, gate, n_final):
    T, D = h.shape
    R = GATHER_ROWS
    est = 2 * TOP_K * R * D * 4 + 4 * R * D * 4 + 3 * R * D * 4
    return pl.pallas_call(
        _combine_body,
        grid_spec=pltpu.PrefetchScalarGridSpec(
            num_scalar_prefetch=1,
            grid=(T // R,),
            in_specs=[pl.BlockSpec((R, D), lambda i, dest: (i, 0)),
                      pl.BlockSpec((R, TOP_K), lambda i, dest: (i, 0)),
                      pl.BlockSpec((1, D), lambda i, dest: (0, 0)),
                      pl.BlockSpec(memory_space=pl.ANY)],
            out_specs=pl.BlockSpec((R, D), lambda i, dest: (i, 0)),
            scratch_shapes=[pltpu.VMEM((2, TOP_K, R, D), f32), pltpu.SemaphoreType.DMA((2,))],
        ),
        out_shape=jax.ShapeDtypeStruct((T, D), f32),
        compiler_params=_params(("arbitrary",), est),
        name="moe_combine_norm",
    )(dest, h, gate, n_final.reshape(1, D).astype(f32), y)


def _route(logits):
    T = logits.shape[0]
    A = T * TOP_K
    g_logits = logits[:, :N_GROUPS]
    e_all = logits[:, N_GROUPS:N_GROUPS + N_EXPERTS].reshape(T, N_GROUPS, EXPERTS_PER_GROUP)
    g_prob = jax.nn.softmax(g_logits, axis=-1)
    grp = jnp.argmax(g_logits, axis=-1).astype(jnp.int32)
    p_grp = jnp.take_along_axis(g_prob, grp[:, None], axis=-1)[:, 0]
    e_logits = jnp.take_along_axis(e_all, grp[:, None, None], axis=1)[:, 0]
    top_v, top_i = lax.top_k(e_logits, TOP_K)
    gate = p_grp[:, None] * jax.nn.softmax(top_v, axis=-1)
    eid = (grp[:, None] * EXPERTS_PER_GROUP + top_i.astype(jnp.int32)).reshape(A)

    onehot = (eid[:, None] == jnp.arange(N_EXPERTS, dtype=jnp.int32)[None, :]).astype(jnp.int32)
    csum = jnp.cumsum(onehot, axis=0)
    rank = jnp.sum(onehot * csum, axis=1) - 1
    counts = csum[-1]
    pcounts = (counts + MOE_ROWS - 1) // MOE_ROWS * MOE_ROWS
    pend = jnp.cumsum(pcounts)
    pstart = pend - pcounts
    dest = (pstart[eid] + rank).astype(jnp.int32)
    n_blocks = -(-(A + N_EXPERTS * (MOE_ROWS - 1)) // MOE_ROWS)
    P = n_blocks * MOE_ROWS
    tok = jnp.arange(A, dtype=jnp.int32) // TOP_K
    slot_tok = jnp.zeros((P,), jnp.int32).at[dest].set(tok)
    block_eid = jnp.minimum(
        jnp.searchsorted(pend, jnp.arange(n_blocks, dtype=jnp.int32) * MOE_ROWS, side='right'),
        N_EXPERTS - 1).astype(jnp.int32)
    n_rows_used = pend[-1:].astype(jnp.int32)
    return gate, dest, slot_tok, block_eid, n_rows_used, P


def kernel(x, mem, n_mix, w_in, gmlp_ln_g, gmlp_ln_b, gmlp_w_s, gmlp_b_s, hgrn_lower_bounds,
           hgrn_norm_g, w_out, n_cross, n_mem, w_q_x, w_k_x, w_v_x, w_o_x, n_moe, w_group,
           b_group, w_router, b_router, w1_e, w3_e, w2_e, n_final):
    B, S, D = x.shape
    n_mem_tok = mem.shape[1]
    T = B * S
    xt = x.reshape(T, D)

    hn = _rmsnorm(xt, n_mix[0], bf16)
    z = _matmul(hn, w_in[0].astype(bf16), bf16, 1024, 1024, "mm_w_in")
    ya = _gmlp(z, gmlp_ln_g[0], gmlp_ln_b[0], gmlp_w_s[0].astype(f32), gmlp_b_s[0].astype(f32))
    yb = _hgrn(z, hgrn_lower_bounds, hgrn_norm_g[0], B, S)
    h1 = _matmul_res(xt, [ya, yb], w_out[0].astype(bf16), 512, 1024, "mm_w_out")

    hc = _rmsnorm(h1, n_cross[0], bf16)
    mn = _rmsnorm(mem.reshape(B * n_mem_tok, D), n_mem[0], bf16)
    kx = _matmul(mn, w_k_x[0].astype(bf16), bf16, B * n_mem_tok, 1024, "mm_w_k")
    vx = _matmul(mn, w_v_x[0].astype(bf16), bf16, B * n_mem_tok, 1024, "mm_w_v")
    ox = _xattn(hc, w_q_x[0].astype(bf16), kx, vx, B, S, n_mem_tok)
    h2 = _matmul_res(h1, [ox], w_o_x[0].astype(bf16), 512, 1024, "mm_w_o")

    logits = _router_logits(h2, n_moe[0], w_group[0], b_group[0], w_router[0], b_router[0])
    gate, dest, slot_tok, block_eid, n_rows_used, P = _route(logits)
    xs = _moe_gather(h2, n_moe[0], slot_tok, n_rows_used, P)
    y = _moe_mlp(xs, w1_e[0].astype(bf16), w3_e[0].astype(bf16), w2_e[0].astype(bf16),
                 block_eid, n_rows_used // MOE_ROWS)
    out = _moe_combine(h2, y, dest, gate.astype(f32), n_final)
    return out.reshape(B, S, D)
```

```python
import functools

import jax
import jax.numpy as jnp
from jax import lax
from jax.experimental import pallas as pl
from jax.experimental.pallas import tpu as pltpu

f32 = jnp.float32
bf16 = jnp.bfloat16

EPS = 1e-6
D_MODEL = 4096
CHUNK = 64
GMLP_HEADS = 4
GMLP_HEAD_DIM = 512
GMLP_BLOCK = 128
D_GMLP = GMLP_HEADS * GMLP_HEAD_DIM
HGRN_HEAD_DIM = 128
HGRN_HEADS = 16
D_HGRN = HGRN_HEADS * HGRN_HEAD_DIM
X_HEADS = 4
X_HEAD_DIM = 1024
N_GROUPS = 4
EXPERTS_PER_GROUP = 8
N_EXPERTS = N_GROUPS * EXPERTS_PER_GROUP
TOP_K = 2
D_EXPERT = 1024

V7X_LANES = 128
V7X_VMEM_LIMIT_BYTES = 60000 * 1024

MOE_ROWS = 512
MOE_F_TILE = 512
MOE_N_TILE = 2048
GATHER_ROWS = 256
HGRN_GROUP = 4
HGRN_ROWS = 512


def _params(sem, est_bytes):
    limit = min(int(est_bytes) + (8 << 20), V7X_VMEM_LIMIT_BYTES)
    return pltpu.CompilerParams(dimension_semantics=sem, vmem_limit_bytes=limit)


def _rms_body(x_ref, g_ref, o_ref):
    x = x_ref[...].astype(f32)
    ms = jnp.mean(x * x, axis=-1, keepdims=True)
    o_ref[...] = (x * lax.rsqrt(ms + EPS) * g_ref[...]).astype(o_ref.dtype)


def _rmsnorm(x, g, out_dtype, tm=256):
    M, D = x.shape
    est = 2 * tm * D * (x.dtype.itemsize + jnp.dtype(out_dtype).itemsize) + 2 * tm * D * 4
    return pl.pallas_call(
        _rms_body,
        grid=(M // tm,),
        in_specs=[pl.BlockSpec((tm, D), lambda i: (i, 0)),
                  pl.BlockSpec((1, D), lambda i: (0, 0))],
        out_specs=pl.BlockSpec((tm, D), lambda i: (i, 0)),
        out_shape=jax.ShapeDtypeStruct((M, D), out_dtype),
        compiler_params=_params(("arbitrary",), est),
        name="rmsnorm",
    )(x, g.reshape(1, D).astype(f32))


CAST_ROWS = 512


def _cast_tile(src_ref, dst_ref, dst_row0=0):
    n = src_ref.shape[0] // CAST_ROWS

    def piece(p, c):
        r = pl.multiple_of(p * CAST_ROWS, CAST_ROWS)
        d = pl.multiple_of(dst_row0 + p * CAST_ROWS, CAST_ROWS)
        dst_ref[pl.ds(d, CAST_ROWS), :] = src_ref[pl.ds(r, CAST_ROWS), :].astype(bf16)
        return c

    lax.fori_loop(0, n, piece, 0)


def _mm_body(a_ref, w_ref, o_ref, wb_ref):
    @pl.when(pl.program_id(1) == 0)
    def _():
        _cast_tile(w_ref, wb_ref)

    o_ref[...] = jnp.dot(a_ref[...], wb_ref[...], preferred_element_type=f32).astype(o_ref.dtype)


def _matmul(a, w, out_dtype, tm, tn, name):
    M, K = a.shape
    N = w.shape[1]
    osz = jnp.dtype(out_dtype).itemsize
    est = 2 * (tm * K * 2 + K * tn * 4 + tm * tn * osz) + K * tn * 2 + tm * tn * 4
    return pl.pallas_call(
        _mm_body,
        grid=(N // tn, M // tm),
        in_specs=[pl.BlockSpec((tm, K), lambda j, i: (i, 0)),
                  pl.BlockSpec((K, tn), lambda j, i: (0, j))],
        out_specs=pl.BlockSpec((tm, tn), lambda j, i: (i, j)),
        out_shape=jax.ShapeDtypeStruct((M, N), out_dtype),
        scratch_shapes=[pltpu.VMEM((K, tn), bf16)],
        compiler_params=_params(("arbitrary", "arbitrary"), est),
        name=name,
    )(a, w)


def _mm_res_body(n_parts, r_ref, *refs):
    a_refs = refs[:n_parts]
    w_refs = refs[n_parts:2 * n_parts]
    o_ref, wb_ref = refs[2 * n_parts:]
    kp = w_refs[0].shape[0]

    @pl.when(pl.program_id(1) == 0)
    def _():
        for p in range(n_parts):
            _cast_tile(w_refs[p], wb_ref, p * kp)

    acc = r_ref[...]
    for p in range(n_parts):
        acc = acc + jnp.dot(a_refs[p][...], wb_ref[p * kp:(p + 1) * kp, :], preferred_element_type=f32)
    o_ref[...] = acc


def _matmul_res(res, a_list, w, tm, tn, name):
    M, N = res.shape
    n = len(a_list)
    Kp = a_list[0].shape[1]
    a_specs = [pl.BlockSpec((tm, Kp), lambda j, i: (i, 0)) for _ in range(n)]
    w_specs = [pl.BlockSpec((Kp, tn), functools.partial(lambda j, i, p: (p, j), p=p)) for p in range(n)]
    est = 2 * (n * tm * Kp * 2 + n * Kp * tn * 4 + 2 * tm * tn * 4) + n * Kp * tn * 2 + 2 * tm * tn * 4
    return pl.pallas_call(
        functools.partial(_mm_res_body, n),
        grid=(N // tn, M // tm),
        in_specs=[pl.BlockSpec((tm, tn), lambda j, i: (i, j))] + a_specs + w_specs,
        out_specs=pl.BlockSpec((tm, tn), lambda j, i: (i, j)),
        out_shape=jax.ShapeDtypeStruct((M, N), f32),
        scratch_shapes=[pltpu.VMEM((n * Kp, tn), bf16)],
        compiler_params=_params(("arbitrary", "arbitrary"), est),
        name=name,
    )(res, *a_list, *([w] * n))


def _gelu(x):
    return 0.5 * x * (1.0 + lax.erf(x * 0.7071067811865476))


def _gmlp_body(u_ref, v_ref, lng_ref, lnb_ref, ws_ref, bs_ref, o_ref):
    rows = u_ref.shape[0]
    gv = _gelu(v_ref[...].astype(f32))
    mu = jnp.mean(gv, axis=-1, keepdims=True)
    xc = gv - mu
    var = jnp.mean(xc * xc, axis=-1, keepdims=True)
    vn = (xc * lax.rsqrt(var + EPS) * lng_ref[...] + lnb_ref[...]).astype(bf16)
    ci = lax.broadcasted_iota(jnp.int32, (GMLP_BLOCK, GMLP_BLOCK), 0) // CHUNK
    cj = lax.broadcasted_iota(jnp.int32, (GMLP_BLOCK, GMLP_BLOCK), 1) // CHUNK
    w = jnp.where(cj <= ci, ws_ref[0], 0.0).astype(bf16)
    bias = bs_ref[0]
    for blk in range(rows // GMLP_BLOCK):
        sl = slice(blk * GMLP_BLOCK, (blk + 1) * GMLP_BLOCK)
        s = jnp.dot(w, vn[sl], preferred_element_type=f32) + bias
        o_ref[sl, :] = (_gelu(u_ref[sl, :].astype(f32)) * s).astype(o_ref.dtype)


def _gmlp(z, ln_g, ln_b, w_s, b_s, tr=256):
    M = z.shape[0]
    C = GMLP_HEAD_DIM
    est = 2 * 3 * tr * C * 2 + 8 * tr * C * 4
    return pl.pallas_call(
        _gmlp_body,
        grid=(M // tr, GMLP_HEADS),
        in_specs=[pl.BlockSpec((tr, C), lambda i, h: (i, h)),
                  pl.BlockSpec((tr, C), lambda i, h: (i, GMLP_HEADS + h)),
                  pl.BlockSpec((1, C), lambda i, h: (0, h)),
                  pl.BlockSpec((1, C), lambda i, h: (0, h)),
                  pl.BlockSpec((1, GMLP_BLOCK, GMLP_BLOCK), lambda i, h: (h, 0, 0)),
                  pl.BlockSpec((1, GMLP_BLOCK, 1), lambda i, h: (h, 0, 0))],
        out_specs=pl.BlockSpec((tr, C), lambda i, h: (i, h)),
        out_shape=jax.ShapeDtypeStruct((M, D_GMLP), bf16),
        compiler_params=_params(("arbitrary", "arbitrary"), est),
        name="gmlp_mixer",
    )(z, z, ln_g.reshape(1, D_GMLP), ln_b.reshape(1, D_GMLP), w_s,
      b_s.reshape(GMLP_HEADS, GMLP_BLOCK, 1))


def _split3(x):
    hi = x.astype(bf16)
    r1 = x - hi.astype(f32)
    mid = r1.astype(bf16)
    lo = (r1 - mid.astype(f32)).astype(bf16)
    return hi, mid, lo


HGRN_STRIP = 2 * CHUNK
_NT = (((1,), (1,)), ((), ()))
_TN = (((0,), (0,)), ((), ()))


def _hgrn_body(q_ref, f_ref, i_ref, g_ref, lbr_ref, ng_ref, o_ref,
               st_ref, qin_s, kin_s, qex_s, kdec_s, dec_s, am_s, xq_s, xk_s, inc_s, smat_s, oacc_s):
    C, K, G = CHUNK, HGRN_HEAD_DIM, HGRN_GROUP
    rows, W = q_ref.shape
    n_chunks = rows // C

    @pl.when(pl.program_id(2) == 0)
    def _():
        st_ref[...] = jnp.zeros_like(st_ref)

    r0 = lbr_ref[0:1, :]
    r1 = lbr_ref[1:2, :]
    m = jnp.maximum(r0, r1)
    e0 = jnp.exp(r0 - m)
    e1 = jnp.exp(r1 - m)
    lb = e0 / (e0 + e1)

    ti = lax.broadcasted_iota(jnp.int32, (C, C), 0)
    si = lax.broadcasted_iota(jnp.int32, (C, C), 1)
    tril = (si <= ti).astype(bf16)

    def chunk_rows(c):
        return pl.ds(pl.multiple_of(c * C, C), C)

    def gates(c, carry):
        rs = chunk_rows(c)
        fg = lb + (1.0 - lb) * jax.nn.sigmoid(f_ref[rs, :].astype(f32))
        hi, mid, lo = _split3(jnp.log(fg))
        b = (jnp.dot(tril, hi, preferred_element_type=f32)
             + jnp.dot(tril, mid, preferred_element_type=f32)
             + jnp.dot(tril, lo, preferred_element_type=f32))
        b_mid = b[C // 2:C // 2 + 1, :]
        b_last = b[C - 1:C, :]
        q_in = jax.nn.silu(q_ref[rs, :].astype(f32)) * jnp.exp(b - b_mid)
        k_in = (1.0 - fg) * jnp.exp(b_mid - b)
        qin_s[rs, :] = q_in.astype(bf16)
        kin_s[rs, :] = k_in.astype(bf16)
        qex_s[rs, :] = (q_in * jnp.exp(b_mid)).astype(bf16)
        kdec_s[rs, :] = (k_in * jnp.exp(b_last - b_mid)).astype(bf16)
        dec_s[pl.ds(pl.multiple_of(c * 8, 8), 8), :] = jnp.broadcast_to(jnp.exp(b_last), (8, W))
        return carry

    lax.fori_loop(0, n_chunks, gates, 0, unroll=2)

    @pl.when((pl.program_id(0) == 0) & (pl.program_id(1) == 0) & (pl.program_id(2) == 0))
    def _():
        am_s[...] = jnp.zeros_like(am_s)
        xq_s[...] = jnp.zeros_like(xq_s)
        xk_s[...] = jnp.zeros_like(xk_s)

    S = HGRN_STRIP
    tt = lax.broadcasted_iota(jnp.int32, (S, S), 0)
    ss = lax.broadcasted_iota(jnp.int32, (S, S), 1)
    keep = jnp.logical_and(ss <= tt, (ss // C) == (tt // C))

    for h in range(G):
        ls = slice(h * K, (h + 1) * K)
        for c in range(n_chunks):
            xq_s[h, c * C:(c + 1) * C, c * K:(c + 1) * K] = qex_s[c * C:(c + 1) * C, ls]
            xk_s[h, c * C:(c + 1) * C, c * K:(c + 1) * K] = kdec_s[c * C:(c + 1) * C, ls]

        a = lax.dot_general(qin_s[:, ls], kin_s[:, ls], _NT, preferred_element_type=f32)
        for r in range(rows // S):
            sl = slice(r * S, (r + 1) * S)
            am_s[sl, sl] = jnp.where(keep, a[sl, sl], 0.0).astype(bf16)
        oacc_s[:, ls] = jnp.dot(am_s[...], i_ref[:, ls], preferred_element_type=f32)

        inc_s[...] = lax.dot_general(i_ref[:, ls], xk_s[h], _TN, preferred_element_type=f32)

        st = st_ref[h]
        for c in range(n_chunks):
            cs = slice(c * K, (c + 1) * K)
            smat_s[:, cs] = st.astype(bf16)
            st = st * dec_s[c * 8:c * 8 + 1, ls] + inc_s[:, cs]
        st_ref[h] = st

        oacc_s[:, ls] += lax.dot_general(xq_s[h], smat_s[...], _NT, preferred_element_type=f32)

    ng = ng_ref[...]

    def outputs(c, carry):
        rs = chunk_rows(c)
        for h in range(G):
            ls = slice(h * K, (h + 1) * K)
            o = oacc_s[rs, ls]
            ms = jnp.mean(o * o, axis=-1, keepdims=True)
            o = o * lax.rsqrt(ms + EPS) * ng[:, ls]
            o = o * jax.nn.silu(g_ref[rs, ls].astype(f32))
            o_ref[rs, ls] = o.astype(o_ref.dtype)
        return carry

    lax.fori_loop(0, n_chunks, outputs, 0, unroll=2)


def _hgrn(z, lower_bounds, norm_g, batch, seq):
    M = z.shape[0]
    W = HGRN_GROUP * HGRN_HEAD_DIM
    ts = HGRN_ROWS
    n_s = seq // ts
    n_hg = D_HGRN // W
    col0 = 2 * D_GMLP // W

    def zspec(part):
        return pl.BlockSpec((ts, W), lambda b, hg, s, part=part: (b * n_s + s, col0 + part * n_hg + hg))

    G, K = HGRN_GROUP, HGRN_HEAD_DIM
    n_chunks = ts // CHUNK
    est = (2 * 5 * ts * W * 2 + 4 * ts * W * 2 + ts * W * 4 + ts * ts * 2 + 2 * G * ts * n_chunks * K * 2
           + K * n_chunks * K * 6 + G * K * K * 4 + 3 * ts * ts * 4)
    return pl.pallas_call(
        _hgrn_body,
        grid=(batch, n_hg, n_s),
        in_specs=[zspec(0), zspec(1), zspec(2), zspec(3),
                  pl.BlockSpec((2, W), lambda b, hg, s: (0, hg)),
                  pl.BlockSpec((1, W), lambda b, hg, s: (0, hg))],
        out_specs=pl.BlockSpec((ts, W), lambda b, hg, s: (b * n_s + s, hg)),
        out_shape=jax.ShapeDtypeStruct((M, D_HGRN), bf16),
        scratch_shapes=[pltpu.VMEM((G, K, K), f32),
                        pltpu.VMEM((ts, W), bf16),
                        pltpu.VMEM((ts, W), bf16),
                        pltpu.VMEM((ts, W), bf16),
                        pltpu.VMEM((ts, W), bf16),
                        pltpu.VMEM((n_chunks * 8, W), f32),
                        pltpu.VMEM((ts, ts), bf16),
                        pltpu.VMEM((G, ts, n_chunks * K), bf16),
                        pltpu.VMEM((G, ts, n_chunks * K), bf16),
                        pltpu.VMEM((K, n_chunks * K), f32),
                        pltpu.VMEM((K, n_chunks * K), bf16),
                        pltpu.VMEM((ts, W), f32)],
        compiler_params=_params(("arbitrary", "arbitrary", "arbitrary"), est),
        name="hgrn2_mixer",
    )(z, z, z, z, lower_bounds.astype(f32), norm_g.reshape(1, D_HGRN).astype(f32))


def _xattn_body(h_ref, wq_ref, k_ref, v_ref, o_ref):
    q = jnp.dot(h_ref[...], wq_ref[...], preferred_element_type=f32).astype(bf16)
    s = lax.dot_general(q, k_ref[...], (((1,), (1,)), ((), ())), preferred_element_type=f32)
    s = s * (X_HEAD_DIM ** -0.5)
    s = s - jnp.max(s, axis=-1, keepdims=True)
    p = jnp.exp(s)
    p = p / jnp.sum(p, axis=-1, keepdims=True)
    o_ref[...] = jnp.dot(p.astype(bf16), v_ref[...], preferred_element_type=f32).astype(o_ref.dtype)


def _xattn(hc, wq, k, v, batch, seq, n_mem, tq=512):
    M, D = hc.shape
    n_s = seq // tq
    dh = X_HEAD_DIM
    est = 2 * (tq * D * 2 + D * dh * 2 + 2 * n_mem * dh * 2 + tq * dh * 2) + 3 * tq * dh * 4
    return pl.pallas_call(
        _xattn_body,
        grid=(X_HEADS, batch, n_s),
        in_specs=[pl.BlockSpec((tq, D), lambda h, b, s: (b * n_s + s, 0)),
                  pl.BlockSpec((D, dh), lambda h, b, s: (0, h)),
                  pl.BlockSpec((n_mem, dh), lambda h, b, s: (b, h)),
                  pl.BlockSpec((n_mem, dh), lambda h, b, s: (b, h))],
        out_specs=pl.BlockSpec((tq, dh), lambda h, b, s: (b * n_s + s, h)),
        out_shape=jax.ShapeDtypeStruct((M, D), bf16),
        compiler_params=_params(("arbitrary", "arbitrary", "arbitrary"), est),
        name="xattn_qproj",
    )(hc, wq, k, v)


def _router_body(h_ref, g_ref, wr_ref, br_ref, o_ref):
    x = h_ref[...]
    ms = jnp.mean(x * x, axis=-1, keepdims=True)
    xn = x * lax.rsqrt(ms + EPS) * g_ref[...]
    xh, xm, xl = _split3(xn)
    wh, wm, wl = _split3(wr_ref[...])
    dot = functools.partial(jnp.dot, preferred_element_type=f32)
    acc = dot(xl, wh) + dot(xh, wl) + dot(xm, wm)
    acc = acc + dot(xm, wh) + dot(xh, wm)
    acc = acc + dot(xh, wh)
    o_ref[...] = acc + br_ref[...]


def _router_logits(h, n_moe, w_group, b_group, w_router, b_router, tm=256):
    M, D = h.shape
    n_log = N_GROUPS + N_EXPERTS
    wr = jnp.concatenate([w_group] + [w_router[g] for g in range(N_GROUPS)], axis=1)
    wr = jnp.pad(wr.astype(f32), ((0, 0), (0, V7X_LANES - n_log)))
    br = jnp.pad(jnp.concatenate([b_group, b_router.reshape(-1)]).astype(f32), (0, V7X_LANES - n_log))
    est = 2 * tm * D * 4 + 2 * D * V7X_LANES * 4 + 6 * tm * D * 4
    return pl.pallas_call(
        _router_body,
        grid=(M // tm,),
        in_specs=[pl.BlockSpec((tm, D), lambda i: (i, 0)),
                  pl.BlockSpec((1, D), lambda i: (0, 0)),
                  pl.BlockSpec((D, V7X_LANES), lambda i: (0, 0)),
                  pl.BlockSpec((1, V7X_LANES), lambda i: (0, 0))],
        out_specs=pl.BlockSpec((tm, V7X_LANES), lambda i: (i, 0)),
        out_shape=jax.ShapeDtypeStruct((M, V7X_LANES), f32),
        compiler_params=_params(("arbitrary",), est),
        name="moe_router",
    )(h, n_moe.reshape(1, D).astype(f32), wr, br.reshape(1, V7X_LANES))


def _row_copy(src_hbm, row, dst, sem):
    return pltpu.make_async_copy(src_hbm.at[pl.ds(row, 1)], dst, sem)


def _gather_body(tok_ref, nrows_ref, h_hbm, g_ref, o_ref, buf, sem):
    R = GATHER_ROWS
    i = pl.program_id(0)
    n_steps = pl.num_programs(0)

    def start(step, slot):
        @pl.when(step * R < nrows_ref[0])
        def _():
            def issue(r, c):
                _row_copy(h_hbm, tok_ref[step * R + r], buf.at[slot, pl.ds(r, 1)], sem.at[slot]).start()
                return c
            lax.fori_loop(0, R, issue, 0)

    @pl.when(i == 0)
    def _():
        start(0, 0)

    @pl.when(i + 1 < n_steps)
    def _():
        start(i + 1, (i + 1) % 2)

    @pl.when(i * R < nrows_ref[0])
    def _():
        slot = i % 2
        pltpu.make_async_copy(h_hbm.at[pl.ds(0, R)], buf.at[slot], sem.at[slot]).wait()
        x = buf[slot]
        ms = jnp.mean(x * x, axis=-1, keepdims=True)
        o_ref[...] = (x * lax.rsqrt(ms + EPS) * g_ref[...]).astype(o_ref.dtype)


def _moe_gather(h, n_moe, slot_tok, n_rows_used, P):
    T, D = h.shape
    R = GATHER_ROWS
    n_steps = P // R

    def omap(i, tok, nrows):
        last = jnp.maximum(nrows[0] // R - 1, 0)
        return (jnp.minimum(i, last), 0)

    est = 2 * R * D * 4 + 2 * R * D * 2 + 3 * R * D * 4
    return pl.pallas_call(
        _gather_body,
        grid_spec=pltpu.PrefetchScalarGridSpec(
            num_scalar_prefetch=2,
            grid=(n_steps,),
            in_specs=[pl.BlockSpec(memory_space=pl.ANY),
                      pl.BlockSpec((1, D), lambda i, tok, nrows: (0, 0))],
            out_specs=pl.BlockSpec((R, D), omap),
            scratch_shapes=[pltpu.VMEM((2, R, D), f32), pltpu.SemaphoreType.DMA((2,))],
        ),
        out_shape=jax.ShapeDtypeStruct((P, D), bf16),
        compiler_params=_params(("arbitrary",), est),
        name="moe_gather",
    )(slot_tok, n_rows_used, h, n_moe.reshape(1, D).astype(f32))


def _moe_block(b, nblk):
    return jnp.minimum(b, nblk[0] - 1)


def _moe_run_start(eid_ref, b):
    return jnp.logical_or(b == 0, eid_ref[b] != eid_ref[jnp.maximum(b - 1, 0)])


def _moe_hidden_body(eid_ref, nblk_ref, x_ref, w1_ref, w3_ref, o_ref, w1b_ref, w3b_ref):
    b = pl.program_id(1)
    active = b < nblk_ref[0]

    @pl.when(jnp.logical_and(active, _moe_run_start(eid_ref, b)))
    def _():
        _cast_tile(w1_ref, w1b_ref)
        _cast_tile(w3_ref, w3b_ref)

    @pl.when(active)
    def _():
        x = x_ref[...]
        h1 = jnp.dot(x, w1b_ref[...], preferred_element_type=f32)
        h3 = jnp.dot(x, w3b_ref[...], preferred_element_type=f32)
        o_ref[...] = (jax.nn.silu(h1) * h3).astype(o_ref.dtype)


def _moe_hidden(xs, w1, w3, block_eid, n_blocks_used):
    P, D = xs.shape
    Bm, Ft = MOE_ROWS, MOE_F_TILE
    est = 2 * (Bm * D * 2 + 2 * D * Ft * 4 + Bm * Ft * 2) + 2 * D * Ft * 2 + 3 * Bm * Ft * 4
    return pl.pallas_call(
        _moe_hidden_body,
        grid_spec=pltpu.PrefetchScalarGridSpec(
            num_scalar_prefetch=2,
            grid=(D_EXPERT // Ft, P // Bm),
            in_specs=[pl.BlockSpec((Bm, D), lambda f, b, eid, nblk: (_moe_block(b, nblk), 0)),
                      pl.BlockSpec((None, D, Ft), lambda f, b, eid, nblk: (eid[_moe_block(b, nblk)], 0, f)),
                      pl.BlockSpec((None, D, Ft), lambda f, b, eid, nblk: (eid[_moe_block(b, nblk)], 0, f))],
            out_specs=pl.BlockSpec((Bm, Ft), lambda f, b, eid, nblk: (_moe_block(b, nblk), f)),
            scratch_shapes=[pltpu.VMEM((D, Ft), bf16), pltpu.VMEM((D, Ft), bf16)],
        ),
        out_shape=jax.ShapeDtypeStruct((P, D_EXPERT), bf16),
        compiler_params=_params(("arbitrary", "arbitrary"), est),
        name="moe_expert_hidden",
    )(block_eid, n_blocks_used, xs, w1, w3)


def _moe_out_body(eid_ref, nblk_ref, h_ref, w2_ref, o_ref, w2b_ref):
    b = pl.program_id(1)
    active = b < nblk_ref[0]

    @pl.when(jnp.logical_and(active, _moe_run_start(eid_ref, b)))
    def _():
        _cast_tile(w2_ref, w2b_ref)

    @pl.when(active)
    def _():
        o_ref[...] = jnp.dot(h_ref[...], w2b_ref[...], preferred_element_type=f32)


def _moe_out(hid, w2, block_eid, n_blocks_used):
    P, F = hid.shape
    D = w2.shape[2]
    Bm, Nt = MOE_ROWS, MOE_N_TILE
    est = 2 * (Bm * F * 2 + F * Nt * 4 + Bm * Nt * 4) + F * Nt * 2 + Bm * Nt * 4
    return pl.pallas_call(
        _moe_out_body,
        grid_spec=pltpu.PrefetchScalarGridSpec(
            num_scalar_prefetch=2,
            grid=(D // Nt, P // Bm),
            in_specs=[pl.BlockSpec((Bm, F), lambda n, b, eid, nblk: (_moe_block(b, nblk), 0)),
                      pl.BlockSpec((None, F, Nt), lambda n, b, eid, nblk: (eid[_moe_block(b, nblk)], 0, n))],
            out_specs=pl.BlockSpec((Bm, Nt), lambda n, b, eid, nblk: (_moe_block(b, nblk), n)),
            scratch_shapes=[pltpu.VMEM((F, Nt), bf16)],
        ),
        out_shape=jax.ShapeDtypeStruct((P, D), f32),
        compiler_params=_params(("arbitrary", "arbitrary"), est),
        name="moe_expert_out",
    )(block_eid, n_blocks_used, hid, w2)


def _combine_body(dest_ref, h_ref, gate_ref, g_ref, y_hbm, o_ref, buf, sem):
    R = GATHER_ROWS
    i = pl.program_id(0)
    n_steps = pl.num_programs(0)

    def start(step, slot):
        def issue(r, c):
            for k in range(TOP_K):
                _row_copy(y_hbm, dest_ref[(step * R + r) * TOP_K + k],
                          buf.at[slot, k, pl.ds(r, 1)], sem.at[slot]).start()
            return c
        lax.fori_loop(0, R, issue, 0)

    @pl.when(i == 0)
    def _():
        start(0, 0)

    @pl.when(i + 1 < n_steps)
    def _():
        start(i + 1, (i + 1) % 2)

    slot = i % 2
    for k in range(TOP_K):
        pltpu.make_async_copy(y_hbm.at[pl.ds(0, R)], buf.at[slot, k], sem.at[slot]).wait()
    gate = gate_ref[...]
    h = h_ref[...] + buf[slot, 0] * gate[:, 0:1] + buf[slot, 1] * gate[:, 1:2]
    ms = jnp.mean(h * h, axis=-1, keepdims=True)
    o_ref[...] = h * lax.rsqrt(ms + EPS) * g_ref[...]


def _moe_combine(h, y, dest, gate, n_final):
    T, D = h.shape
    R = GATHER_ROWS
    est = 2 * TOP_K * R * D * 4 + 4 * R * D * 4 + 3 * R * D * 4
    return pl.pallas_call(
        _combine_body,
        grid_spec=pltpu.PrefetchScalarGridSpec(
            num_scalar_prefetch=1,
            grid=(T // R,),
            in_specs=[pl.BlockSpec((R, D), lambda i, dest: (i, 0)),
                      pl.BlockSpec((R, TOP_K), lambda i, dest: (i, 0)),
                      pl.BlockSpec((1, D), lambda i, dest: (0, 0)),
                      pl.BlockSpec(memory_space=pl.ANY)],
            out_specs=pl.BlockSpec((R, D), lambda i, dest: (i, 0)),
            scratch_shapes=[pltpu.VMEM((2, TOP_K, R, D), f32), pltpu.SemaphoreType.DMA((2,))],
        ),
        out_shape=jax.ShapeDtypeStruct((T, D), f32),
        compiler_params=_params(("arbitrary",), est),
        name="moe_combine_norm",
    )(dest, h, gate, n_final.reshape(1, D).astype(f32), y)


def _route(logits):
    T = logits.shape[0]
    A = T * TOP_K
    g_logits = logits[:, :N_GROUPS]
    e_all = logits[:, N_GROUPS:N_GROUPS + N_EXPERTS].reshape(T, N_GROUPS, EXPERTS_PER_GROUP)
    g_prob = jax.nn.softmax(g_logits, axis=-1)
    grp = jnp.argmax(g_logits, axis=-1).astype(jnp.int32)
    p_grp = jnp.take_along_axis(g_prob, grp[:, None], axis=-1)[:, 0]
    e_logits = jnp.take_along_axis(e_all, grp[:, None, None], axis=1)[:, 0]
    top_v, top_i = lax.top_k(e_logits, TOP_K)
    gate = p_grp[:, None] * jax.nn.softmax(top_v, axis=-1)
    eid = (grp[:, None] * EXPERTS_PER_GROUP + top_i.astype(jnp.int32)).reshape(A)

    onehot = (eid[:, None] == jnp.arange(N_EXPERTS, dtype=jnp.int32)[None, :]).astype(jnp.int32)
    csum = jnp.cumsum(onehot, axis=0)
    rank = jnp.sum(onehot * csum, axis=1) - 1
    counts = csum[-1]
    pcounts = (counts + MOE_ROWS - 1) // MOE_ROWS * MOE_ROWS
    pend = jnp.cumsum(pcounts)
    pstart = pend - pcounts
    dest = (pstart[eid] + rank).astype(jnp.int32)
    n_blocks = -(-(A + N_EXPERTS * (MOE_ROWS - 1)) // MOE_ROWS)
    P = n_blocks * MOE_ROWS
    tok = jnp.arange(A, dtype=jnp.int32) // TOP_K
    slot_tok = jnp.zeros((P,), jnp.int32).at[dest].set(tok)
    block_eid = jnp.minimum(
        jnp.searchsorted(pend, jnp.arange(n_blocks, dtype=jnp.int32) * MOE_ROWS, side='right'),
        N_EXPERTS - 1).astype(jnp.int32)
    n_rows_used = pend[-1:].astype(jnp.int32)
    return gate, dest, slot_tok, block_eid, n_rows_used, P


def kernel(x, mem, n_mix, w_in, gmlp_ln_g, gmlp_ln_b, gmlp_w_s, gmlp_b_s, hgrn_lower_bounds,
           hgrn_norm_g, w_out, n_cross, n_mem, w_q_x, w_k_x, w_v_x, w_o_x, n_moe, w_group,
           b_group, w_router, b_router, w1_e, w3_e, w2_e, n_final):
    B, S, D = x.shape
    n_mem_tok = mem.shape[1]
    T = B * S
    xt = x.reshape(T, D)

    hn = _rmsnorm(xt, n_mix[0], bf16)
    z = _matmul(hn, w_in[0], bf16, 1024, 512, "mm_w_in")
    ya = _gmlp(z, gmlp_ln_g[0], gmlp_ln_b[0], gmlp_w_s[0].astype(f32), gmlp_b_s[0].astype(f32))
    yb = _hgrn(z, hgrn_lower_bounds, hgrn_norm_g[0], B, S)
    h1 = _matmul_res(xt, [ya, yb], w_out[0], 1024, 512, "mm_w_out")

    hc = _rmsnorm(h1, n_cross[0], bf16)
    mn = _rmsnorm(mem.reshape(B * n_mem_tok, D), n_mem[0], bf16)
    kx = _matmul(mn, w_k_x[0], bf16, B * n_mem_tok, 512, "mm_w_k")
    vx = _matmul(mn, w_v_x[0], bf16, B * n_mem_tok, 512, "mm_w_v")
    ox = _xattn(hc, w_q_x[0].astype(bf16), kx, vx, B, S, n_mem_tok)
    h2 = _matmul_res(h1, [ox], w_o_x[0], 1024, 512, "mm_w_o")

    logits = _router_logits(h2, n_moe[0], w_group[0], b_group[0], w_router[0], b_router[0])
    gate, dest, slot_tok, block_eid, n_rows_used, P = _route(logits)
    xs = _moe_gather(h2, n_moe[0], slot_tok, n_rows_used, P)
    n_blocks_used = n_rows_used // MOE_ROWS
    hid = _moe_hidden(xs, w1_e[0], w3_e[0], block_eid, n_blocks_used)
    y = _moe_out(hid, w2_e[0], block_eid, n_blocks_used)
    out = _moe_combine(h2, y, dest, gate.astype(f32), n_final)
    return out.reshape(B, S, D)
```

```python
import functools

import jax
import jax.numpy as jnp
from jax import lax
from jax.experimental import pallas as pl
from jax.experimental.pallas import tpu as pltpu

f32 = jnp.float32
bf16 = jnp.bfloat16

EPS = 1e-6
D_MODEL = 4096
CHUNK = 64
GMLP_HEADS = 4
GMLP_HEAD_DIM = 512
GMLP_BLOCK = 128
D_GMLP = GMLP_HEADS * GMLP_HEAD_DIM
HGRN_HEAD_DIM = 128
HGRN_HEADS = 16
D_HGRN = HGRN_HEADS * HGRN_HEAD_DIM
X_HEADS = 4
X_HEAD_DIM = 1024
N_GROUPS = 4
EXPERTS_PER_GROUP = 8
N_EXPERTS = N_GROUPS * EXPERTS_PER_GROUP
TOP_K = 2
D_EXPERT = 1024

V7X_LANES = 128
V7X_SUBLANES = 8
V7X_VMEM_LIMIT_BYTES = 60000 * 1024

MOE_ROWS = 512
MOE_F_TILE = 512
MOE_N_TILE = 2048
GATHER_ROWS = 256
HGRN_GROUP = 4
HGRN_ROWS = 512


def _params(sem, est_bytes):
    limit = min(int(est_bytes) + (8 << 20), V7X_VMEM_LIMIT_BYTES)
    return pltpu.CompilerParams(dimension_semantics=sem, vmem_limit_bytes=limit)


def _rms_body(x_ref, g_ref, o_ref):
    x = x_ref[...].astype(f32)
    ms = jnp.mean(x * x, axis=-1, keepdims=True)
    o_ref[...] = (x * lax.rsqrt(ms + EPS) * g_ref[...]).astype(o_ref.dtype)


def _rmsnorm(x, g, out_dtype, tm=256):
    M, D = x.shape
    est = 2 * tm * D * (x.dtype.itemsize + jnp.dtype(out_dtype).itemsize) + 2 * tm * D * 4
    return pl.pallas_call(
        _rms_body,
        grid=(M // tm,),
        in_specs=[pl.BlockSpec((tm, D), lambda i: (i, 0)),
                  pl.BlockSpec((1, D), lambda i: (0, 0))],
        out_specs=pl.BlockSpec((tm, D), lambda i: (i, 0)),
        out_shape=jax.ShapeDtypeStruct((M, D), out_dtype),
        compiler_params=_params(("arbitrary",), est),
        name="rmsnorm",
    )(x, g.reshape(1, D).astype(f32))


CAST_ROWS = 512


def _cast_tile(src_ref, dst_ref, dst_row0=0):
    n = src_ref.shape[0] // CAST_ROWS

    def piece(p, c):
        r = pl.multiple_of(p * CAST_ROWS, CAST_ROWS)
        d = pl.multiple_of(dst_row0 + p * CAST_ROWS, CAST_ROWS)
        dst_ref[pl.ds(d, CAST_ROWS), :] = src_ref[pl.ds(r, CAST_ROWS), :].astype(bf16)
        return c

    lax.fori_loop(0, n, piece, 0)


def _mm_body(a_ref, w_ref, o_ref, wb_ref):
    @pl.when(pl.program_id(1) == 0)
    def _():
        _cast_tile(w_ref, wb_ref)

    o_ref[...] = jnp.dot(a_ref[...], wb_ref[...], preferred_element_type=f32).astype(o_ref.dtype)


def _matmul(a, w, out_dtype, tm, tn, name):
    M, K = a.shape
    N = w.shape[1]
    osz = jnp.dtype(out_dtype).itemsize
    est = 2 * (tm * K * 2 + K * tn * 4 + tm * tn * osz) + K * tn * 2 + tm * tn * 4
    return pl.pallas_call(
        _mm_body,
        grid=(N // tn, M // tm),
        in_specs=[pl.BlockSpec((tm, K), lambda j, i: (i, 0)),
                  pl.BlockSpec((K, tn), lambda j, i: (0, j))],
        out_specs=pl.BlockSpec((tm, tn), lambda j, i: (i, j)),
        out_shape=jax.ShapeDtypeStruct((M, N), out_dtype),
        scratch_shapes=[pltpu.VMEM((K, tn), bf16)],
        compiler_params=_params(("arbitrary", "arbitrary"), est),
        name=name,
    )(a, w)


def _mm_res_body(n_parts, r_ref, *refs):
    a_refs = refs[:n_parts]
    w_refs = refs[n_parts:2 * n_parts]
    o_ref, wb_ref = refs[2 * n_parts:]
    kp = w_refs[0].shape[0]

    @pl.when(pl.program_id(1) == 0)
    def _():
        for p in range(n_parts):
            _cast_tile(w_refs[p], wb_ref, p * kp)

    acc = r_ref[...]
    for p in range(n_parts):
        acc = acc + jnp.dot(a_refs[p][...], wb_ref[p * kp:(p + 1) * kp, :], preferred_element_type=f32)
    o_ref[...] = acc


def _matmul_res(res, a_list, w, tm, tn, name):
    M, N = res.shape
    n = len(a_list)
    Kp = a_list[0].shape[1]
    a_specs = [pl.BlockSpec((tm, Kp), lambda j, i: (i, 0)) for _ in range(n)]
    w_specs = [pl.BlockSpec((Kp, tn), functools.partial(lambda j, i, p: (p, j), p=p)) for p in range(n)]
    est = 2 * (n * tm * Kp * 2 + n * Kp * tn * 4 + 2 * tm * tn * 4) + n * Kp * tn * 2 + 2 * tm * tn * 4
    return pl.pallas_call(
        functools.partial(_mm_res_body, n),
        grid=(N // tn, M // tm),
        in_specs=[pl.BlockSpec((tm, tn), lambda j, i: (i, j))] + a_specs + w_specs,
        out_specs=pl.BlockSpec((tm, tn), lambda j, i: (i, j)),
        out_shape=jax.ShapeDtypeStruct((M, N), f32),
        scratch_shapes=[pltpu.VMEM((n * Kp, tn), bf16)],
        compiler_params=_params(("arbitrary", "arbitrary"), est),
        name=name,
    )(res, *a_list, *([w] * n))


def _gelu(x):
    return 0.5 * x * (1.0 + lax.erf(x * 0.7071067811865476))


def _gmlp_body(u_ref, v_ref, lng_ref, lnb_ref, ws_ref, bs_ref, o_ref):
    rows = u_ref.shape[0]
    gv = _gelu(v_ref[...].astype(f32))
    mu = jnp.mean(gv, axis=-1, keepdims=True)
    xc = gv - mu
    var = jnp.mean(xc * xc, axis=-1, keepdims=True)
    vn = (xc * lax.rsqrt(var + EPS) * lng_ref[...] + lnb_ref[...]).astype(bf16)
    ci = lax.broadcasted_iota(jnp.int32, (GMLP_BLOCK, GMLP_BLOCK), 0) // CHUNK
    cj = lax.broadcasted_iota(jnp.int32, (GMLP_BLOCK, GMLP_BLOCK), 1) // CHUNK
    w = jnp.where(cj <= ci, ws_ref[0], 0.0).astype(bf16)
    bias = bs_ref[0]
    for blk in range(rows // GMLP_BLOCK):
        sl = slice(blk * GMLP_BLOCK, (blk + 1) * GMLP_BLOCK)
        s = jnp.dot(w, vn[sl], preferred_element_type=f32) + bias
        o_ref[sl, :] = (_gelu(u_ref[sl, :].astype(f32)) * s).astype(o_ref.dtype)


def _gmlp(z, ln_g, ln_b, w_s, b_s, tr=256):
    M = z.shape[0]
    C = GMLP_HEAD_DIM
    est = 2 * 3 * tr * C * 2 + 8 * tr * C * 4
    return pl.pallas_call(
        _gmlp_body,
        grid=(M // tr, GMLP_HEADS),
        in_specs=[pl.BlockSpec((tr, C), lambda i, h: (i, h)),
                  pl.BlockSpec((tr, C), lambda i, h: (i, GMLP_HEADS + h)),
                  pl.BlockSpec((1, C), lambda i, h: (0, h)),
                  pl.BlockSpec((1, C), lambda i, h: (0, h)),
                  pl.BlockSpec((1, GMLP_BLOCK, GMLP_BLOCK), lambda i, h: (h, 0, 0)),
                  pl.BlockSpec((1, GMLP_BLOCK, 1), lambda i, h: (h, 0, 0))],
        out_specs=pl.BlockSpec((tr, C), lambda i, h: (i, h)),
        out_shape=jax.ShapeDtypeStruct((M, D_GMLP), bf16),
        compiler_params=_params(("arbitrary", "arbitrary"), est),
        name="gmlp_mixer",
    )(z, z, ln_g.reshape(1, D_GMLP), ln_b.reshape(1, D_GMLP), w_s,
      b_s.reshape(GMLP_HEADS, GMLP_BLOCK, 1))


def _split3(x):
    hi = x.astype(bf16)
    r1 = x - hi.astype(f32)
    mid = r1.astype(bf16)
    lo = (r1 - mid.astype(f32)).astype(bf16)
    return hi, mid, lo


HGRN_STRIP = 2 * CHUNK
_NT = (((1,), (1,)), ((), ()))
_TN = (((0,), (0,)), ((), ()))


def _split2(x):
    hi = x.astype(bf16)
    mid = (x - hi.astype(f32)).astype(bf16)
    return hi, mid


def _hgrn_body(q_ref, f_ref, i_ref, g_ref, lbr_ref, ng_ref, o_ref,
               st_ref, qin_s, kin_s, qex_s, kdec_s, dec_s, am_s, xq_s, xk_s, inc_s, smat_s, oacc_s):
    C, K, G = CHUNK, HGRN_HEAD_DIM, HGRN_GROUP
    rows, W = q_ref.shape
    n_chunks = rows // C

    @pl.when(pl.program_id(2) == 0)
    def _():
        st_ref[...] = jnp.zeros_like(st_ref)

    r0 = lbr_ref[0:1, :]
    r1 = lbr_ref[1:2, :]
    m = jnp.maximum(r0, r1)
    e0 = jnp.exp(r0 - m)
    e1 = jnp.exp(r1 - m)
    lb = e0 / (e0 + e1)

    ti = lax.broadcasted_iota(jnp.int32, (C, C), 0)
    si = lax.broadcasted_iota(jnp.int32, (C, C), 1)
    tril = (si <= ti).astype(bf16)

    def chunk_rows(c):
        return pl.ds(pl.multiple_of(c * C, C), C)

    def gates(c, carry):
        rs = chunk_rows(c)
        fg = lb + (1.0 - lb) * jax.nn.sigmoid(f_ref[rs, :].astype(f32))
        hi, mid, lo = _split3(jnp.log(fg))
        b = (jnp.dot(tril, hi, preferred_element_type=f32)
             + jnp.dot(tril, mid, preferred_element_type=f32)
             + jnp.dot(tril, lo, preferred_element_type=f32))
        b_mid = b[C // 2:C // 2 + 1, :]
        b_last = b[C - 1:C, :]
        q_in = jax.nn.silu(q_ref[rs, :].astype(f32)) * jnp.exp(b - b_mid)
        k_in = (1.0 - fg) * jnp.exp(b_mid - b)
        qin_s[rs, :] = q_in.astype(bf16)
        kin_s[rs, :] = k_in.astype(bf16)
        qex_s[rs, :] = (q_in * jnp.exp(b_mid)).astype(bf16)
        kdec_s[rs, :] = (k_in * jnp.exp(b_last - b_mid)).astype(bf16)
        dec_s[pl.ds(pl.multiple_of(c * 8, 8), 8), :] = jnp.broadcast_to(jnp.exp(b_last), (8, W))
        return carry

    lax.fori_loop(0, n_chunks, gates, 0, unroll=2)

    @pl.when((pl.program_id(0) == 0) & (pl.program_id(1) == 0) & (pl.program_id(2) == 0))
    def _():
        am_s[...] = jnp.zeros_like(am_s)
        xq_s[...] = jnp.zeros_like(xq_s)
        xk_s[...] = jnp.zeros_like(xk_s)

    S = HGRN_STRIP
    tt = lax.broadcasted_iota(jnp.int32, (S, S), 0)
    ss = lax.broadcasted_iota(jnp.int32, (S, S), 1)
    keep = jnp.logical_and(ss <= tt, (ss // C) == (tt // C))

    for h in range(G):
        ls = slice(h * K, (h + 1) * K)
        for c in range(n_chunks):
            xq_s[h, c * C:(c + 1) * C, c * K:(c + 1) * K] = qex_s[c * C:(c + 1) * C, ls]
            xk_s[h, c * C:(c + 1) * C, c * K:(c + 1) * K] = kdec_s[c * C:(c + 1) * C, ls]

        a = lax.dot_general(qin_s[:, ls], kin_s[:, ls], _NT, preferred_element_type=f32)
        for r in range(rows // S):
            sl = slice(r * S, (r + 1) * S)
            am_s[sl, sl] = jnp.where(keep, a[sl, sl], 0.0).astype(bf16)
        oacc_s[:, ls] = jnp.dot(am_s[...], i_ref[:, ls], preferred_element_type=f32)

        inc_s[...] = lax.dot_general(i_ref[:, ls], xk_s[h], _TN, preferred_element_type=f32)

        st = st_ref[h]
        for c in range(n_chunks):
            cs = slice(c * K, (c + 1) * K)
            smat_s[:, cs] = st.astype(bf16)
            st = st * dec_s[c * 8:c * 8 + 1, ls] + inc_s[:, cs]
        st_ref[h] = st

        oacc_s[:, ls] += lax.dot_general(xq_s[h], smat_s[...], _NT, preferred_element_type=f32)

    ng = ng_ref[...]

    def outputs(c, carry):
        rs = chunk_rows(c)
        for h in range(G):
            ls = slice(h * K, (h + 1) * K)
            o = oacc_s[rs, ls]
            ms = jnp.mean(o * o, axis=-1, keepdims=True)
            o = o * lax.rsqrt(ms + EPS) * ng[:, ls]
            o = o * jax.nn.silu(g_ref[rs, ls].astype(f32))
            o_ref[rs, ls] = o.astype(o_ref.dtype)
        return carry

    lax.fori_loop(0, n_chunks, outputs, 0, unroll=2)


def _hgrn(z, lower_bounds, norm_g, batch, seq):
    M = z.shape[0]
    W = HGRN_GROUP * HGRN_HEAD_DIM
    ts = HGRN_ROWS
    n_s = seq // ts
    n_hg = D_HGRN // W
    col0 = 2 * D_GMLP // W

    def zspec(part):
        return pl.BlockSpec((ts, W), lambda b, hg, s, part=part: (b * n_s + s, col0 + part * n_hg + hg))

    G, K = HGRN_GROUP, HGRN_HEAD_DIM
    n_chunks = ts // CHUNK
    est = (2 * 5 * ts * W * 2 + 4 * ts * W * 2 + ts * W * 4 + ts * ts * 2 + 2 * G * ts * n_chunks * K * 2
           + K * n_chunks * K * 6 + G * K * K * 4 + 3 * ts * ts * 4)
    return pl.pallas_call(
        _hgrn_body,
        grid=(batch, n_hg, n_s),
        in_specs=[zspec(0), zspec(1), zspec(2), zspec(3),
                  pl.BlockSpec((2, W), lambda b, hg, s: (0, hg)),
                  pl.BlockSpec((1, W), lambda b, hg, s: (0, hg))],
        out_specs=pl.BlockSpec((ts, W), lambda b, hg, s: (b * n_s + s, hg)),
        out_shape=jax.ShapeDtypeStruct((M, D_HGRN), bf16),
        scratch_shapes=[pltpu.VMEM((G, K, K), f32),
                        pltpu.VMEM((ts, W), bf16),
                        pltpu.VMEM((ts, W), bf16),
                        pltpu.VMEM((ts, W), bf16),
                        pltpu.VMEM((ts, W), bf16),
                        pltpu.VMEM((n_chunks * 8, W), f32),
                        pltpu.VMEM((ts, ts), bf16),
                        pltpu.VMEM((G, ts, n_chunks * K), bf16),
                        pltpu.VMEM((G, ts, n_chunks * K), bf16),
                        pltpu.VMEM((K, n_chunks * K), f32),
                        pltpu.VMEM((K, n_chunks * K), bf16),
                        pltpu.VMEM((ts, W), f32)],
        compiler_params=_params(("arbitrary", "arbitrary", "arbitrary"), est),
        name="hgrn2_mixer",
    )(z, z, z, z, lower_bounds.astype(f32), norm_g.reshape(1, D_HGRN).astype(f32))


def _xattn_body(h_ref, wq_ref, k_ref, v_ref, o_ref):
    q = jnp.dot(h_ref[...], wq_ref[...], preferred_element_type=f32).astype(bf16)
    s = lax.dot_general(q, k_ref[...], (((1,), (1,)), ((), ())), preferred_element_type=f32)
    s = s * (X_HEAD_DIM ** -0.5)
    s = s - jnp.max(s, axis=-1, keepdims=True)
    p = jnp.exp(s)
    p = p / jnp.sum(p, axis=-1, keepdims=True)
    o_ref[...] = jnp.dot(p.astype(bf16), v_ref[...], preferred_element_type=f32).astype(o_ref.dtype)


def _xattn(hc, wq, k, v, batch, seq, n_mem, tq=512):
    M, D = hc.shape
    n_s = seq // tq
    dh = X_HEAD_DIM
    est = 2 * (tq * D * 2 + D * dh * 2 + 2 * n_mem * dh * 2 + tq * dh * 2) + 3 * tq * dh * 4
    return pl.pallas_call(
        _xattn_body,
        grid=(X_HEADS, batch, n_s),
        in_specs=[pl.BlockSpec((tq, D), lambda h, b, s: (b * n_s + s, 0)),
                  pl.BlockSpec((D, dh), lambda h, b, s: (0, h)),
                  pl.BlockSpec((n_mem, dh), lambda h, b, s: (b, h)),
                  pl.BlockSpec((n_mem, dh), lambda h, b, s: (b, h))],
        out_specs=pl.BlockSpec((tq, dh), lambda h, b, s: (b * n_s + s, h)),
        out_shape=jax.ShapeDtypeStruct((M, D), bf16),
        compiler_params=_params(("arbitrary", "arbitrary", "arbitrary"), est),
        name="xattn_qproj",
    )(hc, wq, k, v)


u32 = jnp.uint32
ROUTE_GATE_LANE = 0
ROUTE_EID_LANE = 2


def _pack_bf16_pair(lo, hi):
    lo32 = lax.bitcast_convert_type(lo.astype(f32), u32) >> 16
    hi32 = lax.bitcast_convert_type(hi.astype(f32), u32) & jnp.uint32(0xFFFF0000)
    return lo32 | hi32


def _unpack_bf16_pair(w):
    lo = lax.bitcast_convert_type(w << 16, f32).astype(bf16)
    hi = lax.bitcast_convert_type(w & jnp.uint32(0xFFFF0000), f32).astype(bf16)
    return lo, hi


def _first_lane_of(mask, lane):
    return jnp.min(jnp.where(mask, lane, V7X_LANES), axis=-1, keepdims=True)


def _router_body(h_ref, g_ref, wr_ref, br_ref, xm_ref, r_ref):
    x = h_ref[...]
    D = x.shape[1]
    ms = jnp.mean(x * x, axis=-1, keepdims=True)
    xn = x * lax.rsqrt(ms + EPS) * g_ref[...]
    xh, xm = _split2(xn)
    wh, wm = _split2(wr_ref[...])
    dot = functools.partial(jnp.dot, preferred_element_type=f32)
    acc = dot(xm, wh) + dot(xh, wm)
    acc = acc + dot(xh, wh)
    logits = acc + br_ref[...]

    xm_ref[...] = _pack_bf16_pair(xh[:, :D // 2], xh[:, D // 2:])

    lane = lax.broadcasted_iota(jnp.int32, logits.shape, 1)
    neg = jnp.float32(-jnp.inf)
    is_g = lane < N_GROUPS
    gl = jnp.where(is_g, logits, neg)
    gmax = jnp.max(gl, axis=-1, keepdims=True)
    grp = _first_lane_of(gl == gmax, lane)
    p_grp = 1.0 / jnp.sum(jnp.where(is_g, jnp.exp(logits - gmax), 0.0), axis=-1, keepdims=True)
    e0 = N_GROUPS + grp * EXPERTS_PER_GROUP
    el = jnp.where(jnp.logical_and(lane >= e0, lane < e0 + EXPERTS_PER_GROUP), logits, neg)
    m1 = jnp.max(el, axis=-1, keepdims=True)
    i1 = _first_lane_of(el == m1, lane)
    el2 = jnp.where(lane == i1, neg, el)
    m2 = jnp.max(el2, axis=-1, keepdims=True)
    i2 = _first_lane_of(el2 == m2, lane)
    t = jnp.exp(m2 - m1)
    g1 = p_grp * (1.0 / (1.0 + t))
    g2 = p_grp * (t / (1.0 + t))
    rec = jnp.where(lane == ROUTE_GATE_LANE, g1, 0.0)
    rec = jnp.where(lane == ROUTE_GATE_LANE + 1, g2, rec)
    rec = jnp.where(lane == ROUTE_EID_LANE, (i1 - N_GROUPS).astype(f32), rec)
    rec = jnp.where(lane == ROUTE_EID_LANE + 1, (i2 - N_GROUPS).astype(f32), rec)
    r_ref[...] = rec


def _router(h, n_moe, w_group, b_group, w_router, b_router, tm=256):
    M, D = h.shape
    n_log = N_GROUPS + N_EXPERTS
    wr = jnp.concatenate([w_group] + [w_router[g] for g in range(N_GROUPS)], axis=1)
    wr = jnp.pad(wr.astype(f32), ((0, 0), (0, V7X_LANES - n_log)))
    br = jnp.pad(jnp.concatenate([b_group, b_router.reshape(-1)]).astype(f32), (0, V7X_LANES - n_log))
    est = 2 * tm * D * 4 + 2 * D * V7X_LANES * 4 + 2 * tm * D * 2 + 6 * tm * D * 4
    return pl.pallas_call(
        _router_body,
        grid=(M // tm,),
        in_specs=[pl.BlockSpec((tm, D), lambda i: (i, 0)),
                  pl.BlockSpec((1, D), lambda i: (0, 0)),
                  pl.BlockSpec((D, V7X_LANES), lambda i: (0, 0)),
                  pl.BlockSpec((1, V7X_LANES), lambda i: (0, 0))],
        out_specs=[pl.BlockSpec((tm, D // 2), lambda i: (i, 0)),
                   pl.BlockSpec((tm, V7X_LANES), lambda i: (i, 0))],
        out_shape=[jax.ShapeDtypeStruct((M, D // 2), u32),
                   jax.ShapeDtypeStruct((M, V7X_LANES), f32)],
        compiler_params=_params(("arbitrary",), est),
        name="moe_router",
    )(h, n_moe.reshape(1, D).astype(f32), wr, br.reshape(1, V7X_LANES))


def _row_copy(src_hbm, row, dst, sem):
    return pltpu.make_async_copy(src_hbm.at[pl.ds(row, 1)], dst, sem)


PAD_BITS = tuple(1 << k for k in reversed(range((MOE_ROWS - 1).bit_length())))


def _dispatch_body(dest_ref, padend_ref, padlen_ref, xm_ref, xs_hbm, buf, zbuf, sem, zsem):
    R = GATHER_ROWS
    i = pl.program_id(0)
    n_steps = pl.num_programs(0)
    slot = i % 2

    def wait_slot(s):
        for _ in range(TOP_K):
            pltpu.make_async_copy(buf.at[s], xs_hbm.at[pl.ds(0, R)], sem.at[s]).wait()

    def pad_copies(fn):
        def per_expert(e, c):
            ln = padlen_ref[e]
            end = padend_ref[e]
            for bit in PAD_BITS:
                rows_per_copy = bit if bit >= V7X_SUBLANES else 1
                for j in range(bit // rows_per_copy):
                    @pl.when((ln & bit) != 0)
                    def _(bit=bit, end=end, j=j, n=rows_per_copy):
                        start = end - bit + j * n
                        if n > 1:
                            start = pl.multiple_of(start, V7X_SUBLANES)
                        fn(pltpu.make_async_copy(zbuf.at[pl.ds(0, n)], xs_hbm.at[pl.ds(start, n)], zsem))
                end = end - (ln & bit)
            return c
        lax.fori_loop(0, N_EXPERTS, per_expert, 0)

    @pl.when(i == 0)
    def _():
        zbuf[...] = jnp.zeros_like(zbuf)
        pad_copies(lambda cp: cp.start())

    @pl.when(i >= 2)
    def _():
        wait_slot(slot)

    buf[slot] = xm_ref[...]

    def issue(r, c):
        for k in range(TOP_K):
            d = dest_ref[(i * R + r) * TOP_K + k]
            pltpu.make_async_copy(buf.at[slot, pl.ds(r, 1)], xs_hbm.at[pl.ds(d, 1)], sem.at[slot]).start()
        return c

    lax.fori_loop(0, R, issue, 0, unroll=8)

    @pl.when(i == n_steps - 1)
    def _():
        if n_steps >= 2:
            wait_slot(1 - slot)
        wait_slot(slot)
        pad_copies(lambda cp: cp.wait())


def _moe_dispatch(xm, dest, pad_end, pad_len, P):
    T, Dh = xm.shape
    R = GATHER_ROWS
    est = 2 * R * Dh * 4 + 2 * R * Dh * 4 + PAD_BITS[0] * Dh * 4
    return pl.pallas_call(
        _dispatch_body,
        grid_spec=pltpu.PrefetchScalarGridSpec(
            num_scalar_prefetch=3,
            grid=(T // R,),
            in_specs=[pl.BlockSpec((R, Dh), lambda i, dest, pend, plen: (i, 0))],
            out_specs=pl.BlockSpec(memory_space=pl.ANY),
            scratch_shapes=[pltpu.VMEM((2, R, Dh), u32), pltpu.VMEM((PAD_BITS[0], Dh), u32),
                            pltpu.SemaphoreType.DMA((2,)), pltpu.SemaphoreType.DMA],
        ),
        out_shape=jax.ShapeDtypeStruct((P, Dh), u32),
        compiler_params=_params(("arbitrary",), est),
        name="moe_dispatch",
    )(dest, pad_end, pad_len, xm)


def _moe_block(b, nblk):
    return jnp.minimum(b, nblk[0] - 1)


def _moe_run_start(eid_ref, b):
    return jnp.logical_or(b == 0, eid_ref[b] != eid_ref[jnp.maximum(b - 1, 0)])


def _moe_hidden_body(eid_ref, nblk_ref, x_ref, w1_ref, w3_ref, o_ref, w1b_ref, w3b_ref):
    b = pl.program_id(1)
    active = b < nblk_ref[0]

    @pl.when(jnp.logical_and(active, _moe_run_start(eid_ref, b)))
    def _():
        _cast_tile(w1_ref, w1b_ref)
        _cast_tile(w3_ref, w3b_ref)

    @pl.when(active)
    def _():
        half = x_ref.shape[1]
        x_lo, x_hi = _unpack_bf16_pair(x_ref[...])

        def proj(wb_ref):
            return (jnp.dot(x_lo, wb_ref[:half, :], preferred_element_type=f32)
                    + jnp.dot(x_hi, wb_ref[half:, :], preferred_element_type=f32))

        o_ref[...] = (jax.nn.silu(proj(w1b_ref)) * proj(w3b_ref)).astype(o_ref.dtype)


def _moe_hidden(xs, w1, w3, block_eid, n_blocks_used):
    P = xs.shape[0]
    D = w1.shape[1]
    Bm, Ft = MOE_ROWS, MOE_F_TILE
    est = 2 * (Bm * D * 2 + 2 * D * Ft * 4 + Bm * Ft * 2) + 2 * D * Ft * 2 + 3 * Bm * Ft * 4 + Bm * D * 2
    return pl.pallas_call(
        _moe_hidden_body,
        grid_spec=pltpu.PrefetchScalarGridSpec(
            num_scalar_prefetch=2,
            grid=(D_EXPERT // Ft, P // Bm),
            in_specs=[pl.BlockSpec((Bm, D // 2), lambda f, b, eid, nblk: (_moe_block(b, nblk), 0)),
                      pl.BlockSpec((None, D, Ft), lambda f, b, eid, nblk: (eid[_moe_block(b, nblk)], 0, f)),
                      pl.BlockSpec((None, D, Ft), lambda f, b, eid, nblk: (eid[_moe_block(b, nblk)], 0, f))],
            out_specs=pl.BlockSpec((Bm, Ft), lambda f, b, eid, nblk: (_moe_block(b, nblk), f)),
            scratch_shapes=[pltpu.VMEM((D, Ft), bf16), pltpu.VMEM((D, Ft), bf16)],
        ),
        out_shape=jax.ShapeDtypeStruct((P, D_EXPERT), bf16),
        compiler_params=_params(("arbitrary", "arbitrary"), est),
        name="moe_expert_hidden",
    )(block_eid, n_blocks_used, xs, w1, w3)


def _moe_out_body(eid_ref, nblk_ref, h_ref, w2_ref, o_ref, w2b_ref):
    b = pl.program_id(1)
    active = b < nblk_ref[0]

    @pl.when(jnp.logical_and(active, _moe_run_start(eid_ref, b)))
    def _():
        _cast_tile(w2_ref, w2b_ref)

    @pl.when(active)
    def _():
        o_ref[...] = jnp.dot(h_ref[...], w2b_ref[...], preferred_element_type=f32)


def _moe_out(hid, w2, block_eid, n_blocks_used):
    P, F = hid.shape
    D = w2.shape[2]
    Bm, Nt = MOE_ROWS, MOE_N_TILE
    est = 2 * (Bm * F * 2 + F * Nt * 4 + Bm * Nt * 4) + F * Nt * 2 + Bm * Nt * 4
    return pl.pallas_call(
        _moe_out_body,
        grid_spec=pltpu.PrefetchScalarGridSpec(
            num_scalar_prefetch=2,
            grid=(D // Nt, P // Bm),
            in_specs=[pl.BlockSpec((Bm, F), lambda n, b, eid, nblk: (_moe_block(b, nblk), 0)),
                      pl.BlockSpec((None, F, Nt), lambda n, b, eid, nblk: (eid[_moe_block(b, nblk)], 0, n))],
            out_specs=pl.BlockSpec((Bm, Nt), lambda n, b, eid, nblk: (_moe_block(b, nblk), n)),
            scratch_shapes=[pltpu.VMEM((F, Nt), bf16)],
        ),
        out_shape=jax.ShapeDtypeStruct((P, D), f32),
        compiler_params=_params(("arbitrary", "arbitrary"), est),
        name="moe_expert_out",
    )(block_eid, n_blocks_used, hid, w2)


def _combine_body(dest_ref, h_ref, gate_ref, g_ref, y_hbm, o_ref, buf, ss_s, sem):
    R = GATHER_ROWS
    i = pl.program_id(0)
    n_steps = pl.num_programs(0)

    def start(step, slot):
        def issue(r, c):
            for k in range(TOP_K):
                _row_copy(y_hbm, dest_ref[(step * R + r) * TOP_K + k],
                          buf.at[slot, k, pl.ds(r, 1)], sem.at[slot]).start()
            return c
        lax.fori_loop(0, R, issue, 0, unroll=8)

    @pl.when(i == 0)
    def _():
        start(0, 0)

    @pl.when(i + 1 < n_steps)
    def _():
        start(i + 1, (i + 1) % 2)

    slot = i % 2
    for k in range(TOP_K):
        pltpu.make_async_copy(y_hbm.at[pl.ds(0, R)], buf.at[slot, k], sem.at[slot]).wait()
    g_fin = g_ref[...]
    D = h_ref.shape[1]
    RC = V7X_SUBLANES

    def chunk(c):
        return pl.ds(pl.multiple_of(c * RC, RC), RC)

    def accumulate(c, carry):
        rs = chunk(c)
        gate = gate_ref[rs, :]
        h = (h_ref[rs, :] + buf[slot, 0, rs, :] * gate[:, ROUTE_GATE_LANE:ROUTE_GATE_LANE + 1]
             + buf[slot, 1, rs, :] * gate[:, ROUTE_GATE_LANE + 1:ROUTE_GATE_LANE + 2])
        o_ref[rs, :] = h
        sq = h * h
        ss = sq[:, 0:V7X_LANES]
        for j in range(1, D // V7X_LANES):
            ss = ss + sq[:, j * V7X_LANES:(j + 1) * V7X_LANES]
        ss_s[rs, :] = ss
        return carry

    lax.fori_loop(0, R // RC, accumulate, 0, unroll=4)

    def normalize(c, carry):
        rs = chunk(c)
        ms = jnp.sum(ss_s[rs, :], axis=-1, keepdims=True) * (1.0 / D)
        o_ref[rs, :] = o_ref[rs, :] * lax.rsqrt(ms + EPS) * g_fin
        return carry

    lax.fori_loop(0, R // RC, normalize, 0, unroll=4)


def _moe_combine(h, y, dest, gate, n_final):
    T, D = h.shape
    R = GATHER_ROWS
    est = 2 * TOP_K * R * D * 4 + 4 * R * D * 4 + 3 * R * D * 4
    return pl.pallas_call(
        _combine_body,
        grid_spec=pltpu.PrefetchScalarGridSpec(
            num_scalar_prefetch=1,
            grid=(T // R,),
            in_specs=[pl.BlockSpec((R, D), lambda i, dest: (i, 0)),
                      pl.BlockSpec((R, V7X_LANES), lambda i, dest: (i, 0)),
                      pl.BlockSpec((1, D), lambda i, dest: (0, 0)),
                      pl.BlockSpec(memory_space=pl.ANY)],
            out_specs=pl.BlockSpec((R, D), lambda i, dest: (i, 0)),
            scratch_shapes=[pltpu.VMEM((2, TOP_K, R, D), f32), pltpu.VMEM((R, V7X_LANES), f32),
                            pltpu.SemaphoreType.DMA((2,))],
        ),
        out_shape=jax.ShapeDtypeStruct((T, D), f32),
        compiler_params=_params(("arbitrary",), est),
        name="moe_combine_norm",
    )(dest, h, gate, n_final.reshape(1, D).astype(f32), y)


def _slot_layout(rec):
    T = rec.shape[0]
    A = T * TOP_K
    eid = rec[:, ROUTE_EID_LANE:ROUTE_EID_LANE + TOP_K].astype(jnp.int32).reshape(A)
    onehot = (eid[:, None] == jnp.arange(N_EXPERTS, dtype=jnp.int32)[None, :]).astype(jnp.int32)
    csum = jnp.cumsum(onehot, axis=0)
    rank = jnp.sum(onehot * csum, axis=1) - 1
    counts = csum[-1]
    pcounts = (counts + MOE_ROWS - 1) // MOE_ROWS * MOE_ROWS
    pend = jnp.cumsum(pcounts)
    pstart = pend - pcounts
    dest = (pstart[eid] + rank).astype(jnp.int32)
    n_blocks = -(-(A + N_EXPERTS * (MOE_ROWS - 1)) // MOE_ROWS)
    P = n_blocks * MOE_ROWS
    block_eid = jnp.minimum(
        jnp.searchsorted(pend, jnp.arange(n_blocks, dtype=jnp.int32) * MOE_ROWS, side='right'),
        N_EXPERTS - 1).astype(jnp.int32)
    n_blocks_used = (pend[-1:] // MOE_ROWS).astype(jnp.int32)
    pad_end = pend.astype(jnp.int32)
    pad_len = (pcounts - counts).astype(jnp.int32)
    return dest, block_eid, n_blocks_used, pad_end, pad_len, P


def kernel(x, mem, n_mix, w_in, gmlp_ln_g, gmlp_ln_b, gmlp_w_s, gmlp_b_s, hgrn_lower_bounds,
           hgrn_norm_g, w_out, n_cross, n_mem, w_q_x, w_k_x, w_v_x, w_o_x, n_moe, w_group,
           b_group, w_router, b_router, w1_e, w3_e, w2_e, n_final):
    B, S, D = x.shape
    n_mem_tok = mem.shape[1]
    T = B * S
    xt = x.reshape(T, D)

    hn = _rmsnorm(xt, n_mix[0], bf16)
    z = _matmul(hn, w_in[0], bf16, 1024, 512, "mm_w_in")
    ya = _gmlp(z, gmlp_ln_g[0], gmlp_ln_b[0], gmlp_w_s[0].astype(f32), gmlp_b_s[0].astype(f32))
    yb = _hgrn(z, hgrn_lower_bounds, hgrn_norm_g[0], B, S)
    h1 = _matmul_res(xt, [ya, yb], w_out[0], 1024, 512, "mm_w_out")

    hc = _rmsnorm(h1, n_cross[0], bf16)
    mn = _rmsnorm(mem.reshape(B * n_mem_tok, D), n_mem[0], bf16)
    kx = _matmul(mn, w_k_x[0], bf16, B * n_mem_tok, 512, "mm_w_k")
    vx = _matmul(mn, w_v_x[0], bf16, B * n_mem_tok, 512, "mm_w_v")
    ox = _xattn(hc, w_q_x[0].astype(bf16), kx, vx, B, S, n_mem_tok)
    h2 = _matmul_res(h1, [ox], w_o_x[0], 1024, 512, "mm_w_o")

    xm, rec = _router(h2, n_moe[0], w_group[0], b_group[0], w_router[0], b_router[0])
    dest, block_eid, n_blocks_used, pad_end, pad_len, P = _slot_layout(rec)
    xs = _moe_dispatch(xm, dest, pad_end, pad_len, P)
    hid = _moe_hidden(xs, w1_e[0], w3_e[0], block_eid, n_blocks_used)
    y = _moe_out(hid, w2_e[0], block_eid, n_blocks_used)
    out = _moe_combine(h2, y, dest, rec, n_final)
    return out.reshape(B, S, D)
```

```python
import functools

import jax
import jax.numpy as jnp
from jax import lax
from jax.experimental import pallas as pl
from jax.experimental.pallas import tpu as pltpu

f32 = jnp.float32
bf16 = jnp.bfloat16

EPS = 1e-6
D_MODEL = 4096
CHUNK = 64
GMLP_HEADS = 4
GMLP_HEAD_DIM = 512
GMLP_BLOCK = 128
D_GMLP = GMLP_HEADS * GMLP_HEAD_DIM
HGRN_HEAD_DIM = 128
HGRN_HEADS = 16
D_HGRN = HGRN_HEADS * HGRN_HEAD_DIM
X_HEADS = 4
X_HEAD_DIM = 1024
N_GROUPS = 4
EXPERTS_PER_GROUP = 8
N_EXPERTS = N_GROUPS * EXPERTS_PER_GROUP
TOP_K = 2
D_EXPERT = 1024

V7X_LANES = 128
V7X_SUBLANES = 8
V7X_VMEM_LIMIT_BYTES = 60000 * 1024

MOE_ROWS = 256
MOE_F_TILE = 512
MOE_N_TILE = 4096
MOE_W_CHUNKS = 8
GATHER_ROWS = 256
HGRN_GROUP = 4
HGRN_ROWS = 512


def _params(sem, est_bytes):
    limit = min(int(est_bytes) + (8 << 20), V7X_VMEM_LIMIT_BYTES)
    return pltpu.CompilerParams(dimension_semantics=sem, vmem_limit_bytes=limit)


def _rms_body(x_ref, g_ref, o_ref):
    x = x_ref[...].astype(f32)
    ms = jnp.mean(x * x, axis=-1, keepdims=True)
    o_ref[...] = (x * lax.rsqrt(ms + EPS) * g_ref[...]).astype(o_ref.dtype)


def _rmsnorm(x, g, out_dtype, tm=256):
    M, D = x.shape
    est = 2 * tm * D * (x.dtype.itemsize + jnp.dtype(out_dtype).itemsize) + 2 * tm * D * 4
    return pl.pallas_call(
        _rms_body,
        grid=(M // tm,),
        in_specs=[pl.BlockSpec((tm, D), lambda i: (i, 0)),
                  pl.BlockSpec((1, D), lambda i: (0, 0))],
        out_specs=pl.BlockSpec((tm, D), lambda i: (i, 0)),
        out_shape=jax.ShapeDtypeStruct((M, D), out_dtype),
        compiler_params=_params(("arbitrary",), est),
        name="rmsnorm",
    )(x, g.reshape(1, D).astype(f32))


CAST_ROWS = 512


def _cast_tile(src_ref, dst_ref, dst_row0=0):
    rows = min(CAST_ROWS, src_ref.shape[0])
    n = src_ref.shape[0] // rows

    def piece(p, c):
        r = pl.multiple_of(p * rows, rows)
        d = pl.multiple_of(dst_row0 + p * rows, rows)
        dst_ref[pl.ds(d, rows), :] = src_ref[pl.ds(r, rows), :].astype(bf16)
        return c

    lax.fori_loop(0, n, piece, 0)


def _mm_body(a_ref, w_ref, o_ref, wb_ref):
    @pl.when(pl.program_id(1) == 0)
    def _():
        _cast_tile(w_ref, wb_ref)

    o_ref[...] = jnp.dot(a_ref[...], wb_ref[...], preferred_element_type=f32).astype(o_ref.dtype)


def _matmul(a, w, out_dtype, tm, tn, name):
    M, K = a.shape
    N = w.shape[1]
    osz = jnp.dtype(out_dtype).itemsize
    est = 2 * (tm * K * 2 + K * tn * 4 + tm * tn * osz) + K * tn * 2 + tm * tn * 4
    return pl.pallas_call(
        _mm_body,
        grid=(N // tn, M // tm),
        in_specs=[pl.BlockSpec((tm, K), lambda j, i: (i, 0)),
                  pl.BlockSpec((K, tn), lambda j, i: (0, j))],
        out_specs=pl.BlockSpec((tm, tn), lambda j, i: (i, j)),
        out_shape=jax.ShapeDtypeStruct((M, N), out_dtype),
        scratch_shapes=[pltpu.VMEM((K, tn), bf16)],
        compiler_params=_params(("arbitrary", "arbitrary"), est),
        name=name,
    )(a, w)


def _mm_res_body(n_parts, r_ref, *refs):
    a_refs = refs[:n_parts]
    w_refs = refs[n_parts:2 * n_parts]
    o_ref, wb_ref = refs[2 * n_parts:]
    kp = w_refs[0].shape[0]

    @pl.when(pl.program_id(1) == 0)
    def _():
        for p in range(n_parts):
            _cast_tile(w_refs[p], wb_ref, p * kp)

    acc = r_ref[...]
    for p in range(n_parts):
        acc = acc + jnp.dot(a_refs[p][...], wb_ref[p * kp:(p + 1) * kp, :], preferred_element_type=f32)
    o_ref[...] = acc


def _matmul_res(res, a_list, w, tm, tn, name):
    M, N = res.shape
    n = len(a_list)
    Kp = a_list[0].shape[1]
    a_specs = [pl.BlockSpec((tm, Kp), lambda j, i: (i, 0)) for _ in range(n)]
    w_specs = [pl.BlockSpec((Kp, tn), functools.partial(lambda j, i, p: (p, j), p=p)) for p in range(n)]
    est = 2 * (n * tm * Kp * 2 + n * Kp * tn * 4 + 2 * tm * tn * 4) + n * Kp * tn * 2 + 2 * tm * tn * 4
    return pl.pallas_call(
        functools.partial(_mm_res_body, n),
        grid=(N // tn, M // tm),
        in_specs=[pl.BlockSpec((tm, tn), lambda j, i: (i, j))] + a_specs + w_specs,
        out_specs=pl.BlockSpec((tm, tn), lambda j, i: (i, j)),
        out_shape=jax.ShapeDtypeStruct((M, N), f32),
        scratch_shapes=[pltpu.VMEM((n * Kp, tn), bf16)],
        compiler_params=_params(("arbitrary", "arbitrary"), est),
        name=name,
    )(res, *a_list, *([w] * n))


def _gelu(x):
    return 0.5 * x * (1.0 + lax.erf(x * 0.7071067811865476))


def _gmlp_body(u_ref, v_ref, lng_ref, lnb_ref, ws_ref, bs_ref, o_ref):
    rows = u_ref.shape[0]
    gv = _gelu(v_ref[...].astype(f32))
    mu = jnp.mean(gv, axis=-1, keepdims=True)
    xc = gv - mu
    var = jnp.mean(xc * xc, axis=-1, keepdims=True)
    vn = (xc * lax.rsqrt(var + EPS) * lng_ref[...] + lnb_ref[...]).astype(bf16)
    ci = lax.broadcasted_iota(jnp.int32, (GMLP_BLOCK, GMLP_BLOCK), 0) // CHUNK
    cj = lax.broadcasted_iota(jnp.int32, (GMLP_BLOCK, GMLP_BLOCK), 1) // CHUNK
    w = jnp.where(cj <= ci, ws_ref[0], 0.0).astype(bf16)
    bias = bs_ref[0]
    for blk in range(rows // GMLP_BLOCK):
        sl = slice(blk * GMLP_BLOCK, (blk + 1) * GMLP_BLOCK)
        s = jnp.dot(w, vn[sl], preferred_element_type=f32) + bias
        o_ref[sl, :] = (_gelu(u_ref[sl, :].astype(f32)) * s).astype(o_ref.dtype)


def _gmlp(z, ln_g, ln_b, w_s, b_s, tr=256):
    M = z.shape[0]
    C = GMLP_HEAD_DIM
    est = 2 * 3 * tr * C * 2 + 8 * tr * C * 4
    return pl.pallas_call(
        _gmlp_body,
        grid=(M // tr, GMLP_HEADS),
        in_specs=[pl.BlockSpec((tr, C), lambda i, h: (i, h)),
                  pl.BlockSpec((tr, C), lambda i, h: (i, GMLP_HEADS + h)),
                  pl.BlockSpec((1, C), lambda i, h: (0, h)),
                  pl.BlockSpec((1, C), lambda i, h: (0, h)),
                  pl.BlockSpec((1, GMLP_BLOCK, GMLP_BLOCK), lambda i, h: (h, 0, 0)),
                  pl.BlockSpec((1, GMLP_BLOCK, 1), lambda i, h: (h, 0, 0))],
        out_specs=pl.BlockSpec((tr, C), lambda i, h: (i, h)),
        out_shape=jax.ShapeDtypeStruct((M, D_GMLP), bf16),
        compiler_params=_params(("arbitrary", "arbitrary"), est),
        name="gmlp_mixer",
    )(z, z, ln_g.reshape(1, D_GMLP), ln_b.reshape(1, D_GMLP), w_s,
      b_s.reshape(GMLP_HEADS, GMLP_BLOCK, 1))


def _split3(x):
    hi = x.astype(bf16)
    r1 = x - hi.astype(f32)
    mid = r1.astype(bf16)
    lo = (r1 - mid.astype(f32)).astype(bf16)
    return hi, mid, lo


HGRN_STRIP = 2 * CHUNK
_NT = (((1,), (1,)), ((), ()))
_TN = (((0,), (0,)), ((), ()))


def _split2(x):
    hi = x.astype(bf16)
    mid = (x - hi.astype(f32)).astype(bf16)
    return hi, mid


def _hgrn_body(q_ref, f_ref, i_ref, g_ref, lbr_ref, ng_ref, o_ref,
               st_ref, qin_s, kin_s, qex_s, kdec_s, dec_s, am_s, xq_s, xk_s, inc_s, smat_s, oacc_s):
    C, K, G = CHUNK, HGRN_HEAD_DIM, HGRN_GROUP
    rows, W = q_ref.shape
    n_chunks = rows // C

    @pl.when(pl.program_id(2) == 0)
    def _():
        st_ref[...] = jnp.zeros_like(st_ref)

    r0 = lbr_ref[0:1, :]
    r1 = lbr_ref[1:2, :]
    m = jnp.maximum(r0, r1)
    e0 = jnp.exp(r0 - m)
    e1 = jnp.exp(r1 - m)
    lb = e0 / (e0 + e1)

    ti = lax.broadcasted_iota(jnp.int32, (C, C), 0)
    si = lax.broadcasted_iota(jnp.int32, (C, C), 1)
    tril = (si <= ti).astype(bf16)

    def chunk_rows(c):
        return pl.ds(pl.multiple_of(c * C, C), C)

    def gates(c, carry):
        rs = chunk_rows(c)
        fg = lb + (1.0 - lb) * jax.nn.sigmoid(f_ref[rs, :].astype(f32))
        hi, mid, lo = _split3(jnp.log(fg))
        b = (jnp.dot(tril, hi, preferred_element_type=f32)
             + jnp.dot(tril, mid, preferred_element_type=f32)
             + jnp.dot(tril, lo, preferred_element_type=f32))
        b_mid = b[C // 2:C // 2 + 1, :]
        b_last = b[C - 1:C, :]
        q_in = jax.nn.silu(q_ref[rs, :].astype(f32)) * jnp.exp(b - b_mid)
        k_in = (1.0 - fg) * jnp.exp(b_mid - b)
        qin_s[rs, :] = q_in.astype(bf16)
        kin_s[rs, :] = k_in.astype(bf16)
        qex_s[rs, :] = (q_in * jnp.exp(b_mid)).astype(bf16)
        kdec_s[rs, :] = (k_in * jnp.exp(b_last - b_mid)).astype(bf16)
        dec_s[pl.ds(pl.multiple_of(c * 8, 8), 8), :] = jnp.broadcast_to(jnp.exp(b_last), (8, W))
        return carry

    lax.fori_loop(0, n_chunks, gates, 0, unroll=2)

    @pl.when((pl.program_id(0) == 0) & (pl.program_id(1) == 0) & (pl.program_id(2) == 0))
    def _():
        am_s[...] = jnp.zeros_like(am_s)
        xq_s[...] = jnp.zeros_like(xq_s)
        xk_s[...] = jnp.zeros_like(xk_s)

    S = HGRN_STRIP
    tt = lax.broadcasted_iota(jnp.int32, (S, S), 0)
    ss = lax.broadcasted_iota(jnp.int32, (S, S), 1)
    keep = jnp.logical_and(ss <= tt, (ss // C) == (tt // C))

    for h in range(G):
        ls = slice(h * K, (h + 1) * K)
        for c in range(n_chunks):
            xq_s[h, c * C:(c + 1) * C, c * K:(c + 1) * K] = qex_s[c * C:(c + 1) * C, ls]
            xk_s[h, c * C:(c + 1) * C, c * K:(c + 1) * K] = kdec_s[c * C:(c + 1) * C, ls]

        a = lax.dot_general(qin_s[:, ls], kin_s[:, ls], _NT, preferred_element_type=f32)
        for r in range(rows // S):
            sl = slice(r * S, (r + 1) * S)
            am_s[sl, sl] = jnp.where(keep, a[sl, sl], 0.0).astype(bf16)
        oacc_s[:, ls] = jnp.dot(am_s[...], i_ref[:, ls], preferred_element_type=f32)

        inc_s[...] = lax.dot_general(i_ref[:, ls], xk_s[h], _TN, preferred_element_type=f32)

        st = st_ref[h]
        for c in range(n_chunks):
            cs = slice(c * K, (c + 1) * K)
            smat_s[:, cs] = st.astype(bf16)
            st = st * dec_s[c * 8:c * 8 + 1, ls] + inc_s[:, cs]
        st_ref[h] = st

        oacc_s[:, ls] += lax.dot_general(xq_s[h], smat_s[...], _NT, preferred_element_type=f32)

    ng = ng_ref[...]

    def outputs(c, carry):
        rs = chunk_rows(c)
        for h in range(G):
            ls = slice(h * K, (h + 1) * K)
            o = oacc_s[rs, ls]
            ms = jnp.mean(o * o, axis=-1, keepdims=True)
            o = o * lax.rsqrt(ms + EPS) * ng[:, ls]
            o = o * jax.nn.silu(g_ref[rs, ls].astype(f32))
            o_ref[rs, ls] = o.astype(o_ref.dtype)
        return carry

    lax.fori_loop(0, n_chunks, outputs, 0, unroll=2)


def _hgrn(z, lower_bounds, norm_g, batch, seq):
    M = z.shape[0]
    W = HGRN_GROUP * HGRN_HEAD_DIM
    ts = HGRN_ROWS
    n_s = seq // ts
    n_hg = D_HGRN // W
    col0 = 2 * D_GMLP // W

    def zspec(part):
        return pl.BlockSpec((ts, W), lambda b, hg, s, part=part: (b * n_s + s, col0 + part * n_hg + hg))

    G, K = HGRN_GROUP, HGRN_HEAD_DIM
    n_chunks = ts // CHUNK
    est = (2 * 5 * ts * W * 2 + 4 * ts * W * 2 + ts * W * 4 + ts * ts * 2 + 2 * G * ts * n_chunks * K * 2
           + K * n_chunks * K * 6 + G * K * K * 4 + 3 * ts * ts * 4)
    return pl.pallas_call(
        _hgrn_body,
        grid=(batch, n_hg, n_s),
        in_specs=[zspec(0), zspec(1), zspec(2), zspec(3),
                  pl.BlockSpec((2, W), lambda b, hg, s: (0, hg)),
                  pl.BlockSpec((1, W), lambda b, hg, s: (0, hg))],
        out_specs=pl.BlockSpec((ts, W), lambda b, hg, s: (b * n_s + s, hg)),
        out_shape=jax.ShapeDtypeStruct((M, D_HGRN), bf16),
        scratch_shapes=[pltpu.VMEM((G, K, K), f32),
                        pltpu.VMEM((ts, W), bf16),
                        pltpu.VMEM((ts, W), bf16),
                        pltpu.VMEM((ts, W), bf16),
                        pltpu.VMEM((ts, W), bf16),
                        pltpu.VMEM((n_chunks * 8, W), f32),
                        pltpu.VMEM((ts, ts), bf16),
                        pltpu.VMEM((G, ts, n_chunks * K), bf16),
                        pltpu.VMEM((G, ts, n_chunks * K), bf16),
                        pltpu.VMEM((K, n_chunks * K), f32),
                        pltpu.VMEM((K, n_chunks * K), bf16),
                        pltpu.VMEM((ts, W), f32)],
        compiler_params=_params(("arbitrary", "arbitrary", "arbitrary"), est),
        name="hgrn2_mixer",
    )(z, z, z, z, lower_bounds.astype(f32), norm_g.reshape(1, D_HGRN).astype(f32))


def _xattn_body(h_ref, wq_ref, k_ref, v_ref, o_ref):
    q = jnp.dot(h_ref[...], wq_ref[...], preferred_element_type=f32).astype(bf16)
    s = lax.dot_general(q, k_ref[...], (((1,), (1,)), ((), ())), preferred_element_type=f32)
    s = s * (X_HEAD_DIM ** -0.5)
    s = s - jnp.max(s, axis=-1, keepdims=True)
    p = jnp.exp(s)
    p = p / jnp.sum(p, axis=-1, keepdims=True)
    o_ref[...] = jnp.dot(p.astype(bf16), v_ref[...], preferred_element_type=f32).astype(o_ref.dtype)


def _xattn(hc, wq, k, v, batch, seq, n_mem, tq=512):
    M, D = hc.shape
    n_s = seq // tq
    dh = X_HEAD_DIM
    est = 2 * (tq * D * 2 + D * dh * 2 + 2 * n_mem * dh * 2 + tq * dh * 2) + 3 * tq * dh * 4
    return pl.pallas_call(
        _xattn_body,
        grid=(X_HEADS, batch, n_s),
        in_specs=[pl.BlockSpec((tq, D), lambda h, b, s: (b * n_s + s, 0)),
                  pl.BlockSpec((D, dh), lambda h, b, s: (0, h)),
                  pl.BlockSpec((n_mem, dh), lambda h, b, s: (b, h)),
                  pl.BlockSpec((n_mem, dh), lambda h, b, s: (b, h))],
        out_specs=pl.BlockSpec((tq, dh), lambda h, b, s: (b * n_s + s, h)),
        out_shape=jax.ShapeDtypeStruct((M, D), bf16),
        compiler_params=_params(("arbitrary", "arbitrary", "arbitrary"), est),
        name="xattn_qproj",
    )(hc, wq, k, v)


u32 = jnp.uint32
ROUTE_GATE_LANE = 0
ROUTE_EID_LANE = 2


def _pack_bf16_pair(lo, hi):
    lo32 = lax.bitcast_convert_type(lo.astype(f32), u32) >> 16
    hi32 = lax.bitcast_convert_type(hi.astype(f32), u32) & jnp.uint32(0xFFFF0000)
    return lo32 | hi32


def _unpack_bf16_pair(w):
    lo = lax.bitcast_convert_type(w << 16, f32).astype(bf16)
    hi = lax.bitcast_convert_type(w & jnp.uint32(0xFFFF0000), f32).astype(bf16)
    return lo, hi


def _first_lane_of(mask, lane):
    return jnp.min(jnp.where(mask, lane, V7X_LANES), axis=-1, keepdims=True)


def _router_body(h_ref, g_ref, wr_ref, br_ref, xm_ref, r_ref):
    x = h_ref[...]
    D = x.shape[1]
    ms = jnp.mean(x * x, axis=-1, keepdims=True)
    xn = x * lax.rsqrt(ms + EPS) * g_ref[...]
    xh, xm = _split2(xn)
    wh, wm = _split2(wr_ref[...])
    dot = functools.partial(jnp.dot, preferred_element_type=f32)
    acc = dot(xm, wh) + dot(xh, wm)
    acc = acc + dot(xh, wh)
    logits = acc + br_ref[...]

    xm_ref[...] = _pack_bf16_pair(xh[:, :D // 2], xh[:, D // 2:])

    lane = lax.broadcasted_iota(jnp.int32, logits.shape, 1)
    neg = jnp.float32(-jnp.inf)
    is_g = lane < N_GROUPS
    gl = jnp.where(is_g, logits, neg)
    gmax = jnp.max(gl, axis=-1, keepdims=True)
    grp = _first_lane_of(gl == gmax, lane)
    p_grp = 1.0 / jnp.sum(jnp.where(is_g, jnp.exp(logits - gmax), 0.0), axis=-1, keepdims=True)
    e0 = N_GROUPS + grp * EXPERTS_PER_GROUP
    el = jnp.where(jnp.logical_and(lane >= e0, lane < e0 + EXPERTS_PER_GROUP), logits, neg)
    m1 = jnp.max(el, axis=-1, keepdims=True)
    i1 = _first_lane_of(el == m1, lane)
    el2 = jnp.where(lane == i1, neg, el)
    m2 = jnp.max(el2, axis=-1, keepdims=True)
    i2 = _first_lane_of(el2 == m2, lane)
    t = jnp.exp(m2 - m1)
    g1 = p_grp * (1.0 / (1.0 + t))
    g2 = p_grp * (t / (1.0 + t))
    rec = jnp.where(lane == ROUTE_GATE_LANE, g1, 0.0)
    rec = jnp.where(lane == ROUTE_GATE_LANE + 1, g2, rec)
    rec = jnp.where(lane == ROUTE_EID_LANE, (i1 - N_GROUPS).astype(f32), rec)
    rec = jnp.where(lane == ROUTE_EID_LANE + 1, (i2 - N_GROUPS).astype(f32), rec)
    r_ref[...] = rec


def _router(h, n_moe, w_group, b_group, w_router, b_router, tm=256):
    M, D = h.shape
    n_log = N_GROUPS + N_EXPERTS
    wr = jnp.concatenate([w_group] + [w_router[g] for g in range(N_GROUPS)], axis=1)
    wr = jnp.pad(wr.astype(f32), ((0, 0), (0, V7X_LANES - n_log)))
    br = jnp.pad(jnp.concatenate([b_group, b_router.reshape(-1)]).astype(f32), (0, V7X_LANES - n_log))
    est = 2 * tm * D * 4 + 2 * D * V7X_LANES * 4 + 2 * tm * D * 2 + 6 * tm * D * 4
    return pl.pallas_call(
        _router_body,
        grid=(M // tm,),
        in_specs=[pl.BlockSpec((tm, D), lambda i: (i, 0)),
                  pl.BlockSpec((1, D), lambda i: (0, 0)),
                  pl.BlockSpec((D, V7X_LANES), lambda i: (0, 0)),
                  pl.BlockSpec((1, V7X_LANES), lambda i: (0, 0))],
        out_specs=[pl.BlockSpec((tm, D // 2), lambda i: (i, 0)),
                   pl.BlockSpec((tm, V7X_LANES), lambda i: (i, 0))],
        out_shape=[jax.ShapeDtypeStruct((M, D // 2), u32),
                   jax.ShapeDtypeStruct((M, V7X_LANES), f32)],
        compiler_params=_params(("arbitrary",), est),
        name="moe_router",
    )(h, n_moe.reshape(1, D).astype(f32), wr, br.reshape(1, V7X_LANES))


def _row_copy(src_hbm, row, dst, sem):
    return pltpu.make_async_copy(src_hbm.at[pl.ds(row, 1)], dst, sem)


PAD_BITS = tuple(1 << k for k in reversed(range((MOE_ROWS - 1).bit_length())))


def _dispatch_body(dest_ref, padend_ref, padlen_ref, xm_ref, xs_hbm, buf, zbuf, sem, zsem):
    R = GATHER_ROWS
    i = pl.program_id(0)
    n_steps = pl.num_programs(0)
    slot = i % 2

    def wait_slot(s):
        for _ in range(TOP_K):
            pltpu.make_async_copy(buf.at[s], xs_hbm.at[pl.ds(0, R)], sem.at[s]).wait()

    def pad_copies(fn):
        def per_expert(e, c):
            ln = padlen_ref[e]
            end = padend_ref[e]
            for bit in PAD_BITS:
                rows_per_copy = bit if bit >= V7X_SUBLANES else 1
                for j in range(bit // rows_per_copy):
                    @pl.when((ln & bit) != 0)
                    def _(bit=bit, end=end, j=j, n=rows_per_copy):
                        start = end - bit + j * n
                        if n > 1:
                            start = pl.multiple_of(start, V7X_SUBLANES)
                        fn(pltpu.make_async_copy(zbuf.at[pl.ds(0, n)], xs_hbm.at[pl.ds(start, n)], zsem))
                end = end - (ln & bit)
            return c
        lax.fori_loop(0, N_EXPERTS, per_expert, 0)

    @pl.when(i == 0)
    def _():
        zbuf[...] = jnp.zeros_like(zbuf)
        pad_copies(lambda cp: cp.start())

    @pl.when(i >= 2)
    def _():
        wait_slot(slot)

    buf[slot] = xm_ref[...]

    def issue(r, c):
        for k in range(TOP_K):
            d = dest_ref[(i * R + r) * TOP_K + k]
            pltpu.make_async_copy(buf.at[slot, pl.ds(r, 1)], xs_hbm.at[pl.ds(d, 1)], sem.at[slot]).start()
        return c

    lax.fori_loop(0, R, issue, 0, unroll=8)

    @pl.when(i == n_steps - 1)
    def _():
        if n_steps >= 2:
            wait_slot(1 - slot)
        wait_slot(slot)
        pad_copies(lambda cp: cp.wait())


def _moe_dispatch(xm, dest, pad_end, pad_len, P):
    T, Dh = xm.shape
    R = GATHER_ROWS
    est = 2 * R * Dh * 4 + 2 * R * Dh * 4 + PAD_BITS[0] * Dh * 4
    return pl.pallas_call(
        _dispatch_body,
        grid_spec=pltpu.PrefetchScalarGridSpec(
            num_scalar_prefetch=3,
            grid=(T // R,),
            in_specs=[pl.BlockSpec((R, Dh), lambda i, dest, pend, plen: (i, 0))],
            out_specs=pl.BlockSpec(memory_space=pl.ANY),
            scratch_shapes=[pltpu.VMEM((2, R, Dh), u32), pltpu.VMEM((PAD_BITS[0], Dh), u32),
                            pltpu.SemaphoreType.DMA((2,)), pltpu.SemaphoreType.DMA],
        ),
        out_shape=jax.ShapeDtypeStruct((P, Dh), u32),
        compiler_params=_params(("arbitrary",), est),
        name="moe_dispatch",
    )(dest, pad_end, pad_len, xm)


class _WeightStream:
    def __init__(self, w_hbms, wf_ref, wsem, col_tile):
        self.w_hbms, self.wf, self.wsem, self.ct = w_hbms, wf_ref, wsem, col_tile
        self.rows = wf_ref.shape[2]
        self.cr = self.rows // MOE_W_CHUNKS

    def _chunk(self, m, c, step, slot):
        e = step % N_EXPERTS
        t = step // N_EXPERTS
        r = pl.ds(pl.multiple_of(c * self.cr, self.cr), self.cr)
        cols = pl.ds(pl.multiple_of(t * self.ct, V7X_LANES), self.ct)
        return pltpu.make_async_copy(self.w_hbms[m].at[e, r, cols], self.wf.at[slot, m, r], self.wsem.at[slot])

    def start_chunk(self, c, step):
        for m in range(len(self.w_hbms)):
            self._chunk(m, c, step, step % 2).start()

    def start_chunks(self, c0, c1, step):
        lax.fori_loop(c0, c1, lambda c, carry: (self.start_chunk(c, step), carry)[1], 0)

    def wait(self, step):
        slot = step % 2
        for m in range(len(self.w_hbms)):
            for c in range(MOE_W_CHUNKS):
                self._chunk(m, c, step, slot).wait()

    def tile(self, m, step):
        return self.wf.at[step % 2, m]


def _expert_block_loop(row0, n_blocks, in_hbm, out_hbm, out_col0, ibuf, obuf, isem, osem, compute,
                       per_block):
    Bm = MOE_ROWS
    out_cols = obuf.shape[2]

    def rows(j):
        return pl.ds(pl.multiple_of(row0 + j * Bm, Bm), Bm)

    def in_copy(j, slot):
        return pltpu.make_async_copy(in_hbm.at[rows(j)], ibuf.at[slot], isem.at[slot])

    def out_copy(j, slot):
        return pltpu.make_async_copy(
            obuf.at[slot], out_hbm.at[rows(j), pl.ds(pl.multiple_of(out_col0, V7X_LANES), out_cols)],
            osem.at[slot])

    @pl.when(n_blocks > 0)
    def _():
        in_copy(0, 0).start()

    def block(j, carry):
        slot = j % 2

        @pl.when(j + 1 < n_blocks)
        def _():
            in_copy(j + 1, 1 - slot).start()

        per_block(j)
        in_copy(j, slot).wait()

        @pl.when(j >= 2)
        def _():
            out_copy(j - 2, slot).wait()

        obuf[slot] = compute(ibuf[slot])
        out_copy(j, slot).start()
        return carry

    lax.fori_loop(0, n_blocks, block, 0)

    @pl.when(n_blocks >= 2)
    def _():
        out_copy(n_blocks - 2, n_blocks % 2).wait()

    @pl.when(n_blocks >= 1)
    def _():
        out_copy(n_blocks - 1, (n_blocks - 1) % 2).wait()


def _expert_step(stream, n_blocks, cast, run_blocks):
    step = pl.program_id(0) * N_EXPERTS + pl.program_id(1)
    n_steps = pl.num_programs(0) * N_EXPERTS
    has_next = step + 1 < n_steps

    @pl.when(step == 0)
    def _():
        stream.start_chunks(0, MOE_W_CHUNKS, step)

    stream.wait(step)

    @pl.when(n_blocks > 0)
    def _():
        cast(step)

    per = (MOE_W_CHUNKS + jnp.maximum(n_blocks, 1) - 1) // jnp.maximum(n_blocks, 1)

    @pl.when(jnp.logical_and(has_next, n_blocks == 0))
    def _():
        stream.start_chunks(0, MOE_W_CHUNKS, step + 1)

    def per_block(j):
        @pl.when(has_next)
        def _():
            stream.start_chunks(jnp.minimum(j * per, MOE_W_CHUNKS), jnp.minimum((j + 1) * per, MOE_W_CHUNKS),
                                step + 1)

    run_blocks(per_block)


def _moe_hidden_body(blk0_ref, nblk_ref, w1_hbm, w3_hbm, xs_hbm, hid_hbm,
                     wf_ref, w1b_ref, w3b_ref, ibuf, obuf, wsem, isem, osem):
    f = pl.program_id(0)
    e = pl.program_id(1)
    n_blocks = nblk_ref[e]
    half = ibuf.shape[2]
    stream = _WeightStream((w1_hbm, w3_hbm), wf_ref, wsem, MOE_F_TILE)

    def cast(step):
        _cast_tile(stream.tile(0, step), w1b_ref)
        _cast_tile(stream.tile(1, step), w3b_ref)

    def compute(xw):
        x_lo, x_hi = _unpack_bf16_pair(xw)

        def proj(wb_ref):
            return (jnp.dot(x_lo, wb_ref[:half, :], preferred_element_type=f32)
                    + jnp.dot(x_hi, wb_ref[half:, :], preferred_element_type=f32))

        return (jax.nn.silu(proj(w1b_ref)) * proj(w3b_ref)).astype(obuf.dtype)

    _expert_step(stream, n_blocks, cast, functools.partial(
        _expert_block_loop, blk0_ref[e] * MOE_ROWS, n_blocks, xs_hbm, hid_hbm, f * MOE_F_TILE,
        ibuf, obuf, isem, osem, compute))


def _moe_hidden(xs, w1, w3, blk0, nblk):
    P, Dh = xs.shape
    D = w1.shape[1]
    Bm, Ft = MOE_ROWS, MOE_F_TILE
    est = 2 * 2 * D * Ft * 4 + 2 * D * Ft * 2 + 2 * Bm * Dh * 4 + 2 * Bm * Ft * 2 + Bm * D * 2 + 3 * Bm * Ft * 4
    return pl.pallas_call(
        _moe_hidden_body,
        grid_spec=pltpu.PrefetchScalarGridSpec(
            num_scalar_prefetch=2,
            grid=(D_EXPERT // Ft, N_EXPERTS),
            in_specs=[pl.BlockSpec(memory_space=pl.ANY)] * 3,
            out_specs=pl.BlockSpec(memory_space=pl.ANY),
            scratch_shapes=[pltpu.VMEM((2, 2, D, Ft), f32),
                            pltpu.VMEM((D, Ft), bf16), pltpu.VMEM((D, Ft), bf16),
                            pltpu.VMEM((2, Bm, Dh), u32), pltpu.VMEM((2, Bm, Ft), bf16),
                            pltpu.SemaphoreType.DMA((2,)),
                            pltpu.SemaphoreType.DMA((2,)), pltpu.SemaphoreType.DMA((2,))],
        ),
        out_shape=jax.ShapeDtypeStruct((P, D_EXPERT), bf16),
        compiler_params=_params(("arbitrary", "arbitrary"), est),
        name="moe_expert_hidden",
    )(blk0, nblk, w1, w3, xs)


def _moe_out_body(blk0_ref, nblk_ref, w2_hbm, hid_hbm, y_hbm, wf_ref, w2b_ref, ibuf, obuf,
                  wsem, isem, osem):
    n = pl.program_id(0)
    e = pl.program_id(1)
    n_blocks = nblk_ref[e]
    stream = _WeightStream((w2_hbm,), wf_ref, wsem, MOE_N_TILE)

    def cast(step):
        _cast_tile(stream.tile(0, step), w2b_ref)

    def compute(h):
        return jnp.dot(h, w2b_ref[...], preferred_element_type=f32)

    _expert_step(stream, n_blocks, cast, functools.partial(
        _expert_block_loop, blk0_ref[e] * MOE_ROWS, n_blocks, hid_hbm, y_hbm, n * MOE_N_TILE,
        ibuf, obuf, isem, osem, compute))


def _moe_out(hid, w2, blk0, nblk):
    P, F = hid.shape
    D = w2.shape[2]
    Bm, Nt = MOE_ROWS, MOE_N_TILE
    est = 2 * F * Nt * 4 + F * Nt * 2 + 2 * Bm * F * 2 + 2 * Bm * Nt * 4 + Bm * Nt * 4
    return pl.pallas_call(
        _moe_out_body,
        grid_spec=pltpu.PrefetchScalarGridSpec(
            num_scalar_prefetch=2,
            grid=(D // Nt, N_EXPERTS),
            in_specs=[pl.BlockSpec(memory_space=pl.ANY)] * 2,
            out_specs=pl.BlockSpec(memory_space=pl.ANY),
            scratch_shapes=[pltpu.VMEM((2, 1, F, Nt), f32), pltpu.VMEM((F, Nt), bf16),
                            pltpu.VMEM((2, Bm, F), bf16), pltpu.VMEM((2, Bm, Nt), f32),
                            pltpu.SemaphoreType.DMA((2,)),
                            pltpu.SemaphoreType.DMA((2,)), pltpu.SemaphoreType.DMA((2,))],
        ),
        out_shape=jax.ShapeDtypeStruct((P, D), f32),
        compiler_params=_params(("arbitrary", "arbitrary"), est),
        name="moe_expert_out",
    )(blk0, nblk, w2, hid)


def _combine_body(dest_ref, h_ref, gate_ref, g_ref, y_hbm, o_ref, buf, ss_s, sem):
    R = GATHER_ROWS
    i = pl.program_id(0)
    n_steps = pl.num_programs(0)

    def start(step, slot):
        def issue(r, c):
            for k in range(TOP_K):
                _row_copy(y_hbm, dest_ref[(step * R + r) * TOP_K + k],
                          buf.at[slot, k, pl.ds(r, 1)], sem.at[slot]).start()
            return c
        lax.fori_loop(0, R, issue, 0, unroll=8)

    @pl.when(i == 0)
    def _():
        start(0, 0)

    @pl.when(i + 1 < n_steps)
    def _():
        start(i + 1, (i + 1) % 2)

    slot = i % 2
    for k in range(TOP_K):
        pltpu.make_async_copy(y_hbm.at[pl.ds(0, R)], buf.at[slot, k], sem.at[slot]).wait()
    g_fin = g_ref[...]
    D = h_ref.shape[1]
    RC = V7X_SUBLANES

    def chunk(c):
        return pl.ds(pl.multiple_of(c * RC, RC), RC)

    def accumulate(c, carry):
        rs = chunk(c)
        gate = gate_ref[rs, :]
        h = (h_ref[rs, :] + buf[slot, 0, rs, :] * gate[:, ROUTE_GATE_LANE:ROUTE_GATE_LANE + 1]
             + buf[slot, 1, rs, :] * gate[:, ROUTE_GATE_LANE + 1:ROUTE_GATE_LANE + 2])
        o_ref[rs, :] = h
        sq = h * h
        ss = sq[:, 0:V7X_LANES]
        for j in range(1, D // V7X_LANES):
            ss = ss + sq[:, j * V7X_LANES:(j + 1) * V7X_LANES]
        ss_s[rs, :] = ss
        return carry

    lax.fori_loop(0, R // RC, accumulate, 0, unroll=4)

    def normalize(c, carry):
        rs = chunk(c)
        ms = jnp.sum(ss_s[rs, :], axis=-1, keepdims=True) * (1.0 / D)
        o_ref[rs, :] = o_ref[rs, :] * lax.rsqrt(ms + EPS) * g_fin
        return carry

    lax.fori_loop(0, R // RC, normalize, 0, unroll=4)


def _moe_combine(h, y, dest, gate, n_final):
    T, D = h.shape
    R = GATHER_ROWS
    est = 2 * TOP_K * R * D * 4 + 4 * R * D * 4 + 3 * R * D * 4
    return pl.pallas_call(
        _combine_body,
        grid_spec=pltpu.PrefetchScalarGridSpec(
            num_scalar_prefetch=1,
            grid=(T // R,),
            in_specs=[pl.BlockSpec((R, D), lambda i, dest: (i, 0)),
                      pl.BlockSpec((R, V7X_LANES), lambda i, dest: (i, 0)),
                      pl.BlockSpec((1, D), lambda i, dest: (0, 0)),
                      pl.BlockSpec(memory_space=pl.ANY)],
            out_specs=pl.BlockSpec((R, D), lambda i, dest: (i, 0)),
            scratch_shapes=[pltpu.VMEM((2, TOP_K, R, D), f32), pltpu.VMEM((R, V7X_LANES), f32),
                            pltpu.SemaphoreType.DMA((2,))],
        ),
        out_shape=jax.ShapeDtypeStruct((T, D), f32),
        compiler_params=_params(("arbitrary",), est),
        name="moe_combine_norm",
    )(dest, h, gate, n_final.reshape(1, D).astype(f32), y)


def _slot_layout(rec):
    T = rec.shape[0]
    A = T * TOP_K
    eid = rec[:, ROUTE_EID_LANE:ROUTE_EID_LANE + TOP_K].astype(jnp.int32).reshape(A)
    onehot = (eid[:, None] == jnp.arange(N_EXPERTS, dtype=jnp.int32)[None, :]).astype(jnp.int32)
    csum = jnp.cumsum(onehot, axis=0)
    rank = jnp.sum(onehot * csum, axis=1) - 1
    counts = csum[-1]
    pcounts = (counts + MOE_ROWS - 1) // MOE_ROWS * MOE_ROWS
    pend = jnp.cumsum(pcounts)
    pstart = pend - pcounts
    dest = (pstart[eid] + rank).astype(jnp.int32)
    n_blocks = -(-(A + N_EXPERTS * (MOE_ROWS - 1)) // MOE_ROWS)
    P = n_blocks * MOE_ROWS
    blk0 = (pstart // MOE_ROWS).astype(jnp.int32)
    nblk = (pcounts // MOE_ROWS).astype(jnp.int32)
    pad_end = pend.astype(jnp.int32)
    pad_len = (pcounts - counts).astype(jnp.int32)
    return dest, blk0, nblk, pad_end, pad_len, P


def kernel(x, mem, n_mix, w_in, gmlp_ln_g, gmlp_ln_b, gmlp_w_s, gmlp_b_s, hgrn_lower_bounds,
           hgrn_norm_g, w_out, n_cross, n_mem, w_q_x, w_k_x, w_v_x, w_o_x, n_moe, w_group,
           b_group, w_router, b_router, w1_e, w3_e, w2_e, n_final):
    B, S, D = x.shape
    n_mem_tok = mem.shape[1]
    T = B * S
    xt = x.reshape(T, D)

    hn = _rmsnorm(xt, n_mix[0], bf16)
    z = _matmul(hn, w_in[0], bf16, 1024, 512, "mm_w_in")
    ya = _gmlp(z, gmlp_ln_g[0], gmlp_ln_b[0], gmlp_w_s[0].astype(f32), gmlp_b_s[0].astype(f32))
    yb = _hgrn(z, hgrn_lower_bounds, hgrn_norm_g[0], B, S)
    h1 = _matmul_res(xt, [ya, yb], w_out[0], 1024, 512, "mm_w_out")

    hc = _rmsnorm(h1, n_cross[0], bf16)
    mn = _rmsnorm(mem.reshape(B * n_mem_tok, D), n_mem[0], bf16)
    kx = _matmul(mn, w_k_x[0], bf16, B * n_mem_tok, 512, "mm_w_k")
    vx = _matmul(mn, w_v_x[0], bf16, B * n_mem_tok, 512, "mm_w_v")
    ox = _xattn(hc, w_q_x[0].astype(bf16), kx, vx, B, S, n_mem_tok)
    h2 = _matmul_res(h1, [ox], w_o_x[0], 1024, 512, "mm_w_o")

    xm, rec = _router(h2, n_moe[0], w_group[0], b_group[0], w_router[0], b_router[0])
    dest, blk0, nblk, pad_end, pad_len, P = _slot_layout(rec)
    xs = _moe_dispatch(xm, dest, pad_end, pad_len, P)
    hid = _moe_hidden(xs, w1_e[0], w3_e[0], blk0, nblk)
    y = _moe_out(hid, w2_e[0], blk0, nblk)
    out = _moe_combine(h2, y, dest, rec, n_final)
    return out.reshape(B, S, D)
```

```python
import functools

import jax
import jax.numpy as jnp
from jax import lax
from jax.experimental import pallas as pl
from jax.experimental.pallas import tpu as pltpu

f32 = jnp.float32
bf16 = jnp.bfloat16

EPS = 1e-6
D_MODEL = 4096
CHUNK = 64
GMLP_HEADS = 4
GMLP_HEAD_DIM = 512
GMLP_BLOCK = 128
D_GMLP = GMLP_HEADS * GMLP_HEAD_DIM
HGRN_HEAD_DIM = 128
HGRN_HEADS = 16
D_HGRN = HGRN_HEADS * HGRN_HEAD_DIM
X_HEADS = 4
X_HEAD_DIM = 1024
N_GROUPS = 4
EXPERTS_PER_GROUP = 8
N_EXPERTS = N_GROUPS * EXPERTS_PER_GROUP
TOP_K = 2
D_EXPERT = 1024

V7X_LANES = 128
V7X_SUBLANES = 8
V7X_VMEM_LIMIT_BYTES = 60000 * 1024

MOE_ROWS = 256
MOE_F_TILE = 512
MOE_N_TILE = 4096
MOE_W_CHUNKS = 8
GATHER_ROWS = 256
HGRN_GROUP = 4
HGRN_ROWS = 512


def _params(sem, est_bytes):
    limit = min(int(est_bytes) + (8 << 20), V7X_VMEM_LIMIT_BYTES)
    return pltpu.CompilerParams(dimension_semantics=sem, vmem_limit_bytes=limit)


def _rms_body(x_ref, g_ref, o_ref):
    x = x_ref[...].astype(f32)
    ms = jnp.mean(x * x, axis=-1, keepdims=True)
    o_ref[...] = (x * lax.rsqrt(ms + EPS) * g_ref[...]).astype(o_ref.dtype)


def _rmsnorm(x, g, out_dtype, tm=256):
    M, D = x.shape
    est = 2 * tm * D * (x.dtype.itemsize + jnp.dtype(out_dtype).itemsize) + 2 * tm * D * 4
    return pl.pallas_call(
        _rms_body,
        grid=(M // tm,),
        in_specs=[pl.BlockSpec((tm, D), lambda i: (i, 0)),
                  pl.BlockSpec((1, D), lambda i: (0, 0))],
        out_specs=pl.BlockSpec((tm, D), lambda i: (i, 0)),
        out_shape=jax.ShapeDtypeStruct((M, D), out_dtype),
        compiler_params=_params(("arbitrary",), est),
        name="rmsnorm",
    )(x, g.reshape(1, D).astype(f32))


CAST_ROWS = 512


def _cast_tile(src_ref, dst_ref, dst_row0=0):
    rows = min(CAST_ROWS, src_ref.shape[0])
    n = src_ref.shape[0] // rows

    def piece(p, c):
        r = pl.multiple_of(p * rows, rows)
        d = pl.multiple_of(dst_row0 + p * rows, rows)
        dst_ref[pl.ds(d, rows), :] = src_ref[pl.ds(r, rows), :].astype(bf16)
        return c

    lax.fori_loop(0, n, piece, 0)


def _mm_body(a_ref, w_ref, o_ref, wb_ref):
    @pl.when(pl.program_id(1) == 0)
    def _():
        _cast_tile(w_ref, wb_ref)

    o_ref[...] = jnp.dot(a_ref[...], wb_ref[...], preferred_element_type=f32).astype(o_ref.dtype)


def _matmul(a, w, out_dtype, tm, tn, name):
    M, K = a.shape
    N = w.shape[1]
    osz = jnp.dtype(out_dtype).itemsize
    est = 2 * (tm * K * 2 + K * tn * 4 + tm * tn * osz) + K * tn * 2 + tm * tn * 4
    return pl.pallas_call(
        _mm_body,
        grid=(N // tn, M // tm),
        in_specs=[pl.BlockSpec((tm, K), lambda j, i: (i, 0)),
                  pl.BlockSpec((K, tn), lambda j, i: (0, j))],
        out_specs=pl.BlockSpec((tm, tn), lambda j, i: (i, j)),
        out_shape=jax.ShapeDtypeStruct((M, N), out_dtype),
        scratch_shapes=[pltpu.VMEM((K, tn), bf16)],
        compiler_params=_params(("arbitrary", "arbitrary"), est),
        name=name,
    )(a, w)


def _mm_stream_body(n_parts, has_res, *refs):
    refs = list(refs)
    r_ref = refs.pop(0) if has_res else None
    a_refs = [refs.pop(0) for _ in range(n_parts)]
    w_hbm, o_ref, wb_ref, stage_ref, sem = refs
    j = pl.program_id(0)
    i = pl.program_id(1)
    n_j = pl.num_programs(0)
    n_i = pl.num_programs(1)
    cr, tn = stage_ref.shape[1], stage_ref.shape[2]

    def chunk_copy(tile, c, slot):
        src = w_hbm.at[pl.ds(pl.multiple_of(c * cr, cr), cr), pl.ds(pl.multiple_of(tile * tn, tn), tn)]
        return pltpu.make_async_copy(src, stage_ref.at[slot], sem.at[slot])

    def land(tile, c):
        slot = c % 2
        chunk_copy(tile, c, slot).wait()
        wb_ref[tile % 2, pl.ds(pl.multiple_of(c * cr, cr), cr), :] = stage_ref[slot].astype(bf16)

    @pl.when(jnp.logical_and(j == 0, i == 0))
    def _():
        def first(c, carry):
            chunk_copy(0, c, c % 2).start()
            land(0, c)
            return carry
        lax.fori_loop(0, n_i, first, 0)

    @pl.when(jnp.logical_and(j > 0, i == 0))
    def _():
        land(j, n_i - 1)

    @pl.when(jnp.logical_and(j + 1 < n_j, i > 0))
    def _():
        land(j + 1, i - 1)

    @pl.when(j + 1 < n_j)
    def _():
        chunk_copy(j + 1, i, i % 2).start()

    kp = a_refs[0].shape[1]
    acc = None
    for p in range(n_parts):
        part = jnp.dot(a_refs[p][...], wb_ref[j % 2, p * kp:(p + 1) * kp, :], preferred_element_type=f32)
        acc = part if acc is None else acc + part
    if has_res:
        acc = r_ref[...] + acc
    o_ref[...] = acc.astype(o_ref.dtype)


def _matmul_stream(a_list, w, out_dtype, tm, tn, name, res=None):
    M = a_list[0].shape[0]
    K, N = w.shape
    n = len(a_list)
    Kp = K // n
    n_i = M // tm
    cr = K // n_i
    osz = jnp.dtype(out_dtype).itemsize
    has_res = res is not None
    est = (2 * (tm * K * 2 + tm * tn * osz + (tm * tn * 4 if has_res else 0))
           + 2 * K * tn * 2 + 2 * cr * tn * 4 + 2 * tm * tn * 4)
    tile_spec = pl.BlockSpec((tm, tn), lambda j, i: (i, j))
    return pl.pallas_call(
        functools.partial(_mm_stream_body, n, has_res),
        grid=(N // tn, n_i),
        in_specs=([tile_spec] if has_res else [])
        + [pl.BlockSpec((tm, Kp), lambda j, i: (i, 0)) for _ in range(n)]
        + [pl.BlockSpec(memory_space=pl.ANY)],
        out_specs=tile_spec,
        out_shape=jax.ShapeDtypeStruct((M, N), out_dtype),
        scratch_shapes=[pltpu.VMEM((2, K, tn), bf16), pltpu.VMEM((2, cr, tn), f32),
                        pltpu.SemaphoreType.DMA((2,))],
        compiler_params=_params(("arbitrary", "arbitrary"), est),
        name=name,
    )(*([res] if has_res else []), *a_list, w)


def _gelu(x):
    return 0.5 * x * (1.0 + lax.erf(x * 0.7071067811865476))


def _gmlp_body(u_ref, v_ref, lng_ref, lnb_ref, ws_ref, bs_ref, o_ref):
    rows = u_ref.shape[0]
    gv = _gelu(v_ref[...].astype(f32))
    mu = jnp.mean(gv, axis=-1, keepdims=True)
    xc = gv - mu
    var = jnp.mean(xc * xc, axis=-1, keepdims=True)
    vn = (xc * lax.rsqrt(var + EPS) * lng_ref[...] + lnb_ref[...]).astype(bf16)
    ci = lax.broadcasted_iota(jnp.int32, (GMLP_BLOCK, GMLP_BLOCK), 0) // CHUNK
    cj = lax.broadcasted_iota(jnp.int32, (GMLP_BLOCK, GMLP_BLOCK), 1) // CHUNK
    w = jnp.where(cj <= ci, ws_ref[0], 0.0).astype(bf16)
    bias = bs_ref[0]
    for blk in range(rows // GMLP_BLOCK):
        sl = slice(blk * GMLP_BLOCK, (blk + 1) * GMLP_BLOCK)
        s = jnp.dot(w, vn[sl], preferred_element_type=f32) + bias
        o_ref[sl, :] = (_gelu(u_ref[sl, :].astype(f32)) * s).astype(o_ref.dtype)


def _gmlp(z, ln_g, ln_b, w_s, b_s, tr=256):
    M = z.shape[0]
    C = GMLP_HEAD_DIM
    est = 2 * 3 * tr * C * 2 + 8 * tr * C * 4
    return pl.pallas_call(
        _gmlp_body,
        grid=(M // tr, GMLP_HEADS),
        in_specs=[pl.BlockSpec((tr, C), lambda i, h: (i, h)),
                  pl.BlockSpec((tr, C), lambda i, h: (i, GMLP_HEADS + h)),
                  pl.BlockSpec((1, C), lambda i, h: (0, h)),
                  pl.BlockSpec((1, C), lambda i, h: (0, h)),
                  pl.BlockSpec((1, GMLP_BLOCK, GMLP_BLOCK), lambda i, h: (h, 0, 0)),
                  pl.BlockSpec((1, GMLP_BLOCK, 1), lambda i, h: (h, 0, 0))],
        out_specs=pl.BlockSpec((tr, C), lambda i, h: (i, h)),
        out_shape=jax.ShapeDtypeStruct((M, D_GMLP), bf16),
        compiler_params=_params(("arbitrary", "arbitrary"), est),
        name="gmlp_mixer",
    )(z, z, ln_g.reshape(1, D_GMLP), ln_b.reshape(1, D_GMLP), w_s,
      b_s.reshape(GMLP_HEADS, GMLP_BLOCK, 1))


def _split3(x):
    hi = x.astype(bf16)
    r1 = x - hi.astype(f32)
    mid = r1.astype(bf16)
    lo = (r1 - mid.astype(f32)).astype(bf16)
    return hi, mid, lo


HGRN_STRIP = 2 * CHUNK
_NT = (((1,), (1,)), ((), ()))
_TN = (((0,), (0,)), ((), ()))


def _split2(x):
    hi = x.astype(bf16)
    mid = (x - hi.astype(f32)).astype(bf16)
    return hi, mid


def _hgrn_body(q_ref, f_ref, i_ref, g_ref, lbr_ref, ng_ref, o_ref,
               st_ref, qin_s, kin_s, qex_s, kdec_s, dec_s, am_s, xq_s, xk_s, inc_s, smat_s, oacc_s):
    C, K, G = CHUNK, HGRN_HEAD_DIM, HGRN_GROUP
    rows, W = q_ref.shape
    n_chunks = rows // C

    @pl.when(pl.program_id(2) == 0)
    def _():
        st_ref[...] = jnp.zeros_like(st_ref)

    r0 = lbr_ref[0:1, :]
    r1 = lbr_ref[1:2, :]
    m = jnp.maximum(r0, r1)
    e0 = jnp.exp(r0 - m)
    e1 = jnp.exp(r1 - m)
    lb = e0 / (e0 + e1)

    ti = lax.broadcasted_iota(jnp.int32, (C, C), 0)
    si = lax.broadcasted_iota(jnp.int32, (C, C), 1)
    tril = (si <= ti).astype(bf16)

    def chunk_rows(c):
        return pl.ds(pl.multiple_of(c * C, C), C)

    def gates(c, carry):
        rs = chunk_rows(c)
        fg = lb + (1.0 - lb) * jax.nn.sigmoid(f_ref[rs, :].astype(f32))
        hi, mid, lo = _split3(jnp.log(fg))
        b = (jnp.dot(tril, hi, preferred_element_type=f32)
             + jnp.dot(tril, mid, preferred_element_type=f32)
             + jnp.dot(tril, lo, preferred_element_type=f32))
        b_mid = b[C // 2:C // 2 + 1, :]
        b_last = b[C - 1:C, :]
        q_in = jax.nn.silu(q_ref[rs, :].astype(f32)) * jnp.exp(b - b_mid)
        k_in = (1.0 - fg) * jnp.exp(b_mid - b)
        qin_s[rs, :] = q_in.astype(bf16)
        kin_s[rs, :] = k_in.astype(bf16)
        qex_s[rs, :] = (q_in * jnp.exp(b_mid)).astype(bf16)
        kdec_s[rs, :] = (k_in * jnp.exp(b_last - b_mid)).astype(bf16)
        dec_s[pl.ds(pl.multiple_of(c * 8, 8), 8), :] = jnp.broadcast_to(jnp.exp(b_last), (8, W))
        return carry

    lax.fori_loop(0, n_chunks, gates, 0, unroll=2)

    @pl.when((pl.program_id(0) == 0) & (pl.program_id(1) == 0) & (pl.program_id(2) == 0))
    def _():
        am_s[...] = jnp.zeros_like(am_s)
        xq_s[...] = jnp.zeros_like(xq_s)
        xk_s[...] = jnp.zeros_like(xk_s)

    S = HGRN_STRIP
    tt = lax.broadcasted_iota(jnp.int32, (S, S), 0)
    ss = lax.broadcasted_iota(jnp.int32, (S, S), 1)
    keep = jnp.logical_and(ss <= tt, (ss // C) == (tt // C))

    for h in range(G):
        ls = slice(h * K, (h + 1) * K)
        for c in range(n_chunks):
            xq_s[h, c * C:(c + 1) * C, c * K:(c + 1) * K] = qex_s[c * C:(c + 1) * C, ls]
            xk_s[h, c * C:(c + 1) * C, c * K:(c + 1) * K] = kdec_s[c * C:(c + 1) * C, ls]

        a = lax.dot_general(qin_s[:, ls], kin_s[:, ls], _NT, preferred_element_type=f32)
        for r in range(rows // S):
            sl = slice(r * S, (r + 1) * S)
            am_s[sl, sl] = jnp.where(keep, a[sl, sl], 0.0).astype(bf16)
        oacc_s[:, ls] = jnp.dot(am_s[...], i_ref[:, ls], preferred_element_type=f32)

        inc_s[...] = lax.dot_general(i_ref[:, ls], xk_s[h], _TN, preferred_element_type=f32)

        st = st_ref[h]
        for c in range(n_chunks):
            cs = slice(c * K, (c + 1) * K)
            smat_s[:, cs] = st.astype(bf16)
            st = st * dec_s[c * 8:c * 8 + 1, ls] + inc_s[:, cs]
        st_ref[h] = st

        oacc_s[:, ls] += lax.dot_general(xq_s[h], smat_s[...], _NT, preferred_element_type=f32)

    ng = ng_ref[...]

    def outputs(c, carry):
        rs = chunk_rows(c)
        for h in range(G):
            ls = slice(h * K, (h + 1) * K)
            o = oacc_s[rs, ls]
            ms = jnp.mean(o * o, axis=-1, keepdims=True)
            o = o * lax.rsqrt(ms + EPS) * ng[:, ls]
            o = o * jax.nn.silu(g_ref[rs, ls].astype(f32))
            o_ref[rs, ls] = o.astype(o_ref.dtype)
        return carry

    lax.fori_loop(0, n_chunks, outputs, 0, unroll=2)


def _hgrn(z, lower_bounds, norm_g, batch, seq):
    M = z.shape[0]
    W = HGRN_GROUP * HGRN_HEAD_DIM
    ts = HGRN_ROWS
    n_s = seq // ts
    n_hg = D_HGRN // W
    col0 = 2 * D_GMLP // W

    def zspec(part):
        return pl.BlockSpec((ts, W), lambda b, hg, s, part=part: (b * n_s + s, col0 + part * n_hg + hg))

    G, K = HGRN_GROUP, HGRN_HEAD_DIM
    n_chunks = ts // CHUNK
    est = (2 * 5 * ts * W * 2 + 4 * ts * W * 2 + ts * W * 4 + ts * ts * 2 + 2 * G * ts * n_chunks * K * 2
           + K * n_chunks * K * 6 + G * K * K * 4 + 3 * ts * ts * 4)
    return pl.pallas_call(
        _hgrn_body,
        grid=(batch, n_hg, n_s),
        in_specs=[zspec(0), zspec(1), zspec(2), zspec(3),
                  pl.BlockSpec((2, W), lambda b, hg, s: (0, hg)),
                  pl.BlockSpec((1, W), lambda b, hg, s: (0, hg))],
        out_specs=pl.BlockSpec((ts, W), lambda b, hg, s: (b * n_s + s, hg)),
        out_shape=jax.ShapeDtypeStruct((M, D_HGRN), bf16),
        scratch_shapes=[pltpu.VMEM((G, K, K), f32),
                        pltpu.VMEM((ts, W), bf16),
                        pltpu.VMEM((ts, W), bf16),
                        pltpu.VMEM((ts, W), bf16),
                        pltpu.VMEM((ts, W), bf16),
                        pltpu.VMEM((n_chunks * 8, W), f32),
                        pltpu.VMEM((ts, ts), bf16),
                        pltpu.VMEM((G, ts, n_chunks * K), bf16),
                        pltpu.VMEM((G, ts, n_chunks * K), bf16),
                        pltpu.VMEM((K, n_chunks * K), f32),
                        pltpu.VMEM((K, n_chunks * K), bf16),
                        pltpu.VMEM((ts, W), f32)],
        compiler_params=_params(("arbitrary", "arbitrary", "arbitrary"), est),
        name="hgrn2_mixer",
    )(z, z, z, z, lower_bounds.astype(f32), norm_g.reshape(1, D_HGRN).astype(f32))


def _xattn_body(h_ref, wq_ref, k_ref, v_ref, o_ref):
    q = jnp.dot(h_ref[...], wq_ref[...], preferred_element_type=f32).astype(bf16)
    s = lax.dot_general(q, k_ref[...], (((1,), (1,)), ((), ())), preferred_element_type=f32)
    s = s * (X_HEAD_DIM ** -0.5)
    s = s - jnp.max(s, axis=-1, keepdims=True)
    p = jnp.exp(s)
    p = p / jnp.sum(p, axis=-1, keepdims=True)
    o_ref[...] = jnp.dot(p.astype(bf16), v_ref[...], preferred_element_type=f32).astype(o_ref.dtype)


def _xattn(hc, wq, k, v, batch, seq, n_mem, tq=512):
    M, D = hc.shape
    n_s = seq // tq
    dh = X_HEAD_DIM
    est = 2 * (tq * D * 2 + D * dh * 2 + 2 * n_mem * dh * 2 + tq * dh * 2) + 3 * tq * dh * 4
    return pl.pallas_call(
        _xattn_body,
        grid=(X_HEADS, batch, n_s),
        in_specs=[pl.BlockSpec((tq, D), lambda h, b, s: (b * n_s + s, 0)),
                  pl.BlockSpec((D, dh), lambda h, b, s: (0, h)),
                  pl.BlockSpec((n_mem, dh), lambda h, b, s: (b, h)),
                  pl.BlockSpec((n_mem, dh), lambda h, b, s: (b, h))],
        out_specs=pl.BlockSpec((tq, dh), lambda h, b, s: (b * n_s + s, h)),
        out_shape=jax.ShapeDtypeStruct((M, D), bf16),
        compiler_params=_params(("arbitrary", "arbitrary", "arbitrary"), est),
        name="xattn_qproj",
    )(hc, wq, k, v)


u32 = jnp.uint32
ROUTE_GATE_LANE = 0
ROUTE_EID_LANE = 2


def _pack_bf16_pair(lo, hi):
    lo32 = lax.bitcast_convert_type(lo.astype(f32), u32) >> 16
    hi32 = lax.bitcast_convert_type(hi.astype(f32), u32) & jnp.uint32(0xFFFF0000)
    return lo32 | hi32


def _unpack_bf16_pair(w):
    lo = lax.bitcast_convert_type(w << 16, f32).astype(bf16)
    hi = lax.bitcast_convert_type(w & jnp.uint32(0xFFFF0000), f32).astype(bf16)
    return lo, hi


def _first_lane_of(mask, lane):
    return jnp.min(jnp.where(mask, lane, V7X_LANES), axis=-1, keepdims=True)


def _router_body(h_ref, g_ref, wr_ref, br_ref, xm_ref, r_ref):
    x = h_ref[...]
    D = x.shape[1]
    ms = jnp.mean(x * x, axis=-1, keepdims=True)
    xn = x * lax.rsqrt(ms + EPS) * g_ref[...]
    xh, xm = _split2(xn)
    wh, wm = _split2(wr_ref[...])
    dot = functools.partial(jnp.dot, preferred_element_type=f32)
    acc = dot(xm, wh) + dot(xh, wm)
    acc = acc + dot(xh, wh)
    logits = acc + br_ref[...]

    xm_ref[...] = _pack_bf16_pair(xh[:, :D // 2], xh[:, D // 2:])

    lane = lax.broadcasted_iota(jnp.int32, logits.shape, 1)
    neg = jnp.float32(-jnp.inf)
    is_g = lane < N_GROUPS
    gl = jnp.where(is_g, logits, neg)
    gmax = jnp.max(gl, axis=-1, keepdims=True)
    grp = _first_lane_of(gl == gmax, lane)
    p_grp = 1.0 / jnp.sum(jnp.where(is_g, jnp.exp(logits - gmax), 0.0), axis=-1, keepdims=True)
    e0 = N_GROUPS + grp * EXPERTS_PER_GROUP
    el = jnp.where(jnp.logical_and(lane >= e0, lane < e0 + EXPERTS_PER_GROUP), logits, neg)
    m1 = jnp.max(el, axis=-1, keepdims=True)
    i1 = _first_lane_of(el == m1, lane)
    el2 = jnp.where(lane == i1, neg, el)
    m2 = jnp.max(el2, axis=-1, keepdims=True)
    i2 = _first_lane_of(el2 == m2, lane)
    t = jnp.exp(m2 - m1)
    g1 = p_grp * (1.0 / (1.0 + t))
    g2 = p_grp * (t / (1.0 + t))
    rec = jnp.where(lane == ROUTE_GATE_LANE, g1, 0.0)
    rec = jnp.where(lane == ROUTE_GATE_LANE + 1, g2, rec)
    rec = jnp.where(lane == ROUTE_EID_LANE, (i1 - N_GROUPS).astype(f32), rec)
    rec = jnp.where(lane == ROUTE_EID_LANE + 1, (i2 - N_GROUPS).astype(f32), rec)
    r_ref[...] = rec


def _router(h, n_moe, w_group, b_group, w_router, b_router, tm=256):
    M, D = h.shape
    n_log = N_GROUPS + N_EXPERTS
    wr = jnp.concatenate([w_group] + [w_router[g] for g in range(N_GROUPS)], axis=1)
    wr = jnp.pad(wr.astype(f32), ((0, 0), (0, V7X_LANES - n_log)))
    br = jnp.pad(jnp.concatenate([b_group, b_router.reshape(-1)]).astype(f32), (0, V7X_LANES - n_log))
    est = 2 * tm * D * 4 + 2 * D * V7X_LANES * 4 + 2 * tm * D * 2 + 6 * tm * D * 4
    return pl.pallas_call(
        _router_body,
        grid=(M // tm,),
        in_specs=[pl.BlockSpec((tm, D), lambda i: (i, 0)),
                  pl.BlockSpec((1, D), lambda i: (0, 0)),
                  pl.BlockSpec((D, V7X_LANES), lambda i: (0, 0)),
                  pl.BlockSpec((1, V7X_LANES), lambda i: (0, 0))],
        out_specs=[pl.BlockSpec((tm, D // 2), lambda i: (i, 0)),
                   pl.BlockSpec((tm, V7X_LANES), lambda i: (i, 0))],
        out_shape=[jax.ShapeDtypeStruct((M, D // 2), u32),
                   jax.ShapeDtypeStruct((M, V7X_LANES), f32)],
        compiler_params=_params(("arbitrary",), est),
        name="moe_router",
    )(h, n_moe.reshape(1, D).astype(f32), wr, br.reshape(1, V7X_LANES))


def _row_copy(src_hbm, row, dst, sem):
    return pltpu.make_async_copy(src_hbm.at[pl.ds(row, 1)], dst, sem)


PAD_BITS = tuple(1 << k for k in reversed(range((MOE_ROWS - 1).bit_length())))


def _dispatch_body(dest_ref, padend_ref, padlen_ref, xm_ref, xs_hbm, buf, zbuf, sem, zsem):
    R = GATHER_ROWS
    i = pl.program_id(0)
    n_steps = pl.num_programs(0)
    slot = i % 2

    def wait_slot(s):
        for _ in range(TOP_K):
            pltpu.make_async_copy(buf.at[s], xs_hbm.at[pl.ds(0, R)], sem.at[s]).wait()

    def pad_copies(fn):
        def per_expert(e, c):
            ln = padlen_ref[e]
            end = padend_ref[e]
            for bit in PAD_BITS:
                rows_per_copy = bit if bit >= V7X_SUBLANES else 1
                for j in range(bit // rows_per_copy):
                    @pl.when((ln & bit) != 0)
                    def _(bit=bit, end=end, j=j, n=rows_per_copy):
                        start = end - bit + j * n
                        if n > 1:
                            start = pl.multiple_of(start, V7X_SUBLANES)
                        fn(pltpu.make_async_copy(zbuf.at[pl.ds(0, n)], xs_hbm.at[pl.ds(start, n)], zsem))
                end = end - (ln & bit)
            return c
        lax.fori_loop(0, N_EXPERTS, per_expert, 0)

    @pl.when(i == 0)
    def _():
        zbuf[...] = jnp.zeros_like(zbuf)
        pad_copies(lambda cp: cp.start())

    @pl.when(i >= 2)
    def _():
        wait_slot(slot)

    buf[slot] = xm_ref[...]

    def issue(r, c):
        for k in range(TOP_K):
            d = dest_ref[(i * R + r) * TOP_K + k]
            pltpu.make_async_copy(buf.at[slot, pl.ds(r, 1)], xs_hbm.at[pl.ds(d, 1)], sem.at[slot]).start()
        return c

    lax.fori_loop(0, R, issue, 0, unroll=8)

    @pl.when(i == n_steps - 1)
    def _():
        if n_steps >= 2:
            wait_slot(1 - slot)
        wait_slot(slot)
        pad_copies(lambda cp: cp.wait())


def _moe_dispatch(xm, dest, pad_end, pad_len, P):
    T, Dh = xm.shape
    R = GATHER_ROWS
    est = 2 * R * Dh * 4 + 2 * R * Dh * 4 + PAD_BITS[0] * Dh * 4
    return pl.pallas_call(
        _dispatch_body,
        grid_spec=pltpu.PrefetchScalarGridSpec(
            num_scalar_prefetch=3,
            grid=(T // R,),
            in_specs=[pl.BlockSpec((R, Dh), lambda i, dest, pend, plen: (i, 0))],
            out_specs=pl.BlockSpec(memory_space=pl.ANY),
            scratch_shapes=[pltpu.VMEM((2, R, Dh), u32), pltpu.VMEM((PAD_BITS[0], Dh), u32),
                            pltpu.SemaphoreType.DMA((2,)), pltpu.SemaphoreType.DMA],
        ),
        out_shape=jax.ShapeDtypeStruct((P, Dh), u32),
        compiler_params=_params(("arbitrary",), est),
        name="moe_dispatch",
    )(dest, pad_end, pad_len, xm)


class _WeightStream:
    def __init__(self, w_hbms, wf_ref, wsem, col_tile):
        self.w_hbms, self.wf, self.wsem, self.ct = w_hbms, wf_ref, wsem, col_tile
        self.rows = wf_ref.shape[2]
        self.cr = self.rows // MOE_W_CHUNKS

    def _chunk(self, m, c, step, slot):
        e = step % N_EXPERTS
        t = step // N_EXPERTS
        r = pl.ds(pl.multiple_of(c * self.cr, self.cr), self.cr)
        cols = pl.ds(pl.multiple_of(t * self.ct, V7X_LANES), self.ct)
        return pltpu.make_async_copy(self.w_hbms[m].at[e, r, cols], self.wf.at[slot, m, r], self.wsem.at[slot])

    def start_chunk(self, c, step):
        for m in range(len(self.w_hbms)):
            self._chunk(m, c, step, step % 2).start()

    def start_chunks(self, c0, c1, step):
        lax.fori_loop(c0, c1, lambda c, carry: (self.start_chunk(c, step), carry)[1], 0)

    def wait(self, step):
        slot = step % 2
        for m in range(len(self.w_hbms)):
            for c in range(MOE_W_CHUNKS):
                self._chunk(m, c, step, slot).wait()

    def tile(self, m, step):
        return self.wf.at[step % 2, m]


def _expert_block_loop(row0, n_blocks, in_hbm, out_hbm, out_col0, ibuf, obuf, isem, osem, compute,
                       per_block):
    Bm = MOE_ROWS
    out_cols = obuf.shape[2]

    def rows(j):
        return pl.ds(pl.multiple_of(row0 + j * Bm, Bm), Bm)

    def in_copy(j, slot):
        return pltpu.make_async_copy(in_hbm.at[rows(j)], ibuf.at[slot], isem.at[slot])

    def out_copy(j, slot):
        return pltpu.make_async_copy(
            obuf.at[slot], out_hbm.at[rows(j), pl.ds(pl.multiple_of(out_col0, V7X_LANES), out_cols)],
            osem.at[slot])

    @pl.when(n_blocks > 0)
    def _():
        in_copy(0, 0).start()

    def block(j, carry):
        slot = j % 2

        @pl.when(j + 1 < n_blocks)
        def _():
            in_copy(j + 1, 1 - slot).start()

        per_block(j)
        in_copy(j, slot).wait()

        @pl.when(j >= 2)
        def _():
            out_copy(j - 2, slot).wait()

        obuf[slot] = compute(ibuf[slot])
        out_copy(j, slot).start()
        return carry

    lax.fori_loop(0, n_blocks, block, 0)

    @pl.when(n_blocks >= 2)
    def _():
        out_copy(n_blocks - 2, n_blocks % 2).wait()

    @pl.when(n_blocks >= 1)
    def _():
        out_copy(n_blocks - 1, (n_blocks - 1) % 2).wait()


def _expert_step(stream, n_blocks, cast, run_blocks):
    step = pl.program_id(0) * N_EXPERTS + pl.program_id(1)
    n_steps = pl.num_programs(0) * N_EXPERTS
    has_next = step + 1 < n_steps

    @pl.when(step == 0)
    def _():
        stream.start_chunks(0, MOE_W_CHUNKS, step)

    stream.wait(step)

    @pl.when(n_blocks > 0)
    def _():
        cast(step)

    per = (MOE_W_CHUNKS + jnp.maximum(n_blocks, 1) - 1) // jnp.maximum(n_blocks, 1)

    @pl.when(jnp.logical_and(has_next, n_blocks == 0))
    def _():
        stream.start_chunks(0, MOE_W_CHUNKS, step + 1)

    def per_block(j):
        @pl.when(has_next)
        def _():
            stream.start_chunks(jnp.minimum(j * per, MOE_W_CHUNKS), jnp.minimum((j + 1) * per, MOE_W_CHUNKS),
                                step + 1)

    run_blocks(per_block)


def _moe_hidden_body(blk0_ref, nblk_ref, w1_hbm, w3_hbm, xs_hbm, hid_hbm,
                     wf_ref, w1b_ref, w3b_ref, ibuf, obuf, wsem, isem, osem):
    f = pl.program_id(0)
    e = pl.program_id(1)
    n_blocks = nblk_ref[e]
    half = ibuf.shape[2]
    stream = _WeightStream((w1_hbm, w3_hbm), wf_ref, wsem, MOE_F_TILE)

    def cast(step):
        _cast_tile(stream.tile(0, step), w1b_ref)
        _cast_tile(stream.tile(1, step), w3b_ref)

    def compute(xw):
        x_lo, x_hi = _unpack_bf16_pair(xw)

        def proj(wb_ref):
            return (jnp.dot(x_lo, wb_ref[:half, :], preferred_element_type=f32)
                    + jnp.dot(x_hi, wb_ref[half:, :], preferred_element_type=f32))

        return (jax.nn.silu(proj(w1b_ref)) * proj(w3b_ref)).astype(obuf.dtype)

    _expert_step(stream, n_blocks, cast, functools.partial(
        _expert_block_loop, blk0_ref[e] * MOE_ROWS, n_blocks, xs_hbm, hid_hbm, f * MOE_F_TILE,
        ibuf, obuf, isem, osem, compute))


def _moe_hidden(xs, w1, w3, blk0, nblk):
    P, Dh = xs.shape
    D = w1.shape[1]
    Bm, Ft = MOE_ROWS, MOE_F_TILE
    est = 2 * 2 * D * Ft * 4 + 2 * D * Ft * 2 + 2 * Bm * Dh * 4 + 2 * Bm * Ft * 2 + Bm * D * 2 + 3 * Bm * Ft * 4
    return pl.pallas_call(
        _moe_hidden_body,
        grid_spec=pltpu.PrefetchScalarGridSpec(
            num_scalar_prefetch=2,
            grid=(D_EXPERT // Ft, N_EXPERTS),
            in_specs=[pl.BlockSpec(memory_space=pl.ANY)] * 3,
            out_specs=pl.BlockSpec(memory_space=pl.ANY),
            scratch_shapes=[pltpu.VMEM((2, 2, D, Ft), f32),
                            pltpu.VMEM((D, Ft), bf16), pltpu.VMEM((D, Ft), bf16),
                            pltpu.VMEM((2, Bm, Dh), u32), pltpu.VMEM((2, Bm, Ft), bf16),
                            pltpu.SemaphoreType.DMA((2,)),
                            pltpu.SemaphoreType.DMA((2,)), pltpu.SemaphoreType.DMA((2,))],
        ),
        out_shape=jax.ShapeDtypeStruct((P, D_EXPERT), bf16),
        compiler_params=_params(("arbitrary", "arbitrary"), est),
        name="moe_expert_hidden",
    )(blk0, nblk, w1, w3, xs)


def _moe_out_body(blk0_ref, nblk_ref, w2_hbm, hid_hbm, y_hbm, wf_ref, w2b_ref, ibuf, obuf,
                  wsem, isem, osem):
    n = pl.program_id(0)
    e = pl.program_id(1)
    n_blocks = nblk_ref[e]
    stream = _WeightStream((w2_hbm,), wf_ref, wsem, MOE_N_TILE)

    def cast(step):
        _cast_tile(stream.tile(0, step), w2b_ref)

    def compute(h):
        return jnp.dot(h, w2b_ref[...], preferred_element_type=f32)

    _expert_step(stream, n_blocks, cast, functools.partial(
        _expert_block_loop, blk0_ref[e] * MOE_ROWS, n_blocks, hid_hbm, y_hbm, n * MOE_N_TILE,
        ibuf, obuf, isem, osem, compute))


def _moe_out(hid, w2, blk0, nblk):
    P, F = hid.shape
    D = w2.shape[2]
    Bm, Nt = MOE_ROWS, MOE_N_TILE
    est = 2 * F * Nt * 4 + F * Nt * 2 + 2 * Bm * F * 2 + 2 * Bm * Nt * 4 + Bm * Nt * 4
    return pl.pallas_call(
        _moe_out_body,
        grid_spec=pltpu.PrefetchScalarGridSpec(
            num_scalar_prefetch=2,
            grid=(D // Nt, N_EXPERTS),
            in_specs=[pl.BlockSpec(memory_space=pl.ANY)] * 2,
            out_specs=pl.BlockSpec(memory_space=pl.ANY),
            scratch_shapes=[pltpu.VMEM((2, 1, F, Nt), f32), pltpu.VMEM((F, Nt), bf16),
                            pltpu.VMEM((2, Bm, F), bf16), pltpu.VMEM((2, Bm, Nt), f32),
                            pltpu.SemaphoreType.DMA((2,)),
                            pltpu.SemaphoreType.DMA((2,)), pltpu.SemaphoreType.DMA((2,))],
        ),
        out_shape=jax.ShapeDtypeStruct((P, D), f32),
        compiler_params=_params(("arbitrary", "arbitrary"), est),
        name="moe_expert_out",
    )(blk0, nblk, w2, hid)


def _combine_body(dest_ref, h_ref, gate_ref, g_ref, y_hbm, o_ref, buf, ss_s, sem):
    R = GATHER_ROWS
    i = pl.program_id(0)
    n_steps = pl.num_programs(0)

    def start(step, slot):
        def issue(r, c):
            for k in range(TOP_K):
                _row_copy(y_hbm, dest_ref[(step * R + r) * TOP_K + k],
                          buf.at[slot, k, pl.ds(r, 1)], sem.at[slot]).start()
            return c
        lax.fori_loop(0, R, issue, 0, unroll=8)

    @pl.when(i == 0)
    def _():
        start(0, 0)

    @pl.when(i + 1 < n_steps)
    def _():
        start(i + 1, (i + 1) % 2)

    slot = i % 2
    for k in range(TOP_K):
        pltpu.make_async_copy(y_hbm.at[pl.ds(0, R)], buf.at[slot, k], sem.at[slot]).wait()
    g_fin = g_ref[...]
    D = h_ref.shape[1]
    RC = V7X_SUBLANES

    def chunk(c):
        return pl.ds(pl.multiple_of(c * RC, RC), RC)

    def accumulate(c, carry):
        rs = chunk(c)
        gate = gate_ref[rs, :]
        h = (h_ref[rs, :] + buf[slot, 0, rs, :] * gate[:, ROUTE_GATE_LANE:ROUTE_GATE_LANE + 1]
             + buf[slot, 1, rs, :] * gate[:, ROUTE_GATE_LANE + 1:ROUTE_GATE_LANE + 2])
        o_ref[rs, :] = h
        sq = h * h
        ss = sq[:, 0:V7X_LANES]
        for j in range(1, D // V7X_LANES):
            ss = ss + sq[:, j * V7X_LANES:(j + 1) * V7X_LANES]
        ss_s[rs, :] = ss
        return carry

    lax.fori_loop(0, R // RC, accumulate, 0, unroll=4)

    def normalize(c, carry):
        rs = chunk(c)
        ms = jnp.sum(ss_s[rs, :], axis=-1, keepdims=True) * (1.0 / D)
        o_ref[rs, :] = o_ref[rs, :] * lax.rsqrt(ms + EPS) * g_fin
        return carry

    lax.fori_loop(0, R // RC, normalize, 0, unroll=4)


def _moe_combine(h, y, dest, gate, n_final):
    T, D = h.shape
    R = GATHER_ROWS
    est = 2 * TOP_K * R * D * 4 + 4 * R * D * 4 + 3 * R * D * 4
    return pl.pallas_call(
        _combine_body,
        grid_spec=pltpu.PrefetchScalarGridSpec(
            num_scalar_prefetch=1,
            grid=(T // R,),
            in_specs=[pl.BlockSpec((R, D), lambda i, dest: (i, 0)),
                      pl.BlockSpec((R, V7X_LANES), lambda i, dest: (i, 0)),
                      pl.BlockSpec((1, D), lambda i, dest: (0, 0)),
                      pl.BlockSpec(memory_space=pl.ANY)],
            out_specs=pl.BlockSpec((R, D), lambda i, dest: (i, 0)),
            scratch_shapes=[pltpu.VMEM((2, TOP_K, R, D), f32), pltpu.VMEM((R, V7X_LANES), f32),
                            pltpu.SemaphoreType.DMA((2,))],
        ),
        out_shape=jax.ShapeDtypeStruct((T, D), f32),
        compiler_params=_params(("arbitrary",), est),
        name="moe_combine_norm",
    )(dest, h, gate, n_final.reshape(1, D).astype(f32), y)


def _slot_layout(rec):
    T = rec.shape[0]
    A = T * TOP_K
    eid = rec[:, ROUTE_EID_LANE:ROUTE_EID_LANE + TOP_K].astype(jnp.int32).reshape(A)
    onehot = (eid[:, None] == jnp.arange(N_EXPERTS, dtype=jnp.int32)[None, :]).astype(jnp.int32)
    csum = jnp.cumsum(onehot, axis=0)
    rank = jnp.sum(onehot * csum, axis=1) - 1
    counts = csum[-1]
    pcounts = (counts + MOE_ROWS - 1) // MOE_ROWS * MOE_ROWS
    pend = jnp.cumsum(pcounts)
    pstart = pend - pcounts
    dest = (pstart[eid] + rank).astype(jnp.int32)
    n_blocks = -(-(A + N_EXPERTS * (MOE_ROWS - 1)) // MOE_ROWS)
    P = n_blocks * MOE_ROWS
    blk0 = (pstart // MOE_ROWS).astype(jnp.int32)
    nblk = (pcounts // MOE_ROWS).astype(jnp.int32)
    pad_end = pend.astype(jnp.int32)
    pad_len = (pcounts - counts).astype(jnp.int32)
    return dest, blk0, nblk, pad_end, pad_len, P


def kernel(x, mem, n_mix, w_in, gmlp_ln_g, gmlp_ln_b, gmlp_w_s, gmlp_b_s, hgrn_lower_bounds,
           hgrn_norm_g, w_out, n_cross, n_mem, w_q_x, w_k_x, w_v_x, w_o_x, n_moe, w_group,
           b_group, w_router, b_router, w1_e, w3_e, w2_e, n_final):
    B, S, D = x.shape
    n_mem_tok = mem.shape[1]
    T = B * S
    xt = x.reshape(T, D)

    hn = _rmsnorm(xt, n_mix[0], bf16)
    z = _matmul_stream([hn], w_in[0], bf16, 1024, 1024, "mm_w_in")
    ya = _gmlp(z, gmlp_ln_g[0], gmlp_ln_b[0], gmlp_w_s[0].astype(f32), gmlp_b_s[0].astype(f32))
    yb = _hgrn(z, hgrn_lower_bounds, hgrn_norm_g[0], B, S)
    h1 = _matmul_stream([ya, yb], w_out[0], f32, 512, 1024, "mm_w_out", res=xt)

    hc = _rmsnorm(h1, n_cross[0], bf16)
    mn = _rmsnorm(mem.reshape(B * n_mem_tok, D), n_mem[0], bf16)
    kx = _matmul(mn, w_k_x[0], bf16, B * n_mem_tok, 512, "mm_w_k")
    vx = _matmul(mn, w_v_x[0], bf16, B * n_mem_tok, 512, "mm_w_v")
    ox = _xattn(hc, w_q_x[0].astype(bf16), kx, vx, B, S, n_mem_tok)
    h2 = _matmul_stream([ox], w_o_x[0], f32, 512, 1024, "mm_w_o", res=h1)

    xm, rec = _router(h2, n_moe[0], w_group[0], b_group[0], w_router[0], b_router[0])
    dest, blk0, nblk, pad_end, pad_len, P = _slot_layout(rec)
    xs = _moe_dispatch(xm, dest, pad_end, pad_len, P)
    hid = _moe_hidden(xs, w1_e[0], w3_e[0], blk0, nblk)
    y = _moe_out(hid, w2_e[0], blk0, nblk)
    out = _moe_combine(h2, y, dest, rec, n_final)
    return out.reshape(B, S, D)
```

```python
import functools

import jax
import jax.numpy as jnp
from jax import lax
from jax.experimental import pallas as pl
from jax.experimental.pallas import tpu as pltpu

f32 = jnp.float32
bf16 = jnp.bfloat16

EPS = 1e-6
D_MODEL = 4096
CHUNK = 64
GMLP_HEADS = 4
GMLP_HEAD_DIM = 512
GMLP_BLOCK = 128
D_GMLP = GMLP_HEADS * GMLP_HEAD_DIM
HGRN_HEAD_DIM = 128
HGRN_HEADS = 16
D_HGRN = HGRN_HEADS * HGRN_HEAD_DIM
X_HEADS = 4
X_HEAD_DIM = 1024
N_GROUPS = 4
EXPERTS_PER_GROUP = 8
N_EXPERTS = N_GROUPS * EXPERTS_PER_GROUP
TOP_K = 2
D_EXPERT = 1024

V7X_LANES = 128
V7X_SUBLANES = 8
V7X_VMEM_LIMIT_BYTES = 60000 * 1024

MOE_ROWS = 256
MOE_F_TILE = 512
MOE_N_TILE = 4096
MOE_W_CHUNKS = 8
GATHER_ROWS = 256
HGRN_GROUP = 4
HGRN_ROWS = 512


def _params(sem, est_bytes):
    limit = min(int(est_bytes) + (8 << 20), V7X_VMEM_LIMIT_BYTES)
    return pltpu.CompilerParams(dimension_semantics=sem, vmem_limit_bytes=limit)


def _rms_body(x_ref, g_ref, o_ref):
    x = x_ref[...].astype(f32)
    ms = jnp.mean(x * x, axis=-1, keepdims=True)
    o_ref[...] = (x * lax.rsqrt(ms + EPS) * g_ref[...]).astype(o_ref.dtype)


def _rmsnorm(x, g, out_dtype, tm=256):
    M, D = x.shape
    est = 2 * tm * D * (x.dtype.itemsize + jnp.dtype(out_dtype).itemsize) + 2 * tm * D * 4
    return pl.pallas_call(
        _rms_body,
        grid=(M // tm,),
        in_specs=[pl.BlockSpec((tm, D), lambda i: (i, 0)),
                  pl.BlockSpec((1, D), lambda i: (0, 0))],
        out_specs=pl.BlockSpec((tm, D), lambda i: (i, 0)),
        out_shape=jax.ShapeDtypeStruct((M, D), out_dtype),
        compiler_params=_params(("arbitrary",), est),
        name="rmsnorm",
    )(x, g.reshape(1, D).astype(f32))


CAST_ROWS = 512


def _cast_tile(src_ref, dst_ref, dst_row0=0):
    rows = min(CAST_ROWS, src_ref.shape[0])
    n = src_ref.shape[0] // rows

    def piece(p, c):
        r = pl.multiple_of(p * rows, rows)
        d = pl.multiple_of(dst_row0 + p * rows, rows)
        dst_ref[pl.ds(d, rows), :] = src_ref[pl.ds(r, rows), :].astype(bf16)
        return c

    lax.fori_loop(0, n, piece, 0)


def _mm_body(a_ref, w_ref, o_ref, wb_ref):
    @pl.when(pl.program_id(1) == 0)
    def _():
        _cast_tile(w_ref, wb_ref)

    o_ref[...] = jnp.dot(a_ref[...], wb_ref[...], preferred_element_type=f32).astype(o_ref.dtype)


def _matmul(a, w, out_dtype, tm, tn, name):
    M, K = a.shape
    N = w.shape[1]
    osz = jnp.dtype(out_dtype).itemsize
    est = 2 * (tm * K * 2 + K * tn * 4 + tm * tn * osz) + K * tn * 2 + tm * tn * 4
    return pl.pallas_call(
        _mm_body,
        grid=(N // tn, M // tm),
        in_specs=[pl.BlockSpec((tm, K), lambda j, i: (i, 0)),
                  pl.BlockSpec((K, tn), lambda j, i: (0, j))],
        out_specs=pl.BlockSpec((tm, tn), lambda j, i: (i, j)),
        out_shape=jax.ShapeDtypeStruct((M, N), out_dtype),
        scratch_shapes=[pltpu.VMEM((K, tn), bf16)],
        compiler_params=_params(("arbitrary", "arbitrary"), est),
        name=name,
    )(a, w)


def _mm_res_body(n_parts, with_norm_inputs, r_ref, *refs):
    a_refs = refs[:n_parts]
    w_refs = refs[n_parts:2 * n_parts]
    out_refs = refs[2 * n_parts:-1]
    wb_ref = refs[-1]
    kp = w_refs[0].shape[0]

    @pl.when(pl.program_id(1) == 0)
    def _():
        for p in range(n_parts):
            _cast_tile(w_refs[p], wb_ref, p * kp)

    acc = r_ref[...]
    for p in range(n_parts):
        acc = acc + jnp.dot(a_refs[p][...], wb_ref[p * kp:(p + 1) * kp, :], preferred_element_type=f32)
    out_refs[0][...] = acc
    if with_norm_inputs:
        out_refs[1][...] = acc.astype(bf16)
        sq = acc * acc
        ss = sq[:, 0:V7X_LANES]
        for g in range(1, acc.shape[1] // V7X_LANES):
            ss = ss + sq[:, g * V7X_LANES:(g + 1) * V7X_LANES]
        out_refs[2][...] = ss


def _matmul_res(res, a_list, w, tm, tn, name, with_norm_inputs=False):
    M, N = res.shape
    n = len(a_list)
    Kp = a_list[0].shape[1]
    a_specs = [pl.BlockSpec((tm, Kp), lambda j, i: (i, 0)) for _ in range(n)]
    w_specs = [pl.BlockSpec((Kp, tn), functools.partial(lambda j, i, p: (p, j), p=p)) for p in range(n)]
    est = 2 * (n * tm * Kp * 2 + n * Kp * tn * 4 + 2 * tm * tn * 4) + n * Kp * tn * 2 + 2 * tm * tn * 4
    tile = pl.BlockSpec((tm, tn), lambda j, i: (i, j))
    out_specs, out_shape = [tile], [jax.ShapeDtypeStruct((M, N), f32)]
    if with_norm_inputs:
        est += 2 * tm * tn * 2 + 2 * tm * V7X_LANES * 4 + tm * tn * 4
        out_specs += [tile, pl.BlockSpec((tm, V7X_LANES), lambda j, i: (i, j))]
        out_shape += [jax.ShapeDtypeStruct((M, N), bf16),
                      jax.ShapeDtypeStruct((M, N // tn * V7X_LANES), f32)]
    outs = pl.pallas_call(
        functools.partial(_mm_res_body, n, with_norm_inputs),
        grid=(N // tn, M // tm),
        in_specs=[tile] + a_specs + w_specs,
        out_specs=out_specs,
        out_shape=out_shape,
        scratch_shapes=[pltpu.VMEM((n * Kp, tn), bf16)],
        compiler_params=_params(("arbitrary", "arbitrary"), est),
        name=name,
    )(res, *a_list, *([w] * n))
    return outs if with_norm_inputs else outs[0]


def _gelu(x):
    return 0.5 * x * (1.0 + lax.erf(x * 0.7071067811865476))


def _gmlp_body(u_ref, v_ref, lng_ref, lnb_ref, ws_ref, bs_ref, o_ref):
    rows = u_ref.shape[0]
    gv = _gelu(v_ref[...].astype(f32))
    mu = jnp.mean(gv, axis=-1, keepdims=True)
    xc = gv - mu
    var = jnp.mean(xc * xc, axis=-1, keepdims=True)
    vn = (xc * lax.rsqrt(var + EPS) * lng_ref[...] + lnb_ref[...]).astype(bf16)
    ci = lax.broadcasted_iota(jnp.int32, (GMLP_BLOCK, GMLP_BLOCK), 0) // CHUNK
    cj = lax.broadcasted_iota(jnp.int32, (GMLP_BLOCK, GMLP_BLOCK), 1) // CHUNK
    w = jnp.where(cj <= ci, ws_ref[0], 0.0).astype(bf16)
    bias = bs_ref[0]
    for blk in range(rows // GMLP_BLOCK):
        sl = slice(blk * GMLP_BLOCK, (blk + 1) * GMLP_BLOCK)
        s = jnp.dot(w, vn[sl], preferred_element_type=f32) + bias
        o_ref[sl, :] = (_gelu(u_ref[sl, :].astype(f32)) * s).astype(o_ref.dtype)


def _gmlp(z, ln_g, ln_b, w_s, b_s, tr=1024):
    M = z.shape[0]
    C = GMLP_HEAD_DIM
    est = 2 * 3 * tr * C * 2 + 8 * tr * C * 4
    return pl.pallas_call(
        _gmlp_body,
        grid=(M // tr, GMLP_HEADS),
        in_specs=[pl.BlockSpec((tr, C), lambda i, h: (i, h)),
                  pl.BlockSpec((tr, C), lambda i, h: (i, GMLP_HEADS + h)),
                  pl.BlockSpec((1, C), lambda i, h: (0, h)),
                  pl.BlockSpec((1, C), lambda i, h: (0, h)),
                  pl.BlockSpec((1, GMLP_BLOCK, GMLP_BLOCK), lambda i, h: (h, 0, 0)),
                  pl.BlockSpec((1, GMLP_BLOCK, 1), lambda i, h: (h, 0, 0))],
        out_specs=pl.BlockSpec((tr, C), lambda i, h: (i, h)),
        out_shape=jax.ShapeDtypeStruct((M, D_GMLP), bf16),
        compiler_params=_params(("arbitrary", "arbitrary"), est),
        name="gmlp_mixer",
    )(z, z, ln_g.reshape(1, D_GMLP), ln_b.reshape(1, D_GMLP), w_s,
      b_s.reshape(GMLP_HEADS, GMLP_BLOCK, 1))


def _split3(x):
    hi = x.astype(bf16)
    r1 = x - hi.astype(f32)
    mid = r1.astype(bf16)
    lo = (r1 - mid.astype(f32)).astype(bf16)
    return hi, mid, lo


HGRN_STRIP = 2 * CHUNK
_NT = (((1,), (1,)), ((), ()))
_TN = (((0,), (0,)), ((), ()))


def _split2(x):
    hi = x.astype(bf16)
    mid = (x - hi.astype(f32)).astype(bf16)
    return hi, mid


def _hgrn_body(q_ref, f_ref, i_ref, g_ref, lbr_ref, ng_ref, o_ref,
               st_ref, qin_s, kin_s, qex_s, kdec_s, dec_s, am_s, xq_s, xk_s, inc_s, smat_s, oacc_s):
    C, K, G = CHUNK, HGRN_HEAD_DIM, HGRN_GROUP
    rows, W = q_ref.shape
    n_chunks = rows // C

    @pl.when(pl.program_id(2) == 0)
    def _():
        st_ref[...] = jnp.zeros_like(st_ref)

    r0 = lbr_ref[0:1, :]
    r1 = lbr_ref[1:2, :]
    m = jnp.maximum(r0, r1)
    e0 = jnp.exp(r0 - m)
    e1 = jnp.exp(r1 - m)
    lb = e0 / (e0 + e1)

    ti = lax.broadcasted_iota(jnp.int32, (C, C), 0)
    si = lax.broadcasted_iota(jnp.int32, (C, C), 1)
    tril = (si <= ti).astype(bf16)

    def chunk_rows(c):
        return pl.ds(pl.multiple_of(c * C, C), C)

    def gates(c, carry):
        rs = chunk_rows(c)
        fg = lb + (1.0 - lb) * jax.nn.sigmoid(f_ref[rs, :].astype(f32))
        hi, mid, lo = _split3(jnp.log(fg))
        b = (jnp.dot(tril, hi, preferred_element_type=f32)
             + jnp.dot(tril, mid, preferred_element_type=f32)
             + jnp.dot(tril, lo, preferred_element_type=f32))
        b_mid = b[C // 2:C // 2 + 1, :]
        b_last = b[C - 1:C, :]
        q_in = jax.nn.silu(q_ref[rs, :].astype(f32)) * jnp.exp(b - b_mid)
        k_in = (1.0 - fg) * jnp.exp(b_mid - b)
        qin_s[rs, :] = q_in.astype(bf16)
        kin_s[rs, :] = k_in.astype(bf16)
        qex_s[rs, :] = (q_in * jnp.exp(b_mid)).astype(bf16)
        kdec_s[rs, :] = (k_in * jnp.exp(b_last - b_mid)).astype(bf16)
        dec_s[pl.ds(pl.multiple_of(c * 8, 8), 8), :] = jnp.broadcast_to(jnp.exp(b_last), (8, W))
        return carry

    lax.fori_loop(0, n_chunks, gates, 0, unroll=2)

    @pl.when((pl.program_id(0) == 0) & (pl.program_id(1) == 0) & (pl.program_id(2) == 0))
    def _():
        am_s[...] = jnp.zeros_like(am_s)
        xq_s[...] = jnp.zeros_like(xq_s)
        xk_s[...] = jnp.zeros_like(xk_s)

    S = HGRN_STRIP
    tt = lax.broadcasted_iota(jnp.int32, (S, S), 0)
    ss = lax.broadcasted_iota(jnp.int32, (S, S), 1)
    keep = jnp.logical_and(ss <= tt, (ss // C) == (tt // C))

    for h in range(G):
        ls = slice(h * K, (h + 1) * K)
        for c in range(n_chunks):
            xq_s[h, c * C:(c + 1) * C, c * K:(c + 1) * K] = qex_s[c * C:(c + 1) * C, ls]
            xk_s[h, c * C:(c + 1) * C, c * K:(c + 1) * K] = kdec_s[c * C:(c + 1) * C, ls]

        a = lax.dot_general(qin_s[:, ls], kin_s[:, ls], _NT, preferred_element_type=f32)
        for r in range(rows // S):
            sl = slice(r * S, (r + 1) * S)
            am_s[sl, sl] = jnp.where(keep, a[sl, sl], 0.0).astype(bf16)
        oacc_s[:, ls] = jnp.dot(am_s[...], i_ref[:, ls], preferred_element_type=f32)

        inc_s[...] = lax.dot_general(i_ref[:, ls], xk_s[h], _TN, preferred_element_type=f32)

        st = st_ref[h]
        for c in range(n_chunks):
            cs = slice(c * K, (c + 1) * K)
            smat_s[:, cs] = st.astype(bf16)
            st = st * dec_s[c * 8:c * 8 + 1, ls] + inc_s[:, cs]
        st_ref[h] = st

        oacc_s[:, ls] += lax.dot_general(xq_s[h], smat_s[...], _NT, preferred_element_type=f32)

    ng = ng_ref[...]

    def outputs(c, carry):
        rs = chunk_rows(c)
        for h in range(G):
            ls = slice(h * K, (h + 1) * K)
            o = oacc_s[rs, ls]
            ms = jnp.mean(o * o, axis=-1, keepdims=True)
            o = o * lax.rsqrt(ms + EPS) * ng[:, ls]
            o = o * jax.nn.silu(g_ref[rs, ls].astype(f32))
            o_ref[rs, ls] = o.astype(o_ref.dtype)
        return carry

    lax.fori_loop(0, n_chunks, outputs, 0, unroll=2)


def _hgrn(z, lower_bounds, norm_g, batch, seq):
    M = z.shape[0]
    W = HGRN_GROUP * HGRN_HEAD_DIM
    ts = HGRN_ROWS
    n_s = seq // ts
    n_hg = D_HGRN // W
    col0 = 2 * D_GMLP // W

    def zspec(part):
        return pl.BlockSpec((ts, W), lambda b, hg, s, part=part: (b * n_s + s, col0 + part * n_hg + hg))

    G, K = HGRN_GROUP, HGRN_HEAD_DIM
    n_chunks = ts // CHUNK
    est = (2 * 5 * ts * W * 2 + 4 * ts * W * 2 + ts * W * 4 + ts * ts * 2 + 2 * G * ts * n_chunks * K * 2
           + K * n_chunks * K * 6 + G * K * K * 4 + 3 * ts * ts * 4)
    return pl.pallas_call(
        _hgrn_body,
        grid=(batch, n_hg, n_s),
        in_specs=[zspec(0), zspec(1), zspec(2), zspec(3),
                  pl.BlockSpec((2, W), lambda b, hg, s: (0, hg)),
                  pl.BlockSpec((1, W), lambda b, hg, s: (0, hg))],
        out_specs=pl.BlockSpec((ts, W), lambda b, hg, s: (b * n_s + s, hg)),
        out_shape=jax.ShapeDtypeStruct((M, D_HGRN), bf16),
        scratch_shapes=[pltpu.VMEM((G, K, K), f32),
                        pltpu.VMEM((ts, W), bf16),
                        pltpu.VMEM((ts, W), bf16),
                        pltpu.VMEM((ts, W), bf16),
                        pltpu.VMEM((ts, W), bf16),
                        pltpu.VMEM((n_chunks * 8, W), f32),
                        pltpu.VMEM((ts, ts), bf16),
                        pltpu.VMEM((G, ts, n_chunks * K), bf16),
                        pltpu.VMEM((G, ts, n_chunks * K), bf16),
                        pltpu.VMEM((K, n_chunks * K), f32),
                        pltpu.VMEM((K, n_chunks * K), bf16),
                        pltpu.VMEM((ts, W), f32)],
        compiler_params=_params(("arbitrary", "arbitrary", "arbitrary"), est),
        name="hgrn2_mixer",
    )(z, z, z, z, lower_bounds.astype(f32), norm_g.reshape(1, D_HGRN).astype(f32))


def _xattn_body(h_ref, ss_ref, wq_ref, k_ref, v_ref, o_ref):
    D = h_ref.shape[1]
    inv_rms = lax.rsqrt(jnp.sum(ss_ref[...], axis=-1, keepdims=True) * (1.0 / D) + EPS)
    q = (jnp.dot(h_ref[...], wq_ref[...], preferred_element_type=f32) * inv_rms).astype(bf16)
    s = lax.dot_general(q, k_ref[...], (((1,), (1,)), ((), ())), preferred_element_type=f32)
    s = s * (X_HEAD_DIM ** -0.5)
    s = s - jnp.max(s, axis=-1, keepdims=True)
    p = jnp.exp(s)
    p = p / jnp.sum(p, axis=-1, keepdims=True)
    o_ref[...] = jnp.dot(p.astype(bf16), v_ref[...], preferred_element_type=f32).astype(o_ref.dtype)


def _xattn(hb, ss, wq, k, v, batch, seq, n_mem, tq=1024):
    M, D = hb.shape
    n_s = seq // tq
    dh = X_HEAD_DIM
    est = (2 * (tq * D * 2 + tq * ss.shape[1] * 4 + D * dh * 2 + 2 * n_mem * dh * 2 + tq * dh * 2)
           + 3 * tq * dh * 4)
    return pl.pallas_call(
        _xattn_body,
        grid=(X_HEADS, batch, n_s),
        in_specs=[pl.BlockSpec((tq, D), lambda h, b, s: (b * n_s + s, 0)),
                  pl.BlockSpec((tq, ss.shape[1]), lambda h, b, s: (b * n_s + s, 0)),
                  pl.BlockSpec((D, dh), lambda h, b, s: (0, h)),
                  pl.BlockSpec((n_mem, dh), lambda h, b, s: (b, h)),
                  pl.BlockSpec((n_mem, dh), lambda h, b, s: (b, h))],
        out_specs=pl.BlockSpec((tq, dh), lambda h, b, s: (b * n_s + s, h)),
        out_shape=jax.ShapeDtypeStruct((M, D), bf16),
        compiler_params=_params(("arbitrary", "arbitrary", "arbitrary"), est),
        name="xattn_qproj",
    )(hb, ss, wq, k, v)


u32 = jnp.uint32
ROUTE_GATE_LANE = 0
ROUTE_EID_LANE = 2


def _pack_bf16_pair(lo, hi):
    lo32 = lax.bitcast_convert_type(lo.astype(f32), u32) >> 16
    hi32 = lax.bitcast_convert_type(hi.astype(f32), u32) & jnp.uint32(0xFFFF0000)
    return lo32 | hi32


def _unpack_bf16_pair(w):
    lo = lax.bitcast_convert_type(w << 16, f32).astype(bf16)
    hi = lax.bitcast_convert_type(w & jnp.uint32(0xFFFF0000), f32).astype(bf16)
    return lo, hi


def _first_lane_of(mask, lane):
    return jnp.min(jnp.where(mask, lane, V7X_LANES), axis=-1, keepdims=True)


def _router_body(h_ref, g_ref, wr_ref, br_ref, xm_ref, r_ref):
    x = h_ref[...]
    D = x.shape[1]
    ms = jnp.mean(x * x, axis=-1, keepdims=True)
    xn = x * lax.rsqrt(ms + EPS) * g_ref[...]
    xh, xm = _split2(xn)
    wh, wm = _split2(wr_ref[...])
    dot = functools.partial(jnp.dot, preferred_element_type=f32)
    acc = dot(xm, wh) + dot(xh, wm)
    acc = acc + dot(xh, wh)
    logits = acc + br_ref[...]

    xm_ref[...] = _pack_bf16_pair(xh[:, :D // 2], xh[:, D // 2:])

    lane = lax.broadcasted_iota(jnp.int32, logits.shape, 1)
    neg = jnp.float32(-jnp.inf)
    is_g = lane < N_GROUPS
    gl = jnp.where(is_g, logits, neg)
    gmax = jnp.max(gl, axis=-1, keepdims=True)
    grp = _first_lane_of(gl == gmax, lane)
    p_grp = 1.0 / jnp.sum(jnp.where(is_g, jnp.exp(logits - gmax), 0.0), axis=-1, keepdims=True)
    e0 = N_GROUPS + grp * EXPERTS_PER_GROUP
    el = jnp.where(jnp.logical_and(lane >= e0, lane < e0 + EXPERTS_PER_GROUP), logits, neg)
    m1 = jnp.max(el, axis=-1, keepdims=True)
    i1 = _first_lane_of(el == m1, lane)
    el2 = jnp.where(lane == i1, neg, el)
    m2 = jnp.max(el2, axis=-1, keepdims=True)
    i2 = _first_lane_of(el2 == m2, lane)
    t = jnp.exp(m2 - m1)
    g1 = p_grp * (1.0 / (1.0 + t))
    g2 = p_grp * (t / (1.0 + t))
    rec = jnp.where(lane == ROUTE_GATE_LANE, g1, 0.0)
    rec = jnp.where(lane == ROUTE_GATE_LANE + 1, g2, rec)
    rec = jnp.where(lane == ROUTE_EID_LANE, (i1 - N_GROUPS).astype(f32), rec)
    rec = jnp.where(lane == ROUTE_EID_LANE + 1, (i2 - N_GROUPS).astype(f32), rec)
    r_ref[...] = rec


def _router(h, n_moe, w_group, b_group, w_router, b_router, tm=256):
    M, D = h.shape
    n_log = N_GROUPS + N_EXPERTS
    wr = jnp.concatenate([w_group] + [w_router[g] for g in range(N_GROUPS)], axis=1)
    wr = jnp.pad(wr.astype(f32), ((0, 0), (0, V7X_LANES - n_log)))
    br = jnp.pad(jnp.concatenate([b_group, b_router.reshape(-1)]).astype(f32), (0, V7X_LANES - n_log))
    est = 2 * tm * D * 4 + 2 * D * V7X_LANES * 4 + 2 * tm * D * 2 + 6 * tm * D * 4
    return pl.pallas_call(
        _router_body,
        grid=(M // tm,),
        in_specs=[pl.BlockSpec((tm, D), lambda i: (i, 0)),
                  pl.BlockSpec((1, D), lambda i: (0, 0)),
                  pl.BlockSpec((D, V7X_LANES), lambda i: (0, 0)),
                  pl.BlockSpec((1, V7X_LANES), lambda i: (0, 0))],
        out_specs=[pl.BlockSpec((tm, D // 2), lambda i: (i, 0)),
                   pl.BlockSpec((tm, V7X_LANES), lambda i: (i, 0))],
        out_shape=[jax.ShapeDtypeStruct((M, D // 2), u32),
                   jax.ShapeDtypeStruct((M, V7X_LANES), f32)],
        compiler_params=_params(("arbitrary",), est),
        name="moe_router",
    )(h, n_moe.reshape(1, D).astype(f32), wr, br.reshape(1, V7X_LANES))


def _row_copy(src_hbm, row, dst, sem):
    return pltpu.make_async_copy(src_hbm.at[pl.ds(row, 1)], dst, sem)


PAD_BITS = tuple(1 << k for k in reversed(range((MOE_ROWS - 1).bit_length())))


def _dispatch_body(dest_ref, padend_ref, padlen_ref, xm_ref, xs_hbm, buf, zbuf, sem, zsem):
    R = GATHER_ROWS
    i = pl.program_id(0)
    n_steps = pl.num_programs(0)
    slot = i % 2

    def wait_slot(s):
        for _ in range(TOP_K):
            pltpu.make_async_copy(buf.at[s], xs_hbm.at[pl.ds(0, R)], sem.at[s]).wait()

    def pad_copies(fn):
        def per_expert(e, c):
            ln = padlen_ref[e]
            end = padend_ref[e]
            for bit in PAD_BITS:
                rows_per_copy = bit if bit >= V7X_SUBLANES else 1
                for j in range(bit // rows_per_copy):
                    @pl.when((ln & bit) != 0)
                    def _(bit=bit, end=end, j=j, n=rows_per_copy):
                        start = end - bit + j * n
                        if n > 1:
                            start = pl.multiple_of(start, V7X_SUBLANES)
                        fn(pltpu.make_async_copy(zbuf.at[pl.ds(0, n)], xs_hbm.at[pl.ds(start, n)], zsem))
                end = end - (ln & bit)
            return c
        lax.fori_loop(0, N_EXPERTS, per_expert, 0)

    @pl.when(i == 0)
    def _():
        zbuf[...] = jnp.zeros_like(zbuf)
        pad_copies(lambda cp: cp.start())

    @pl.when(i >= 2)
    def _():
        wait_slot(slot)

    buf[slot] = xm_ref[...]

    def issue(r, c):
        for k in range(TOP_K):
            d = dest_ref[(i * R + r) * TOP_K + k]
            pltpu.make_async_copy(buf.at[slot, pl.ds(r, 1)], xs_hbm.at[pl.ds(d, 1)], sem.at[slot]).start()
        return c

    lax.fori_loop(0, R, issue, 0, unroll=8)

    @pl.when(i == n_steps - 1)
    def _():
        if n_steps >= 2:
            wait_slot(1 - slot)
        wait_slot(slot)
        pad_copies(lambda cp: cp.wait())


def _moe_dispatch(xm, dest, pad_end, pad_len, P):
    T, Dh = xm.shape
    R = GATHER_ROWS
    est = 2 * R * Dh * 4 + 2 * R * Dh * 4 + PAD_BITS[0] * Dh * 4
    return pl.pallas_call(
        _dispatch_body,
        grid_spec=pltpu.PrefetchScalarGridSpec(
            num_scalar_prefetch=3,
            grid=(T // R,),
            in_specs=[pl.BlockSpec((R, Dh), lambda i, dest, pend, plen: (i, 0))],
            out_specs=pl.BlockSpec(memory_space=pl.ANY),
            scratch_shapes=[pltpu.VMEM((2, R, Dh), u32), pltpu.VMEM((PAD_BITS[0], Dh), u32),
                            pltpu.SemaphoreType.DMA((2,)), pltpu.SemaphoreType.DMA],
        ),
        out_shape=jax.ShapeDtypeStruct((P, Dh), u32),
        compiler_params=_params(("arbitrary",), est),
        name="moe_dispatch",
    )(dest, pad_end, pad_len, xm)


class _WeightStream:
    def __init__(self, w_hbms, wf_ref, wsem, col_tile):
        self.w_hbms, self.wf, self.wsem, self.ct = w_hbms, wf_ref, wsem, col_tile
        self.rows = wf_ref.shape[2]
        self.cr = self.rows // MOE_W_CHUNKS

    def _chunk(self, m, c, step, slot):
        e = step % N_EXPERTS
        t = step // N_EXPERTS
        r = pl.ds(pl.multiple_of(c * self.cr, self.cr), self.cr)
        cols = pl.ds(pl.multiple_of(t * self.ct, V7X_LANES), self.ct)
        return pltpu.make_async_copy(self.w_hbms[m].at[e, r, cols], self.wf.at[slot, m, r], self.wsem.at[slot])

    def start_chunk(self, c, step):
        for m in range(len(self.w_hbms)):
            self._chunk(m, c, step, step % 2).start()

    def start_chunks(self, c0, c1, step):
        lax.fori_loop(c0, c1, lambda c, carry: (self.start_chunk(c, step), carry)[1], 0)

    def wait(self, step):
        slot = step % 2
        for m in range(len(self.w_hbms)):
            for c in range(MOE_W_CHUNKS):
                self._chunk(m, c, step, slot).wait()

    def tile(self, m, step):
        return self.wf.at[step % 2, m]


def _expert_block_loop(row0, n_blocks, in_hbm, out_hbm, out_col0, ibuf, obuf, isem, osem, compute,
                       per_block):
    Bm = MOE_ROWS
    out_cols = obuf.shape[2]

    def rows(j):
        return pl.ds(pl.multiple_of(row0 + j * Bm, Bm), Bm)

    def in_copy(j, slot):
        return pltpu.make_async_copy(in_hbm.at[rows(j)], ibuf.at[slot], isem.at[slot])

    def out_copy(j, slot):
        return pltpu.make_async_copy(
            obuf.at[slot], out_hbm.at[rows(j), pl.ds(pl.multiple_of(out_col0, V7X_LANES), out_cols)],
            osem.at[slot])

    @pl.when(n_blocks > 0)
    def _():
        in_copy(0, 0).start()

    def block(j, carry):
        slot = j % 2

        @pl.when(j + 1 < n_blocks)
        def _():
            in_copy(j + 1, 1 - slot).start()

        per_block(j)
        in_copy(j, slot).wait()

        @pl.when(j >= 2)
        def _():
            out_copy(j - 2, slot).wait()

        obuf[slot] = compute(ibuf[slot])
        out_copy(j, slot).start()
        return carry

    lax.fori_loop(0, n_blocks, block, 0)

    @pl.when(n_blocks >= 2)
    def _():
        out_copy(n_blocks - 2, n_blocks % 2).wait()

    @pl.when(n_blocks >= 1)
    def _():
        out_copy(n_blocks - 1, (n_blocks - 1) % 2).wait()


def _expert_step(stream, n_blocks, cast, run_blocks):
    step = pl.program_id(0) * N_EXPERTS + pl.program_id(1)
    n_steps = pl.num_programs(0) * N_EXPERTS
    has_next = step + 1 < n_steps

    @pl.when(step == 0)
    def _():
        stream.start_chunks(0, MOE_W_CHUNKS, step)

    stream.wait(step)

    @pl.when(n_blocks > 0)
    def _():
        cast(step)

    per = (MOE_W_CHUNKS + jnp.maximum(n_blocks, 1) - 1) // jnp.maximum(n_blocks, 1)

    @pl.when(jnp.logical_and(has_next, n_blocks == 0))
    def _():
        stream.start_chunks(0, MOE_W_CHUNKS, step + 1)

    def per_block(j):
        @pl.when(has_next)
        def _():
            stream.start_chunks(jnp.minimum(j * per, MOE_W_CHUNKS), jnp.minimum((j + 1) * per, MOE_W_CHUNKS),
                                step + 1)

    run_blocks(per_block)


def _moe_hidden_body(blk0_ref, nblk_ref, w1_hbm, w3_hbm, xs_hbm, hid_hbm,
                     wf_ref, w1b_ref, w3b_ref, ibuf, obuf, wsem, isem, osem):
    f = pl.program_id(0)
    e = pl.program_id(1)
    n_blocks = nblk_ref[e]
    half = ibuf.shape[2]
    stream = _WeightStream((w1_hbm, w3_hbm), wf_ref, wsem, MOE_F_TILE)

    def cast(step):
        _cast_tile(stream.tile(0, step), w1b_ref)
        _cast_tile(stream.tile(1, step), w3b_ref)

    def compute(xw):
        x_lo, x_hi = _unpack_bf16_pair(xw)

        def proj(wb_ref):
            return (jnp.dot(x_lo, wb_ref[:half, :], preferred_element_type=f32)
                    + jnp.dot(x_hi, wb_ref[half:, :], preferred_element_type=f32))

        return (jax.nn.silu(proj(w1b_ref)) * proj(w3b_ref)).astype(obuf.dtype)

    _expert_step(stream, n_blocks, cast, functools.partial(
        _expert_block_loop, blk0_ref[e] * MOE_ROWS, n_blocks, xs_hbm, hid_hbm, f * MOE_F_TILE,
        ibuf, obuf, isem, osem, compute))


def _moe_hidden(xs, w1, w3, blk0, nblk):
    P, Dh = xs.shape
    D = w1.shape[1]
    Bm, Ft = MOE_ROWS, MOE_F_TILE
    est = 2 * 2 * D * Ft * 4 + 2 * D * Ft * 2 + 2 * Bm * Dh * 4 + 2 * Bm * Ft * 2 + Bm * D * 2 + 3 * Bm * Ft * 4
    return pl.pallas_call(
        _moe_hidden_body,
        grid_spec=pltpu.PrefetchScalarGridSpec(
            num_scalar_prefetch=2,
            grid=(D_EXPERT // Ft, N_EXPERTS),
            in_specs=[pl.BlockSpec(memory_space=pl.ANY)] * 3,
            out_specs=pl.BlockSpec(memory_space=pl.ANY),
            scratch_shapes=[pltpu.VMEM((2, 2, D, Ft), f32),
                            pltpu.VMEM((D, Ft), bf16), pltpu.VMEM((D, Ft), bf16),
                            pltpu.VMEM((2, Bm, Dh), u32), pltpu.VMEM((2, Bm, Ft), bf16),
                            pltpu.SemaphoreType.DMA((2,)),
                            pltpu.SemaphoreType.DMA((2,)), pltpu.SemaphoreType.DMA((2,))],
        ),
        out_shape=jax.ShapeDtypeStruct((P, D_EXPERT), bf16),
        compiler_params=_params(("arbitrary", "arbitrary"), est),
        name="moe_expert_hidden",
    )(blk0, nblk, w1, w3, xs)


def _moe_out_body(blk0_ref, nblk_ref, w2_hbm, hid_hbm, y_hbm, wf_ref, w2b_ref, ibuf, obuf,
                  wsem, isem, osem):
    n = pl.program_id(0)
    e = pl.program_id(1)
    n_blocks = nblk_ref[e]
    stream = _WeightStream((w2_hbm,), wf_ref, wsem, MOE_N_TILE)

    def cast(step):
        _cast_tile(stream.tile(0, step), w2b_ref)

    def compute(h):
        return jnp.dot(h, w2b_ref[...], preferred_element_type=f32)

    _expert_step(stream, n_blocks, cast, functools.partial(
        _expert_block_loop, blk0_ref[e] * MOE_ROWS, n_blocks, hid_hbm, y_hbm, n * MOE_N_TILE,
        ibuf, obuf, isem, osem, compute))


def _moe_out(hid, w2, blk0, nblk):
    P, F = hid.shape
    D = w2.shape[2]
    Bm, Nt = MOE_ROWS, MOE_N_TILE
    est = 2 * F * Nt * 4 + F * Nt * 2 + 2 * Bm * F * 2 + 2 * Bm * Nt * 4 + Bm * Nt * 4
    return pl.pallas_call(
        _moe_out_body,
        grid_spec=pltpu.PrefetchScalarGridSpec(
            num_scalar_prefetch=2,
            grid=(D // Nt, N_EXPERTS),
            in_specs=[pl.BlockSpec(memory_space=pl.ANY)] * 2,
            out_specs=pl.BlockSpec(memory_space=pl.ANY),
            scratch_shapes=[pltpu.VMEM((2, 1, F, Nt), f32), pltpu.VMEM((F, Nt), bf16),
                            pltpu.VMEM((2, Bm, F), bf16), pltpu.VMEM((2, Bm, Nt), f32),
                            pltpu.SemaphoreType.DMA((2,)),
                            pltpu.SemaphoreType.DMA((2,)), pltpu.SemaphoreType.DMA((2,))],
        ),
        out_shape=jax.ShapeDtypeStruct((P, D), f32),
        compiler_params=_params(("arbitrary", "arbitrary"), est),
        name="moe_expert_out",
    )(blk0, nblk, w2, hid)


def _combine_body(dest_ref, h_ref, gate_ref, g_ref, y_hbm, o_ref, buf, ss_s, sem):
    R = GATHER_ROWS
    i = pl.program_id(0)
    n_steps = pl.num_programs(0)

    def issue_rows(step, slot, r0, n):
        for r in range(n):
            for k in range(TOP_K):
                _row_copy(y_hbm, dest_ref[(step * R + r0 + r) * TOP_K + k],
                          buf.at[slot, k, pl.ds(r0 + r, 1)], sem.at[slot]).start()

    @pl.when(i == 0)
    def _():
        lax.fori_loop(0, R // V7X_SUBLANES,
                      lambda c, carry: (issue_rows(0, 0, c * V7X_SUBLANES, V7X_SUBLANES), carry)[1], 0)

    slot = i % 2
    for k in range(TOP_K):
        pltpu.make_async_copy(y_hbm.at[pl.ds(0, R)], buf.at[slot, k], sem.at[slot]).wait()
    g_fin = g_ref[...]
    D = h_ref.shape[1]
    RC = V7X_SUBLANES

    def chunk(c):
        return pl.ds(pl.multiple_of(c * RC, RC), RC)

    def passes(prefetch):
        def accumulate(c, carry):
            rs = chunk(c)
            gate = gate_ref[rs, :]
            h = (h_ref[rs, :] + buf[slot, 0, rs, :] * gate[:, ROUTE_GATE_LANE:ROUTE_GATE_LANE + 1]
                 + buf[slot, 1, rs, :] * gate[:, ROUTE_GATE_LANE + 1:ROUTE_GATE_LANE + 2])
            o_ref[rs, :] = h
            sq = h * h
            ss = sq[:, 0:V7X_LANES]
            for j in range(1, D // V7X_LANES):
                ss = ss + sq[:, j * V7X_LANES:(j + 1) * V7X_LANES]
            ss_s[rs, :] = ss
            if prefetch:
                issue_rows(i + 1, 1 - slot, c * RC, RC // 2)
            return carry

        lax.fori_loop(0, R // RC, accumulate, 0, unroll=4)

        def normalize(c, carry):
            rs = chunk(c)
            ms = jnp.sum(ss_s[rs, :], axis=-1, keepdims=True) * (1.0 / D)
            o_ref[rs, :] = o_ref[rs, :] * lax.rsqrt(ms + EPS) * g_fin
            if prefetch:
                issue_rows(i + 1, 1 - slot, c * RC + RC // 2, RC // 2)
            return carry

        lax.fori_loop(0, R // RC, normalize, 0, unroll=4)

    @pl.when(i + 1 < n_steps)
    def _():
        passes(True)

    @pl.when(i + 1 == n_steps)
    def _():
        passes(False)


def _moe_combine(h, y, dest, gate, n_final):
    T, D = h.shape
    R = GATHER_ROWS
    est = 2 * TOP_K * R * D * 4 + 4 * R * D * 4 + 3 * R * D * 4
    return pl.pallas_call(
        _combine_body,
        grid_spec=pltpu.PrefetchScalarGridSpec(
            num_scalar_prefetch=1,
            grid=(T // R,),
            in_specs=[pl.BlockSpec((R, D), lambda i, dest: (i, 0)),
                      pl.BlockSpec((R, V7X_LANES), lambda i, dest: (i, 0)),
                      pl.BlockSpec((1, D), lambda i, dest: (0, 0)),
                      pl.BlockSpec(memory_space=pl.ANY)],
            out_specs=pl.BlockSpec((R, D), lambda i, dest: (i, 0)),
            scratch_shapes=[pltpu.VMEM((2, TOP_K, R, D), f32), pltpu.VMEM((R, V7X_LANES), f32),
                            pltpu.SemaphoreType.DMA((2,))],
        ),
        out_shape=jax.ShapeDtypeStruct((T, D), f32),
        compiler_params=_params(("arbitrary",), est),
        name="moe_combine_norm",
    )(dest, h, gate, n_final.reshape(1, D).astype(f32), y)


def _slot_layout(rec):
    T = rec.shape[0]
    A = T * TOP_K
    eid = rec[:, ROUTE_EID_LANE:ROUTE_EID_LANE + TOP_K].astype(jnp.int32).reshape(A)
    onehot = (eid[:, None] == jnp.arange(N_EXPERTS, dtype=jnp.int32)[None, :]).astype(jnp.int32)
    csum = jnp.cumsum(onehot, axis=0)
    rank = jnp.sum(onehot * csum, axis=1) - 1
    counts = csum[-1]
    pcounts = (counts + MOE_ROWS - 1) // MOE_ROWS * MOE_ROWS
    pend = jnp.cumsum(pcounts)
    pstart = pend - pcounts
    dest = (pstart[eid] + rank).astype(jnp.int32)
    n_blocks = -(-(A + N_EXPERTS * (MOE_ROWS - 1)) // MOE_ROWS)
    P = n_blocks * MOE_ROWS
    blk0 = (pstart // MOE_ROWS).astype(jnp.int32)
    nblk = (pcounts // MOE_ROWS).astype(jnp.int32)
    pad_end = pend.astype(jnp.int32)
    pad_len = (pcounts - counts).astype(jnp.int32)
    return dest, blk0, nblk, pad_end, pad_len, P


def kernel(x, mem, n_mix, w_in, gmlp_ln_g, gmlp_ln_b, gmlp_w_s, gmlp_b_s, hgrn_lower_bounds,
           hgrn_norm_g, w_out, n_cross, n_mem, w_q_x, w_k_x, w_v_x, w_o_x, n_moe, w_group,
           b_group, w_router, b_router, w1_e, w3_e, w2_e, n_final):
    B, S, D = x.shape
    n_mem_tok = mem.shape[1]
    T = B * S
    xt = x.reshape(T, D)

    hn = _rmsnorm(xt, n_mix[0], bf16)
    z = _matmul(hn, w_in[0], bf16, 1024, 512, "mm_w_in")
    ya = _gmlp(z, gmlp_ln_g[0], gmlp_ln_b[0], gmlp_w_s[0].astype(f32), gmlp_b_s[0].astype(f32))
    yb = _hgrn(z, hgrn_lower_bounds, hgrn_norm_g[0], B, S)
    h1, h1b, h1ss = _matmul_res(xt, [ya, yb], w_out[0], 1024, 512, "mm_w_out", with_norm_inputs=True)

    mn = _rmsnorm(mem.reshape(B * n_mem_tok, D), n_mem[0], bf16)
    kx = _matmul(mn, w_k_x[0], bf16, B * n_mem_tok, 512, "mm_w_k")
    vx = _matmul(mn, w_v_x[0], bf16, B * n_mem_tok, 512, "mm_w_v")
    wq_gained = (w_q_x[0] * n_cross[0].astype(f32)[:, None]).astype(bf16)
    ox = _xattn(h1b, h1ss, wq_gained, kx, vx, B, S, n_mem_tok)
    h2 = _matmul_res(h1, [ox], w_o_x[0], 1024, 512, "mm_w_o")

    xm, rec = _router(h2, n_moe[0], w_group[0], b_group[0], w_router[0], b_router[0])
    dest, blk0, nblk, pad_end, pad_len, P = _slot_layout(rec)
    xs = _moe_dispatch(xm, dest, pad_end, pad_len, P)
    hid = _moe_hidden(xs, w1_e[0], w3_e[0], blk0, nblk)
    y = _moe_out(hid, w2_e[0], blk0, nblk)
    out = _moe_combine(h2, y, dest, rec, n_final)
    return out.reshape(B, S, D)
```

```python
import functools

import jax
import jax.numpy as jnp
from jax import lax
from jax.experimental import pallas as pl
from jax.experimental.pallas import tpu as pltpu

f32 = jnp.float32
bf16 = jnp.bfloat16

EPS = 1e-6
D_MODEL = 4096
CHUNK = 64
GMLP_HEADS = 4
GMLP_HEAD_DIM = 512
GMLP_BLOCK = 128
D_GMLP = GMLP_HEADS * GMLP_HEAD_DIM
HGRN_HEAD_DIM = 128
HGRN_HEADS = 16
D_HGRN = HGRN_HEADS * HGRN_HEAD_DIM
X_HEADS = 4
X_HEAD_DIM = 1024
N_GROUPS = 4
EXPERTS_PER_GROUP = 8
N_EXPERTS = N_GROUPS * EXPERTS_PER_GROUP
TOP_K = 2
D_EXPERT = 1024

V7X_LANES = 128
V7X_SUBLANES = 8
V7X_VMEM_LIMIT_BYTES = 60000 * 1024

MOE_ROWS = 256
MOE_F_TILE = 512
MOE_N_TILE = 4096
MOE_W_CHUNKS = 8
GATHER_ROWS = 256
HGRN_GROUP = 4
HGRN_ROWS = 512


def _params(sem, est_bytes):
    limit = min(int(est_bytes) + (8 << 20), V7X_VMEM_LIMIT_BYTES)
    return pltpu.CompilerParams(dimension_semantics=sem, vmem_limit_bytes=limit)


def _rms_body(x_ref, g_ref, o_ref):
    x = x_ref[...].astype(f32)
    ms = jnp.mean(x * x, axis=-1, keepdims=True)
    o_ref[...] = (x * lax.rsqrt(ms + EPS) * g_ref[...]).astype(o_ref.dtype)


def _rmsnorm(x, g, out_dtype, tm=256):
    M, D = x.shape
    est = 2 * tm * D * (x.dtype.itemsize + jnp.dtype(out_dtype).itemsize) + 2 * tm * D * 4
    return pl.pallas_call(
        _rms_body,
        grid=(M // tm,),
        in_specs=[pl.BlockSpec((tm, D), lambda i: (i, 0)),
                  pl.BlockSpec((1, D), lambda i: (0, 0))],
        out_specs=pl.BlockSpec((tm, D), lambda i: (i, 0)),
        out_shape=jax.ShapeDtypeStruct((M, D), out_dtype),
        compiler_params=_params(("arbitrary",), est),
        name="rmsnorm",
    )(x, g.reshape(1, D).astype(f32))


CAST_ROWS = 512


def _cast_tile(src_ref, dst_ref, dst_row0=0):
    rows = min(CAST_ROWS, src_ref.shape[0])
    n = src_ref.shape[0] // rows

    def piece(p, c):
        r = pl.multiple_of(p * rows, rows)
        d = pl.multiple_of(dst_row0 + p * rows, rows)
        dst_ref[pl.ds(d, rows), :] = src_ref[pl.ds(r, rows), :].astype(bf16)
        return c

    lax.fori_loop(0, n, piece, 0)


def _mm_body(a_ref, w_ref, o_ref, wb_ref):
    @pl.when(pl.program_id(1) == 0)
    def _():
        _cast_tile(w_ref, wb_ref)

    o_ref[...] = jnp.dot(a_ref[...], wb_ref[...], preferred_element_type=f32).astype(o_ref.dtype)


def _matmul(a, w, out_dtype, tm, tn, name):
    M, K = a.shape
    N = w.shape[1]
    osz = jnp.dtype(out_dtype).itemsize
    est = 2 * (tm * K * 2 + K * tn * 4 + tm * tn * osz) + K * tn * 2 + tm * tn * 4
    return pl.pallas_call(
        _mm_body,
        grid=(N // tn, M // tm),
        in_specs=[pl.BlockSpec((tm, K), lambda j, i: (i, 0)),
                  pl.BlockSpec((K, tn), lambda j, i: (0, j))],
        out_specs=pl.BlockSpec((tm, tn), lambda j, i: (i, j)),
        out_shape=jax.ShapeDtypeStruct((M, N), out_dtype),
        scratch_shapes=[pltpu.VMEM((K, tn), bf16)],
        compiler_params=_params(("arbitrary", "arbitrary"), est),
        name=name,
    )(a, w)


def _mm_res_body(n_parts, with_norm_inputs, r_ref, *refs):
    a_refs = refs[:n_parts]
    w_refs = refs[n_parts:2 * n_parts]
    out_refs = refs[2 * n_parts:-1]
    wb_ref = refs[-1]
    kp = w_refs[0].shape[0]

    @pl.when(pl.program_id(1) == 0)
    def _():
        for p in range(n_parts):
            _cast_tile(w_refs[p], wb_ref, p * kp)

    acc = r_ref[...]
    for p in range(n_parts):
        acc = acc + jnp.dot(a_refs[p][...], wb_ref[p * kp:(p + 1) * kp, :], preferred_element_type=f32)
    out_refs[0][...] = acc
    if with_norm_inputs:
        out_refs[1][...] = acc.astype(bf16)
        sq = acc * acc
        ss = sq[:, 0:V7X_LANES]
        for g in range(1, acc.shape[1] // V7X_LANES):
            ss = ss + sq[:, g * V7X_LANES:(g + 1) * V7X_LANES]
        out_refs[2][...] = ss


def _matmul_res(res, a_list, w, tm, tn, name, with_norm_inputs=False):
    M, N = res.shape
    n = len(a_list)
    Kp = a_list[0].shape[1]
    a_specs = [pl.BlockSpec((tm, Kp), lambda j, i: (i, 0)) for _ in range(n)]
    w_specs = [pl.BlockSpec((Kp, tn), functools.partial(lambda j, i, p: (p, j), p=p)) for p in range(n)]
    est = 2 * (n * tm * Kp * 2 + n * Kp * tn * 4 + 2 * tm * tn * 4) + n * Kp * tn * 2 + 2 * tm * tn * 4
    tile = pl.BlockSpec((tm, tn), lambda j, i: (i, j))
    out_specs, out_shape = [tile], [jax.ShapeDtypeStruct((M, N), f32)]
    if with_norm_inputs:
        est += 2 * tm * tn * 2 + 2 * tm * V7X_LANES * 4 + tm * tn * 4
        out_specs += [tile, pl.BlockSpec((tm, V7X_LANES), lambda j, i: (i, j))]
        out_shape += [jax.ShapeDtypeStruct((M, N), bf16),
                      jax.ShapeDtypeStruct((M, N // tn * V7X_LANES), f32)]
    outs = pl.pallas_call(
        functools.partial(_mm_res_body, n, with_norm_inputs),
        grid=(N // tn, M // tm),
        in_specs=[tile] + a_specs + w_specs,
        out_specs=out_specs,
        out_shape=out_shape,
        scratch_shapes=[pltpu.VMEM((n * Kp, tn), bf16)],
        compiler_params=_params(("arbitrary", "arbitrary"), est),
        name=name,
    )(res, *a_list, *([w] * n))
    return outs if with_norm_inputs else outs[0]


def _gelu(x):
    return 0.5 * x * (1.0 + lax.erf(x * 0.7071067811865476))


def _gmlp_body(u_ref, v_ref, lng_ref, lnb_ref, ws_ref, bs_ref, o_ref):
    rows = u_ref.shape[0]
    gv = _gelu(v_ref[...].astype(f32))
    mu = jnp.mean(gv, axis=-1, keepdims=True)
    xc = gv - mu
    var = jnp.mean(xc * xc, axis=-1, keepdims=True)
    vn = (xc * lax.rsqrt(var + EPS) * lng_ref[...] + lnb_ref[...]).astype(bf16)
    ci = lax.broadcasted_iota(jnp.int32, (GMLP_BLOCK, GMLP_BLOCK), 0) // CHUNK
    cj = lax.broadcasted_iota(jnp.int32, (GMLP_BLOCK, GMLP_BLOCK), 1) // CHUNK
    w = jnp.where(cj <= ci, ws_ref[0], 0.0).astype(bf16)
    bias = bs_ref[0]
    for blk in range(rows // GMLP_BLOCK):
        sl = slice(blk * GMLP_BLOCK, (blk + 1) * GMLP_BLOCK)
        s = jnp.dot(w, vn[sl], preferred_element_type=f32) + bias
        o_ref[sl, :] = (_gelu(u_ref[sl, :].astype(f32)) * s).astype(o_ref.dtype)


def _gmlp(z, ln_g, ln_b, w_s, b_s, tr=1024):
    M = z.shape[0]
    C = GMLP_HEAD_DIM
    est = 2 * 3 * tr * C * 2 + 8 * tr * C * 4
    return pl.pallas_call(
        _gmlp_body,
        grid=(M // tr, GMLP_HEADS),
        in_specs=[pl.BlockSpec((tr, C), lambda i, h: (i, h)),
                  pl.BlockSpec((tr, C), lambda i, h: (i, GMLP_HEADS + h)),
                  pl.BlockSpec((1, C), lambda i, h: (0, h)),
                  pl.BlockSpec((1, C), lambda i, h: (0, h)),
                  pl.BlockSpec((1, GMLP_BLOCK, GMLP_BLOCK), lambda i, h: (h, 0, 0)),
                  pl.BlockSpec((1, GMLP_BLOCK, 1), lambda i, h: (h, 0, 0))],
        out_specs=pl.BlockSpec((tr, C), lambda i, h: (i, h)),
        out_shape=jax.ShapeDtypeStruct((M, D_GMLP), bf16),
        compiler_params=_params(("arbitrary", "arbitrary"), est),
        name="gmlp_mixer",
    )(z, z, ln_g.reshape(1, D_GMLP), ln_b.reshape(1, D_GMLP), w_s,
      b_s.reshape(GMLP_HEADS, GMLP_BLOCK, 1))


def _split3(x):
    hi = x.astype(bf16)
    r1 = x - hi.astype(f32)
    mid = r1.astype(bf16)
    lo = (r1 - mid.astype(f32)).astype(bf16)
    return hi, mid, lo


HGRN_STRIP = 2 * CHUNK
_NT = (((1,), (1,)), ((), ()))
_TN = (((0,), (0,)), ((), ()))


def _split2(x):
    hi = x.astype(bf16)
    mid = (x - hi.astype(f32)).astype(bf16)
    return hi, mid


def _hgrn_body(q_ref, f_ref, i_ref, g_ref, lbr_ref, ng_ref, o_ref,
               st_ref, qin_s, kin_s, qex_s, kdec_s, dec_s, am_s, xq_s, xk_s, inc_s, smat_s, oacc_s):
    C, K, G = CHUNK, HGRN_HEAD_DIM, HGRN_GROUP
    rows, W = q_ref.shape
    n_chunks = rows // C

    @pl.when(pl.program_id(2) == 0)
    def _():
        st_ref[...] = jnp.zeros_like(st_ref)

    r0 = lbr_ref[0:1, :]
    r1 = lbr_ref[1:2, :]
    m = jnp.maximum(r0, r1)
    e0 = jnp.exp(r0 - m)
    e1 = jnp.exp(r1 - m)
    lb = e0 / (e0 + e1)

    ti = lax.broadcasted_iota(jnp.int32, (C, C), 0)
    si = lax.broadcasted_iota(jnp.int32, (C, C), 1)
    tril = (si <= ti).astype(bf16)

    def chunk_rows(c):
        return pl.ds(pl.multiple_of(c * C, C), C)

    def gates(c, carry):
        rs = chunk_rows(c)
        fg = lb + (1.0 - lb) * jax.nn.sigmoid(f_ref[rs, :].astype(f32))
        hi, mid, lo = _split3(jnp.log(fg))
        b = (jnp.dot(tril, hi, preferred_element_type=f32)
             + jnp.dot(tril, mid, preferred_element_type=f32)
             + jnp.dot(tril, lo, preferred_element_type=f32))
        b_mid = b[C // 2:C // 2 + 1, :]
        b_last = b[C - 1:C, :]
        q_in = jax.nn.silu(q_ref[rs, :].astype(f32)) * jnp.exp(b - b_mid)
        k_in = (1.0 - fg) * jnp.exp(b_mid - b)
        qin_s[rs, :] = q_in.astype(bf16)
        kin_s[rs, :] = k_in.astype(bf16)
        qex_s[rs, :] = (q_in * jnp.exp(b_mid)).astype(bf16)
        kdec_s[rs, :] = (k_in * jnp.exp(b_last - b_mid)).astype(bf16)
        dec_s[pl.ds(pl.multiple_of(c * 8, 8), 8), :] = jnp.broadcast_to(jnp.exp(b_last), (8, W))
        return carry

    lax.fori_loop(0, n_chunks, gates, 0, unroll=2)

    @pl.when((pl.program_id(0) == 0) & (pl.program_id(1) == 0) & (pl.program_id(2) == 0))
    def _():
        am_s[...] = jnp.zeros_like(am_s)
        xq_s[...] = jnp.zeros_like(xq_s)
        xk_s[...] = jnp.zeros_like(xk_s)

    S = HGRN_STRIP
    tt = lax.broadcasted_iota(jnp.int32, (S, S), 0)
    ss = lax.broadcasted_iota(jnp.int32, (S, S), 1)
    keep = jnp.logical_and(ss <= tt, (ss // C) == (tt // C))

    for h in range(G):
        ls = slice(h * K, (h + 1) * K)
        for c in range(n_chunks):
            xq_s[h, c * C:(c + 1) * C, c * K:(c + 1) * K] = qex_s[c * C:(c + 1) * C, ls]
            xk_s[h, c * C:(c + 1) * C, c * K:(c + 1) * K] = kdec_s[c * C:(c + 1) * C, ls]

        a = lax.dot_general(qin_s[:, ls], kin_s[:, ls], _NT, preferred_element_type=f32)
        for r in range(rows // S):
            sl = slice(r * S, (r + 1) * S)
            am_s[sl, sl] = jnp.where(keep, a[sl, sl], 0.0).astype(bf16)
        oacc_s[:, ls] = jnp.dot(am_s[...], i_ref[:, ls], preferred_element_type=f32)

        inc_s[...] = lax.dot_general(i_ref[:, ls], xk_s[h], _TN, preferred_element_type=f32)

        st = st_ref[h]
        for c in range(n_chunks):
            cs = slice(c * K, (c + 1) * K)
            smat_s[:, cs] = st.astype(bf16)
            st = st * dec_s[c * 8:c * 8 + 1, ls] + inc_s[:, cs]
        st_ref[h] = st

        oacc_s[:, ls] += lax.dot_general(xq_s[h], smat_s[...], _NT, preferred_element_type=f32)

    ng = ng_ref[...]

    def outputs(c, carry):
        rs = chunk_rows(c)
        for h in range(G):
            ls = slice(h * K, (h + 1) * K)
            o = oacc_s[rs, ls]
            ms = jnp.mean(o * o, axis=-1, keepdims=True)
            o = o * lax.rsqrt(ms + EPS) * ng[:, ls]
            o = o * jax.nn.silu(g_ref[rs, ls].astype(f32))
            o_ref[rs, ls] = o.astype(o_ref.dtype)
        return carry

    lax.fori_loop(0, n_chunks, outputs, 0, unroll=2)


def _hgrn(z, lower_bounds, norm_g, batch, seq):
    M = z.shape[0]
    W = HGRN_GROUP * HGRN_HEAD_DIM
    ts = HGRN_ROWS
    n_s = seq // ts
    n_hg = D_HGRN // W
    col0 = 2 * D_GMLP // W

    def zspec(part):
        return pl.BlockSpec((ts, W), lambda b, hg, s, part=part: (b * n_s + s, col0 + part * n_hg + hg))

    G, K = HGRN_GROUP, HGRN_HEAD_DIM
    n_chunks = ts // CHUNK
    est = (2 * 5 * ts * W * 2 + 4 * ts * W * 2 + ts * W * 4 + ts * ts * 2 + 2 * G * ts * n_chunks * K * 2
           + K * n_chunks * K * 6 + G * K * K * 4 + 3 * ts * ts * 4)
    return pl.pallas_call(
        _hgrn_body,
        grid=(batch, n_hg, n_s),
        in_specs=[zspec(0), zspec(1), zspec(2), zspec(3),
                  pl.BlockSpec((2, W), lambda b, hg, s: (0, hg)),
                  pl.BlockSpec((1, W), lambda b, hg, s: (0, hg))],
        out_specs=pl.BlockSpec((ts, W), lambda b, hg, s: (b * n_s + s, hg)),
        out_shape=jax.ShapeDtypeStruct((M, D_HGRN), bf16),
        scratch_shapes=[pltpu.VMEM((G, K, K), f32),
                        pltpu.VMEM((ts, W), bf16),
                        pltpu.VMEM((ts, W), bf16),
                        pltpu.VMEM((ts, W), bf16),
                        pltpu.VMEM((ts, W), bf16),
                        pltpu.VMEM((n_chunks * 8, W), f32),
                        pltpu.VMEM((ts, ts), bf16),
                        pltpu.VMEM((G, ts, n_chunks * K), bf16),
                        pltpu.VMEM((G, ts, n_chunks * K), bf16),
                        pltpu.VMEM((K, n_chunks * K), f32),
                        pltpu.VMEM((K, n_chunks * K), bf16),
                        pltpu.VMEM((ts, W), f32)],
        compiler_params=_params(("arbitrary", "arbitrary", "arbitrary"), est),
        name="hgrn2_mixer",
    )(z, z, z, z, lower_bounds.astype(f32), norm_g.reshape(1, D_HGRN).astype(f32))


def _xattn_body(h_ref, ss_ref, wq_ref, k_ref, v_ref, o_ref):
    D = h_ref.shape[1]
    inv_rms = lax.rsqrt(jnp.sum(ss_ref[...], axis=-1, keepdims=True) * (1.0 / D) + EPS)
    q = (jnp.dot(h_ref[...], wq_ref[...], preferred_element_type=f32) * inv_rms).astype(bf16)
    s = lax.dot_general(q, k_ref[...], (((1,), (1,)), ((), ())), preferred_element_type=f32)
    s = s * (X_HEAD_DIM ** -0.5)
    s = s - jnp.max(s, axis=-1, keepdims=True)
    p = jnp.exp(s)
    p = p / jnp.sum(p, axis=-1, keepdims=True)
    o_ref[...] = jnp.dot(p.astype(bf16), v_ref[...], preferred_element_type=f32).astype(o_ref.dtype)


def _xattn(hb, ss, wq, k, v, batch, seq, n_mem, tq=1024):
    M, D = hb.shape
    n_s = seq // tq
    dh = X_HEAD_DIM
    est = (2 * (tq * D * 2 + tq * ss.shape[1] * 4 + D * dh * 2 + 2 * n_mem * dh * 2 + tq * dh * 2)
           + 3 * tq * dh * 4)
    return pl.pallas_call(
        _xattn_body,
        grid=(X_HEADS, batch, n_s),
        in_specs=[pl.BlockSpec((tq, D), lambda h, b, s: (b * n_s + s, 0)),
                  pl.BlockSpec((tq, ss.shape[1]), lambda h, b, s: (b * n_s + s, 0)),
                  pl.BlockSpec((D, dh), lambda h, b, s: (0, h)),
                  pl.BlockSpec((n_mem, dh), lambda h, b, s: (b, h)),
                  pl.BlockSpec((n_mem, dh), lambda h, b, s: (b, h))],
        out_specs=pl.BlockSpec((tq, dh), lambda h, b, s: (b * n_s + s, h)),
        out_shape=jax.ShapeDtypeStruct((M, D), bf16),
        compiler_params=_params(("arbitrary", "arbitrary", "arbitrary"), est),
        name="xattn_qproj",
    )(hb, ss, wq, k, v)


u32 = jnp.uint32
ROUTE_GATE_LANE = 0
ROUTE_EID_LANE = 2


def _pack_bf16_pair(lo, hi):
    lo32 = lax.bitcast_convert_type(lo.astype(f32), u32) >> 16
    hi32 = lax.bitcast_convert_type(hi.astype(f32), u32) & jnp.uint32(0xFFFF0000)
    return lo32 | hi32


def _unpack_bf16_pair(w):
    lo = lax.bitcast_convert_type(w << 16, f32).astype(bf16)
    hi = lax.bitcast_convert_type(w & jnp.uint32(0xFFFF0000), f32).astype(bf16)
    return lo, hi


def _unpack_pair_f32(w):
    return (lax.bitcast_convert_type(w << 16, f32),
            lax.bitcast_convert_type(w & jnp.uint32(0xFFFF0000), f32))


def _first_lane_of(mask, lane):
    return jnp.min(jnp.where(mask, lane, V7X_LANES), axis=-1, keepdims=True)


def _router_body(h_ref, g_ref, wr_ref, br_ref, xm_ref, r_ref):
    x = h_ref[...]
    D = x.shape[1]
    ms = jnp.mean(x * x, axis=-1, keepdims=True)
    xn = x * lax.rsqrt(ms + EPS) * g_ref[...]
    xh, xm = _split2(xn)
    wh, wm = _split2(wr_ref[...])
    dot = functools.partial(jnp.dot, preferred_element_type=f32)
    acc = dot(xm, wh) + dot(xh, wm)
    acc = acc + dot(xh, wh)
    logits = acc + br_ref[...]

    xm_ref[...] = _pack_bf16_pair(xh[:, :D // 2], xh[:, D // 2:])

    lane = lax.broadcasted_iota(jnp.int32, logits.shape, 1)
    neg = jnp.float32(-jnp.inf)
    is_g = lane < N_GROUPS
    gl = jnp.where(is_g, logits, neg)
    gmax = jnp.max(gl, axis=-1, keepdims=True)
    grp = _first_lane_of(gl == gmax, lane)
    p_grp = 1.0 / jnp.sum(jnp.where(is_g, jnp.exp(logits - gmax), 0.0), axis=-1, keepdims=True)
    e0 = N_GROUPS + grp * EXPERTS_PER_GROUP
    el = jnp.where(jnp.logical_and(lane >= e0, lane < e0 + EXPERTS_PER_GROUP), logits, neg)
    m1 = jnp.max(el, axis=-1, keepdims=True)
    i1 = _first_lane_of(el == m1, lane)
    el2 = jnp.where(lane == i1, neg, el)
    m2 = jnp.max(el2, axis=-1, keepdims=True)
    i2 = _first_lane_of(el2 == m2, lane)
    t = jnp.exp(m2 - m1)
    g1 = p_grp * (1.0 / (1.0 + t))
    g2 = p_grp * (t / (1.0 + t))
    rec = jnp.where(lane == ROUTE_GATE_LANE, g1, 0.0)
    rec = jnp.where(lane == ROUTE_GATE_LANE + 1, g2, rec)
    rec = jnp.where(lane == ROUTE_EID_LANE, (i1 - N_GROUPS).astype(f32), rec)
    rec = jnp.where(lane == ROUTE_EID_LANE + 1, (i2 - N_GROUPS).astype(f32), rec)
    r_ref[...] = rec


def _router(h, n_moe, w_group, b_group, w_router, b_router, tm=256):
    M, D = h.shape
    n_log = N_GROUPS + N_EXPERTS
    wr = jnp.concatenate([w_group] + [w_router[g] for g in range(N_GROUPS)], axis=1)
    wr = jnp.pad(wr.astype(f32), ((0, 0), (0, V7X_LANES - n_log)))
    br = jnp.pad(jnp.concatenate([b_group, b_router.reshape(-1)]).astype(f32), (0, V7X_LANES - n_log))
    est = 2 * tm * D * 4 + 2 * D * V7X_LANES * 4 + 2 * tm * D * 2 + 6 * tm * D * 4
    return pl.pallas_call(
        _router_body,
        grid=(M // tm,),
        in_specs=[pl.BlockSpec((tm, D), lambda i: (i, 0)),
                  pl.BlockSpec((1, D), lambda i: (0, 0)),
                  pl.BlockSpec((D, V7X_LANES), lambda i: (0, 0)),
                  pl.BlockSpec((1, V7X_LANES), lambda i: (0, 0))],
        out_specs=[pl.BlockSpec((tm, D // 2), lambda i: (i, 0)),
                   pl.BlockSpec((tm, V7X_LANES), lambda i: (i, 0))],
        out_shape=[jax.ShapeDtypeStruct((M, D // 2), u32),
                   jax.ShapeDtypeStruct((M, V7X_LANES), f32)],
        compiler_params=_params(("arbitrary",), est),
        name="moe_router",
    )(h, n_moe.reshape(1, D).astype(f32), wr, br.reshape(1, V7X_LANES))


def _row_copy(src_hbm, row, dst, sem):
    return pltpu.make_async_copy(src_hbm.at[pl.ds(row, 1)], dst, sem)


PAD_BITS = tuple(1 << k for k in reversed(range((MOE_ROWS - 1).bit_length())))


def _dispatch_body(dest_ref, padend_ref, padlen_ref, xm_ref, xs_hbm, buf, zbuf, sem, zsem):
    R = GATHER_ROWS
    i = pl.program_id(0)
    n_steps = pl.num_programs(0)
    slot = i % 2

    def wait_slot(s):
        for _ in range(TOP_K):
            pltpu.make_async_copy(buf.at[s], xs_hbm.at[pl.ds(0, R)], sem.at[s]).wait()

    def pad_copies(fn):
        def per_expert(e, c):
            ln = padlen_ref[e]
            end = padend_ref[e]
            for bit in PAD_BITS:
                rows_per_copy = bit if bit >= V7X_SUBLANES else 1
                for j in range(bit // rows_per_copy):
                    @pl.when((ln & bit) != 0)
                    def _(bit=bit, end=end, j=j, n=rows_per_copy):
                        start = end - bit + j * n
                        if n > 1:
                            start = pl.multiple_of(start, V7X_SUBLANES)
                        fn(pltpu.make_async_copy(zbuf.at[pl.ds(0, n)], xs_hbm.at[pl.ds(start, n)], zsem))
                end = end - (ln & bit)
            return c
        lax.fori_loop(0, N_EXPERTS, per_expert, 0)

    @pl.when(i == 0)
    def _():
        zbuf[...] = jnp.zeros_like(zbuf)
        pad_copies(lambda cp: cp.start())

    @pl.when(i >= 2)
    def _():
        wait_slot(slot)

    buf[slot] = xm_ref[...]

    def issue(r, c):
        for k in range(TOP_K):
            d = dest_ref[(i * R + r) * TOP_K + k]
            pltpu.make_async_copy(buf.at[slot, pl.ds(r, 1)], xs_hbm.at[pl.ds(d, 1)], sem.at[slot]).start()
        return c

    lax.fori_loop(0, R, issue, 0, unroll=8)

    @pl.when(i == n_steps - 1)
    def _():
        if n_steps >= 2:
            wait_slot(1 - slot)
        wait_slot(slot)
        pad_copies(lambda cp: cp.wait())


def _moe_dispatch(xm, dest, pad_end, pad_len, P):
    T, Dh = xm.shape
    R = GATHER_ROWS
    est = 2 * R * Dh * 4 + 2 * R * Dh * 4 + PAD_BITS[0] * Dh * 4
    return pl.pallas_call(
        _dispatch_body,
        grid_spec=pltpu.PrefetchScalarGridSpec(
            num_scalar_prefetch=3,
            grid=(T // R,),
            in_specs=[pl.BlockSpec((R, Dh), lambda i, dest, pend, plen: (i, 0))],
            out_specs=pl.BlockSpec(memory_space=pl.ANY),
            scratch_shapes=[pltpu.VMEM((2, R, Dh), u32), pltpu.VMEM((PAD_BITS[0], Dh), u32),
                            pltpu.SemaphoreType.DMA((2,)), pltpu.SemaphoreType.DMA],
        ),
        out_shape=jax.ShapeDtypeStruct((P, Dh), u32),
        compiler_params=_params(("arbitrary",), est),
        name="moe_dispatch",
    )(dest, pad_end, pad_len, xm)


class _WeightStream:
    def __init__(self, w_hbms, wf_ref, wsem, col_tile):
        self.w_hbms, self.wf, self.wsem, self.ct = w_hbms, wf_ref, wsem, col_tile
        self.rows = wf_ref.shape[2]
        self.cr = self.rows // MOE_W_CHUNKS

    def _chunk(self, m, c, step, slot):
        e = step % N_EXPERTS
        t = step // N_EXPERTS
        r = pl.ds(pl.multiple_of(c * self.cr, self.cr), self.cr)
        cols = pl.ds(pl.multiple_of(t * self.ct, V7X_LANES), self.ct)
        return pltpu.make_async_copy(self.w_hbms[m].at[e, r, cols], self.wf.at[slot, m, r], self.wsem.at[slot])

    def start_chunk(self, c, step):
        for m in range(len(self.w_hbms)):
            self._chunk(m, c, step, step % 2).start()

    def start_chunks(self, c0, c1, step):
        lax.fori_loop(c0, c1, lambda c, carry: (self.start_chunk(c, step), carry)[1], 0)

    def wait(self, step):
        slot = step % 2
        for m in range(len(self.w_hbms)):
            for c in range(MOE_W_CHUNKS):
                self._chunk(m, c, step, slot).wait()

    def tile(self, m, step):
        return self.wf.at[step % 2, m]


def _expert_block_loop(row0, n_blocks, in_hbm, out_hbm, out_col0, ibuf, obuf, isem, osem, compute,
                       per_block):
    Bm = MOE_ROWS
    out_cols = obuf.shape[2]

    def rows(j):
        return pl.ds(pl.multiple_of(row0 + j * Bm, Bm), Bm)

    def in_copy(j, slot):
        return pltpu.make_async_copy(in_hbm.at[rows(j)], ibuf.at[slot], isem.at[slot])

    def out_copy(j, slot):
        return pltpu.make_async_copy(
            obuf.at[slot], out_hbm.at[rows(j), pl.ds(pl.multiple_of(out_col0, V7X_LANES), out_cols)],
            osem.at[slot])

    @pl.when(n_blocks > 0)
    def _():
        in_copy(0, 0).start()

    def block(j, carry):
        slot = j % 2

        @pl.when(j + 1 < n_blocks)
        def _():
            in_copy(j + 1, 1 - slot).start()

        per_block(j)
        in_copy(j, slot).wait()

        @pl.when(j >= 2)
        def _():
            out_copy(j - 2, slot).wait()

        obuf[slot] = compute(ibuf[slot])
        out_copy(j, slot).start()
        return carry

    lax.fori_loop(0, n_blocks, block, 0)

    @pl.when(n_blocks >= 2)
    def _():
        out_copy(n_blocks - 2, n_blocks % 2).wait()

    @pl.when(n_blocks >= 1)
    def _():
        out_copy(n_blocks - 1, (n_blocks - 1) % 2).wait()


def _expert_step(stream, n_blocks, cast, run_blocks):
    step = pl.program_id(0) * N_EXPERTS + pl.program_id(1)
    n_steps = pl.num_programs(0) * N_EXPERTS
    has_next = step + 1 < n_steps

    @pl.when(step == 0)
    def _():
        stream.start_chunks(0, MOE_W_CHUNKS, step)

    stream.wait(step)

    @pl.when(n_blocks > 0)
    def _():
        cast(step)

    per = (MOE_W_CHUNKS + jnp.maximum(n_blocks, 1) - 1) // jnp.maximum(n_blocks, 1)

    @pl.when(jnp.logical_and(has_next, n_blocks == 0))
    def _():
        stream.start_chunks(0, MOE_W_CHUNKS, step + 1)

    def per_block(j):
        @pl.when(has_next)
        def _():
            stream.start_chunks(jnp.minimum(j * per, MOE_W_CHUNKS), jnp.minimum((j + 1) * per, MOE_W_CHUNKS),
                                step + 1)

    run_blocks(per_block)


def _moe_hidden_body(blk0_ref, nblk_ref, w1_hbm, w3_hbm, xs_hbm, hid_hbm,
                     wf_ref, w1b_ref, w3b_ref, ibuf, obuf, wsem, isem, osem):
    f = pl.program_id(0)
    e = pl.program_id(1)
    n_blocks = nblk_ref[e]
    half = ibuf.shape[2]
    stream = _WeightStream((w1_hbm, w3_hbm), wf_ref, wsem, MOE_F_TILE)

    def cast(step):
        _cast_tile(stream.tile(0, step), w1b_ref)
        _cast_tile(stream.tile(1, step), w3b_ref)

    def compute(xw):
        x_lo, x_hi = _unpack_bf16_pair(xw)

        def proj(wb_ref):
            return (jnp.dot(x_lo, wb_ref[:half, :], preferred_element_type=f32)
                    + jnp.dot(x_hi, wb_ref[half:, :], preferred_element_type=f32))

        return (jax.nn.silu(proj(w1b_ref)) * proj(w3b_ref)).astype(obuf.dtype)

    _expert_step(stream, n_blocks, cast, functools.partial(
        _expert_block_loop, blk0_ref[e] * MOE_ROWS, n_blocks, xs_hbm, hid_hbm, f * MOE_F_TILE,
        ibuf, obuf, isem, osem, compute))


def _moe_hidden(xs, w1, w3, blk0, nblk):
    P, Dh = xs.shape
    D = w1.shape[1]
    Bm, Ft = MOE_ROWS, MOE_F_TILE
    est = 2 * 2 * D * Ft * 4 + 2 * D * Ft * 2 + 2 * Bm * Dh * 4 + 2 * Bm * Ft * 2 + Bm * D * 2 + 3 * Bm * Ft * 4
    return pl.pallas_call(
        _moe_hidden_body,
        grid_spec=pltpu.PrefetchScalarGridSpec(
            num_scalar_prefetch=2,
            grid=(D_EXPERT // Ft, N_EXPERTS),
            in_specs=[pl.BlockSpec(memory_space=pl.ANY)] * 3,
            out_specs=pl.BlockSpec(memory_space=pl.ANY),
            scratch_shapes=[pltpu.VMEM((2, 2, D, Ft), f32),
                            pltpu.VMEM((D, Ft), bf16), pltpu.VMEM((D, Ft), bf16),
                            pltpu.VMEM((2, Bm, Dh), u32), pltpu.VMEM((2, Bm, Ft), bf16),
                            pltpu.SemaphoreType.DMA((2,)),
                            pltpu.SemaphoreType.DMA((2,)), pltpu.SemaphoreType.DMA((2,))],
        ),
        out_shape=jax.ShapeDtypeStruct((P, D_EXPERT), bf16),
        compiler_params=_params(("arbitrary", "arbitrary"), est),
        name="moe_expert_hidden",
    )(blk0, nblk, w1, w3, xs)


def _moe_out_body(blk0_ref, nblk_ref, w2_hbm, hid_hbm, y_hbm, wf_ref, w2b_ref, ibuf, obuf,
                  wsem, isem, osem):
    n = pl.program_id(0)
    e = pl.program_id(1)
    n_blocks = nblk_ref[e]
    stream = _WeightStream((w2_hbm,), wf_ref, wsem, MOE_N_TILE)

    def cast(step):
        _cast_tile(stream.tile(0, step), w2b_ref)

    def compute(h):
        y = jnp.dot(h, w2b_ref[...], preferred_element_type=f32)
        half = y.shape[1] // 2
        return _pack_bf16_pair(y[:, :half].astype(bf16), y[:, half:].astype(bf16))

    _expert_step(stream, n_blocks, cast, functools.partial(
        _expert_block_loop, blk0_ref[e] * MOE_ROWS, n_blocks, hid_hbm, y_hbm, n * (MOE_N_TILE // 2),
        ibuf, obuf, isem, osem, compute))


def _moe_out(hid, w2, blk0, nblk):
    P, F = hid.shape
    D = w2.shape[2]
    Bm, Nt = MOE_ROWS, MOE_N_TILE
    assert Nt == D, "the packed output pairs feature j with j + D/2: the column tile must span all of D"
    est = 2 * F * Nt * 4 + F * Nt * 2 + 2 * Bm * F * 2 + 2 * Bm * Nt * 2 + 2 * Bm * Nt * 4
    return pl.pallas_call(
        _moe_out_body,
        grid_spec=pltpu.PrefetchScalarGridSpec(
            num_scalar_prefetch=2,
            grid=(D // Nt, N_EXPERTS),
            in_specs=[pl.BlockSpec(memory_space=pl.ANY)] * 2,
            out_specs=pl.BlockSpec(memory_space=pl.ANY),
            scratch_shapes=[pltpu.VMEM((2, 1, F, Nt), f32), pltpu.VMEM((F, Nt), bf16),
                            pltpu.VMEM((2, Bm, F), bf16), pltpu.VMEM((2, Bm, Nt // 2), u32),
                            pltpu.SemaphoreType.DMA((2,)),
                            pltpu.SemaphoreType.DMA((2,)), pltpu.SemaphoreType.DMA((2,))],
        ),
        out_shape=jax.ShapeDtypeStruct((P, D // 2), u32),
        compiler_params=_params(("arbitrary", "arbitrary"), est),
        name="moe_expert_out",
    )(blk0, nblk, w2, hid)


def _combine_body(dest_ref, h_ref, gate_ref, g_ref, y_hbm, o_ref, buf, ss_s, sem):
    R = GATHER_ROWS
    i = pl.program_id(0)
    n_steps = pl.num_programs(0)

    def issue_rows(step, slot, r0, n):
        for r in range(n):
            for k in range(TOP_K):
                _row_copy(y_hbm, dest_ref[(step * R + r0 + r) * TOP_K + k],
                          buf.at[slot, k, pl.ds(r0 + r, 1)], sem.at[slot]).start()

    @pl.when(i == 0)
    def _():
        lax.fori_loop(0, R // V7X_SUBLANES,
                      lambda c, carry: (issue_rows(0, 0, c * V7X_SUBLANES, V7X_SUBLANES), carry)[1], 0)

    slot = i % 2
    for k in range(TOP_K):
        pltpu.make_async_copy(y_hbm.at[pl.ds(0, R)], buf.at[slot, k], sem.at[slot]).wait()
    g_fin = g_ref[...]
    D = h_ref.shape[1]
    RC = V7X_SUBLANES

    def chunk(c):
        return pl.ds(pl.multiple_of(c * RC, RC), RC)

    def passes(prefetch):
        def accumulate(c, carry):
            rs = chunk(c)
            gate = gate_ref[rs, :]
            g1 = gate[:, ROUTE_GATE_LANE:ROUTE_GATE_LANE + 1]
            g2 = gate[:, ROUTE_GATE_LANE + 1:ROUTE_GATE_LANE + 2]
            y1_lo, y1_hi = _unpack_pair_f32(buf[slot, 0, rs, :])
            y2_lo, y2_hi = _unpack_pair_f32(buf[slot, 1, rs, :])
            ss = None
            for cols, y1, y2 in ((slice(0, D // 2), y1_lo, y2_lo), (slice(D // 2, D), y1_hi, y2_hi)):
                h = h_ref[rs, cols] + y1 * g1 + y2 * g2
                o_ref[rs, cols] = h
                sq = h * h
                for j in range(sq.shape[1] // V7X_LANES):
                    part = sq[:, j * V7X_LANES:(j + 1) * V7X_LANES]
                    ss = part if ss is None else ss + part
            ss_s[rs, :] = ss
            if prefetch:
                issue_rows(i + 1, 1 - slot, c * RC, RC // 2)
            return carry

        lax.fori_loop(0, R // RC, accumulate, 0, unroll=4)

        def normalize(c, carry):
            rs = chunk(c)
            ms = jnp.sum(ss_s[rs, :], axis=-1, keepdims=True) * (1.0 / D)
            o_ref[rs, :] = o_ref[rs, :] * lax.rsqrt(ms + EPS) * g_fin
            if prefetch:
                issue_rows(i + 1, 1 - slot, c * RC + RC // 2, RC // 2)
            return carry

        lax.fori_loop(0, R // RC, normalize, 0, unroll=4)

    @pl.when(i + 1 < n_steps)
    def _():
        passes(True)

    @pl.when(i + 1 == n_steps)
    def _():
        passes(False)


def _moe_combine(h, y, dest, gate, n_final):
    T, D = h.shape
    R = GATHER_ROWS
    est = 2 * TOP_K * R * D * 4 + 4 * R * D * 4 + 3 * R * D * 4
    return pl.pallas_call(
        _combine_body,
        grid_spec=pltpu.PrefetchScalarGridSpec(
            num_scalar_prefetch=1,
            grid=(T // R,),
            in_specs=[pl.BlockSpec((R, D), lambda i, dest: (i, 0)),
                      pl.BlockSpec((R, V7X_LANES), lambda i, dest: (i, 0)),
                      pl.BlockSpec((1, D), lambda i, dest: (0, 0)),
                      pl.BlockSpec(memory_space=pl.ANY)],
            out_specs=pl.BlockSpec((R, D), lambda i, dest: (i, 0)),
            scratch_shapes=[pltpu.VMEM((2, TOP_K, R, D // 2), u32), pltpu.VMEM((R, V7X_LANES), f32),
                            pltpu.SemaphoreType.DMA((2,))],
        ),
        out_shape=jax.ShapeDtypeStruct((T, D), f32),
        compiler_params=_params(("arbitrary",), est),
        name="moe_combine_norm",
    )(dest, h, gate, n_final.reshape(1, D).astype(f32), y)


def _slot_layout(rec):
    T = rec.shape[0]
    A = T * TOP_K
    eid = rec[:, ROUTE_EID_LANE:ROUTE_EID_LANE + TOP_K].astype(jnp.int32).reshape(A)
    onehot = (eid[:, None] == jnp.arange(N_EXPERTS, dtype=jnp.int32)[None, :]).astype(jnp.int32)
    csum = jnp.cumsum(onehot, axis=0)
    rank = jnp.sum(onehot * csum, axis=1) - 1
    counts = csum[-1]
    pcounts = (counts + MOE_ROWS - 1) // MOE_ROWS * MOE_ROWS
    pend = jnp.cumsum(pcounts)
    pstart = pend - pcounts
    dest = (pstart[eid] + rank).astype(jnp.int32)
    n_blocks = -(-(A + N_EXPERTS * (MOE_ROWS - 1)) // MOE_ROWS)
    P = n_blocks * MOE_ROWS
    blk0 = (pstart // MOE_ROWS).astype(jnp.int32)
    nblk = (pcounts // MOE_ROWS).astype(jnp.int32)
    pad_end = pend.astype(jnp.int32)
    pad_len = (pcounts - counts).astype(jnp.int32)
    return dest, blk0, nblk, pad_end, pad_len, P


def kernel(x, mem, n_mix, w_in, gmlp_ln_g, gmlp_ln_b, gmlp_w_s, gmlp_b_s, hgrn_lower_bounds,
           hgrn_norm_g, w_out, n_cross, n_mem, w_q_x, w_k_x, w_v_x, w_o_x, n_moe, w_group,
           b_group, w_router, b_router, w1_e, w3_e, w2_e, n_final):
    B, S, D = x.shape
    n_mem_tok = mem.shape[1]
    T = B * S
    xt = x.reshape(T, D)

    hn = _rmsnorm(xt, n_mix[0], bf16)
    z = _matmul(hn, w_in[0], bf16, 1024, 512, "mm_w_in")
    ya = _gmlp(z, gmlp_ln_g[0], gmlp_ln_b[0], gmlp_w_s[0].astype(f32), gmlp_b_s[0].astype(f32))
    yb = _hgrn(z, hgrn_lower_bounds, hgrn_norm_g[0], B, S)
    h1, h1b, h1ss = _matmul_res(xt, [ya, yb], w_out[0], 1024, 512, "mm_w_out", with_norm_inputs=True)

    mn = _rmsnorm(mem.reshape(B * n_mem_tok, D), n_mem[0], bf16)
    kx = _matmul(mn, w_k_x[0], bf16, B * n_mem_tok, 512, "mm_w_k")
    vx = _matmul(mn, w_v_x[0], bf16, B * n_mem_tok, 512, "mm_w_v")
    wq_gained = (w_q_x[0] * n_cross[0].astype(f32)[:, None]).astype(bf16)
    ox = _xattn(h1b, h1ss, wq_gained, kx, vx, B, S, n_mem_tok)
    h2 = _matmul_res(h1, [ox], w_o_x[0], 1024, 512, "mm_w_o")

    xm, rec = _router(h2, n_moe[0], w_group[0], b_group[0], w_router[0], b_router[0])
    dest, blk0, nblk, pad_end, pad_len, P = _slot_layout(rec)
    xs = _moe_dispatch(xm, dest, pad_end, pad_len, P)
    hid = _moe_hidden(xs, w1_e[0], w3_e[0], blk0, nblk)
    y = _moe_out(hid, w2_e[0], blk0, nblk)
    out = _moe_combine(h2, y, dest, rec, n_final)
    return out.reshape(B, S, D)
```

```python
import functools

import jax
import jax.numpy as jnp
from jax import lax
from jax.experimental import pallas as pl
from jax.experimental.pallas import tpu as pltpu

f32 = jnp.float32
bf16 = jnp.bfloat16

EPS = 1e-6
D_MODEL = 4096
CHUNK = 64
GMLP_HEADS = 4
GMLP_HEAD_DIM = 512
GMLP_BLOCK = 128
D_GMLP = GMLP_HEADS * GMLP_HEAD_DIM
HGRN_HEAD_DIM = 128
HGRN_HEADS = 16
D_HGRN = HGRN_HEADS * HGRN_HEAD_DIM
X_HEADS = 4
X_HEAD_DIM = 1024
N_GROUPS = 4
EXPERTS_PER_GROUP = 8
N_EXPERTS = N_GROUPS * EXPERTS_PER_GROUP
TOP_K = 2
D_EXPERT = 1024

V7X_LANES = 128
V7X_SUBLANES = 8
V7X_VMEM_LIMIT_BYTES = 60000 * 1024

MOE_ROWS = 256
MOE_F_TILE = 512
MOE_N_TILE = 4096
MOE_W_CHUNKS = 8
GATHER_ROWS = 256
HGRN_GROUP = 4
HGRN_ROWS = 512
HGRN_SUB = 128


def _params(sem, est_bytes):
    limit = min(int(est_bytes) + (8 << 20), V7X_VMEM_LIMIT_BYTES)
    return pltpu.CompilerParams(dimension_semantics=sem, vmem_limit_bytes=limit)


def _rms_body(x_ref, g_ref, o_ref):
    x = x_ref[...].astype(f32)
    ms = jnp.mean(x * x, axis=-1, keepdims=True)
    o_ref[...] = (x * lax.rsqrt(ms + EPS) * g_ref[...]).astype(o_ref.dtype)


def _rmsnorm(x, g, out_dtype, tm=256):
    M, D = x.shape
    est = 2 * tm * D * (x.dtype.itemsize + jnp.dtype(out_dtype).itemsize) + 2 * tm * D * 4
    return pl.pallas_call(
        _rms_body,
        grid=(M // tm,),
        in_specs=[pl.BlockSpec((tm, D), lambda i: (i, 0)),
                  pl.BlockSpec((1, D), lambda i: (0, 0))],
        out_specs=pl.BlockSpec((tm, D), lambda i: (i, 0)),
        out_shape=jax.ShapeDtypeStruct((M, D), out_dtype),
        compiler_params=_params(("arbitrary",), est),
        name="rmsnorm",
    )(x, g.reshape(1, D).astype(f32))


CAST_ROWS = 512


def _cast_tile(src_ref, dst_ref, dst_row0=0):
    rows = min(CAST_ROWS, src_ref.shape[0])
    n = src_ref.shape[0] // rows

    def piece(p, c):
        r = pl.multiple_of(p * rows, rows)
        d = pl.multiple_of(dst_row0 + p * rows, rows)
        dst_ref[pl.ds(d, rows), :] = src_ref[pl.ds(r, rows), :].astype(bf16)
        return c

    lax.fori_loop(0, n, piece, 0)


def _mm_body(a_ref, w_ref, o_ref, wb_ref):
    @pl.when(pl.program_id(1) == 0)
    def _():
        _cast_tile(w_ref, wb_ref)

    o_ref[...] = jnp.dot(a_ref[...], wb_ref[...], preferred_element_type=f32).astype(o_ref.dtype)


def _matmul(a, w, out_dtype, tm, tn, name):
    M, K = a.shape
    N = w.shape[1]
    osz = jnp.dtype(out_dtype).itemsize
    est = 2 * (tm * K * 2 + K * tn * 4 + tm * tn * osz) + K * tn * 2 + tm * tn * 4
    return pl.pallas_call(
        _mm_body,
        grid=(N // tn, M // tm),
        in_specs=[pl.BlockSpec((tm, K), lambda j, i: (i, 0)),
                  pl.BlockSpec((K, tn), lambda j, i: (0, j))],
        out_specs=pl.BlockSpec((tm, tn), lambda j, i: (i, j)),
        out_shape=jax.ShapeDtypeStruct((M, N), out_dtype),
        scratch_shapes=[pltpu.VMEM((K, tn), bf16)],
        compiler_params=_params(("arbitrary", "arbitrary"), est),
        name=name,
    )(a, w)


def _mm_res_body(n_parts, with_norm_inputs, r_ref, *refs):
    a_refs = refs[:n_parts]
    w_refs = refs[n_parts:2 * n_parts]
    out_refs = refs[2 * n_parts:-1]
    wb_ref = refs[-1]
    kp = w_refs[0].shape[0]

    @pl.when(pl.program_id(1) == 0)
    def _():
        for p in range(n_parts):
            _cast_tile(w_refs[p], wb_ref, p * kp)

    acc = r_ref[...]
    for p in range(n_parts):
        acc = acc + jnp.dot(a_refs[p][...], wb_ref[p * kp:(p + 1) * kp, :], preferred_element_type=f32)
    out_refs[0][...] = acc
    if with_norm_inputs:
        out_refs[1][...] = acc.astype(bf16)
        sq = acc * acc
        ss = sq[:, 0:V7X_LANES]
        for g in range(1, acc.shape[1] // V7X_LANES):
            ss = ss + sq[:, g * V7X_LANES:(g + 1) * V7X_LANES]
        out_refs[2][...] = ss


def _matmul_res(res, a_list, w, tm, tn, name, with_norm_inputs=False):
    M, N = res.shape
    n = len(a_list)
    Kp = a_list[0].shape[1]
    a_specs = [pl.BlockSpec((tm, Kp), lambda j, i: (i, 0)) for _ in range(n)]
    w_specs = [pl.BlockSpec((Kp, tn), functools.partial(lambda j, i, p: (p, j), p=p)) for p in range(n)]
    est = 2 * (n * tm * Kp * 2 + n * Kp * tn * 4 + 2 * tm * tn * 4) + n * Kp * tn * 2 + 2 * tm * tn * 4
    tile = pl.BlockSpec((tm, tn), lambda j, i: (i, j))
    out_specs, out_shape = [tile], [jax.ShapeDtypeStruct((M, N), f32)]
    if with_norm_inputs:
        est += 2 * tm * tn * 2 + 2 * tm * V7X_LANES * 4 + tm * tn * 4
        out_specs += [tile, pl.BlockSpec((tm, V7X_LANES), lambda j, i: (i, j))]
        out_shape += [jax.ShapeDtypeStruct((M, N), bf16),
                      jax.ShapeDtypeStruct((M, N // tn * V7X_LANES), f32)]
    outs = pl.pallas_call(
        functools.partial(_mm_res_body, n, with_norm_inputs),
        grid=(N // tn, M // tm),
        in_specs=[tile] + a_specs + w_specs,
        out_specs=out_specs,
        out_shape=out_shape,
        scratch_shapes=[pltpu.VMEM((n * Kp, tn), bf16)],
        compiler_params=_params(("arbitrary", "arbitrary"), est),
        name=name,
    )(res, *a_list, *([w] * n))
    return outs if with_norm_inputs else outs[0]


def _gelu(x):
    return 0.5 * x * (1.0 + lax.erf(x * 0.7071067811865476))


def _gmlp_body(u_ref, v_ref, lng_ref, lnb_ref, ws_ref, bs_ref, o_ref):
    rows = u_ref.shape[0]
    gv = _gelu(v_ref[...].astype(f32))
    mu = jnp.mean(gv, axis=-1, keepdims=True)
    xc = gv - mu
    var = jnp.mean(xc * xc, axis=-1, keepdims=True)
    vn = (xc * lax.rsqrt(var + EPS) * lng_ref[...] + lnb_ref[...]).astype(bf16)
    ci = lax.broadcasted_iota(jnp.int32, (GMLP_BLOCK, GMLP_BLOCK), 0) // CHUNK
    cj = lax.broadcasted_iota(jnp.int32, (GMLP_BLOCK, GMLP_BLOCK), 1) // CHUNK
    w = jnp.where(cj <= ci, ws_ref[0], 0.0).astype(bf16)
    bias = bs_ref[0]
    for blk in range(rows // GMLP_BLOCK):
        sl = slice(blk * GMLP_BLOCK, (blk + 1) * GMLP_BLOCK)
        s = jnp.dot(w, vn[sl], preferred_element_type=f32) + bias
        o_ref[sl, :] = (_gelu(u_ref[sl, :].astype(f32)) * s).astype(o_ref.dtype)


def _gmlp(z, ln_g, ln_b, w_s, b_s, tr=1024):
    M = z.shape[0]
    C = GMLP_HEAD_DIM
    est = 2 * 3 * tr * C * 2 + 8 * tr * C * 4
    return pl.pallas_call(
        _gmlp_body,
        grid=(M // tr, GMLP_HEADS),
        in_specs=[pl.BlockSpec((tr, C), lambda i, h: (i, h)),
                  pl.BlockSpec((tr, C), lambda i, h: (i, GMLP_HEADS + h)),
                  pl.BlockSpec((1, C), lambda i, h: (0, h)),
                  pl.BlockSpec((1, C), lambda i, h: (0, h)),
                  pl.BlockSpec((1, GMLP_BLOCK, GMLP_BLOCK), lambda i, h: (h, 0, 0)),
                  pl.BlockSpec((1, GMLP_BLOCK, 1), lambda i, h: (h, 0, 0))],
        out_specs=pl.BlockSpec((tr, C), lambda i, h: (i, h)),
        out_shape=jax.ShapeDtypeStruct((M, D_GMLP), bf16),
        compiler_params=_params(("arbitrary", "arbitrary"), est),
        name="gmlp_mixer",
    )(z, z, ln_g.reshape(1, D_GMLP), ln_b.reshape(1, D_GMLP), w_s,
      b_s.reshape(GMLP_HEADS, GMLP_BLOCK, 1))


def _split3(x):
    hi = x.astype(bf16)
    r1 = x - hi.astype(f32)
    mid = r1.astype(bf16)
    lo = (r1 - mid.astype(f32)).astype(bf16)
    return hi, mid, lo


HGRN_STRIP = 2 * CHUNK
_NT = (((1,), (1,)), ((), ()))
_TN = (((0,), (0,)), ((), ()))


def _split2(x):
    hi = x.astype(bf16)
    mid = (x - hi.astype(f32)).astype(bf16)
    return hi, mid


def _hgrn_body(q_ref, f_ref, i_ref, g_ref, lbr_ref, ng_ref, o_ref,
               st_ref, qin_s, kin_s, qex_s, kdec_s, dec_s, am_s, xq_s, xk_s, inc_s, smat_s, oacc_s):
    C, K, G = CHUNK, HGRN_HEAD_DIM, HGRN_GROUP
    rows, W = q_ref.shape
    n_chunks = rows // C

    @pl.when(pl.program_id(2) == 0)
    def _():
        st_ref[...] = jnp.zeros_like(st_ref)

    r0 = lbr_ref[0:1, :]
    r1 = lbr_ref[1:2, :]
    m = jnp.maximum(r0, r1)
    e0 = jnp.exp(r0 - m)
    e1 = jnp.exp(r1 - m)
    lb = e0 / (e0 + e1)

    ti = lax.broadcasted_iota(jnp.int32, (C, C), 0)
    si = lax.broadcasted_iota(jnp.int32, (C, C), 1)
    tril = (si <= ti).astype(bf16)

    def chunk_rows(c):
        return pl.ds(pl.multiple_of(c * C, C), C)

    def gates(c, carry):
        rs = chunk_rows(c)
        fg = lb + (1.0 - lb) * jax.nn.sigmoid(f_ref[rs, :].astype(f32))
        hi, mid, lo = _split3(jnp.log(fg))
        b = (jnp.dot(tril, hi, preferred_element_type=f32)
             + jnp.dot(tril, mid, preferred_element_type=f32)
             + jnp.dot(tril, lo, preferred_element_type=f32))
        b_mid = b[C // 2:C // 2 + 1, :]
        b_last = b[C - 1:C, :]
        q_in = jax.nn.silu(q_ref[rs, :].astype(f32)) * jnp.exp(b - b_mid)
        k_in = (1.0 - fg) * jnp.exp(b_mid - b)
        qin_s[rs, :] = q_in.astype(bf16)
        kin_s[rs, :] = k_in.astype(bf16)
        qex_s[rs, :] = (q_in * jnp.exp(b_mid)).astype(bf16)
        kdec_s[rs, :] = (k_in * jnp.exp(b_last - b_mid)).astype(bf16)
        dec_s[pl.ds(pl.multiple_of(c * 8, 8), 8), :] = jnp.broadcast_to(jnp.exp(b_last), (8, W))
        return carry

    lax.fori_loop(0, n_chunks, gates, 0, unroll=2)

    @pl.when((pl.program_id(0) == 0) & (pl.program_id(1) == 0) & (pl.program_id(2) == 0))
    def _():
        am_s[...] = jnp.zeros_like(am_s)
        xq_s[...] = jnp.zeros_like(xq_s)
        xk_s[...] = jnp.zeros_like(xk_s)

    S = HGRN_STRIP
    tt = lax.broadcasted_iota(jnp.int32, (S, S), 0)
    ss = lax.broadcasted_iota(jnp.int32, (S, S), 1)
    keep = jnp.logical_and(ss <= tt, (ss // C) == (tt // C))

    SUB = HGRN_SUB
    cps = SUB // C
    lanes = [slice(h * K, (h + 1) * K) for h in range(G)]
    subs = [slice(u * SUB, (u + 1) * SUB) for u in range(rows // SUB)]
    for h, ls in enumerate(lanes):
        for c in range(n_chunks):
            cb = slice((c % cps) * K, (c % cps + 1) * K)
            xq_s[h, c * C:(c + 1) * C, cb] = qex_s[c * C:(c + 1) * C, ls]
            xk_s[h, c * C:(c + 1) * C, cb] = kdec_s[c * C:(c + 1) * C, ls]

    scores = [[lax.dot_general(qin_s[rs, ls], kin_s[rs, ls], _NT, preferred_element_type=f32) for rs in subs]
              for ls in lanes]
    for h, ls in enumerate(lanes):
        for u, rs in enumerate(subs):
            inc_s[h, :, u * cps * K:(u + 1) * cps * K] = lax.dot_general(
                i_ref[rs, ls], xk_s[h, rs, :], _TN, preferred_element_type=f32)
    for h in range(G):
        for u in range(len(subs)):
            for r in range(SUB // S):
                sl = slice(r * S, (r + 1) * S)
                am_s[h, u, sl, sl] = jnp.where(keep, scores[h][u][sl, sl], 0.0).astype(bf16)

    for h, ls in enumerate(lanes):
        st = st_ref[h]
        for c in range(n_chunks):
            cs = slice(c * K, (c + 1) * K)
            smat_s[h, :, cs] = st.astype(bf16)
            st = st * dec_s[c * 8:c * 8 + 1, ls] + inc_s[h, :, cs]
        st_ref[h] = st

    for h, ls in enumerate(lanes):
        for u, rs in enumerate(subs):
            oacc_s[rs, ls] = (
                jnp.dot(am_s[h, u], i_ref[rs, ls], preferred_element_type=f32)
                + lax.dot_general(xq_s[h, rs, :], smat_s[h, :, u * cps * K:(u + 1) * cps * K], _NT,
                                  preferred_element_type=f32))

    ng = ng_ref[...]

    def outputs(c, carry):
        rs = chunk_rows(c)
        for h in range(G):
            ls = slice(h * K, (h + 1) * K)
            o = oacc_s[rs, ls]
            ms = jnp.mean(o * o, axis=-1, keepdims=True)
            o = o * lax.rsqrt(ms + EPS) * ng[:, ls]
            o = o * jax.nn.silu(g_ref[rs, ls].astype(f32))
            o_ref[rs, ls] = o.astype(o_ref.dtype)
        return carry

    lax.fori_loop(0, n_chunks, outputs, 0, unroll=2)


def _hgrn(z, lower_bounds, norm_g, batch, seq):
    M = z.shape[0]
    W = HGRN_GROUP * HGRN_HEAD_DIM
    ts = HGRN_ROWS
    n_s = seq // ts
    n_hg = D_HGRN // W
    col0 = 2 * D_GMLP // W

    def zspec(part):
        return pl.BlockSpec((ts, W), lambda b, hg, s, part=part: (b * n_s + s, col0 + part * n_hg + hg))

    G, K, SUB = HGRN_GROUP, HGRN_HEAD_DIM, HGRN_SUB
    n_chunks = ts // CHUNK
    est = (2 * 5 * ts * W * 2 + 4 * ts * W * 2 + ts * W * 4 + G * ts * ts * 2 + 2 * G * ts * n_chunks * K * 2
           + G * K * n_chunks * K * 6 + G * K * K * 4 + (G + 2) * ts * ts * 4)
    return pl.pallas_call(
        _hgrn_body,
        grid=(batch, n_hg, n_s),
        in_specs=[zspec(0), zspec(1), zspec(2), zspec(3),
                  pl.BlockSpec((2, W), lambda b, hg, s: (0, hg)),
                  pl.BlockSpec((1, W), lambda b, hg, s: (0, hg))],
        out_specs=pl.BlockSpec((ts, W), lambda b, hg, s: (b * n_s + s, hg)),
        out_shape=jax.ShapeDtypeStruct((M, D_HGRN), bf16),
        scratch_shapes=[pltpu.VMEM((G, K, K), f32),
                        pltpu.VMEM((ts, W), bf16),
                        pltpu.VMEM((ts, W), bf16),
                        pltpu.VMEM((ts, W), bf16),
                        pltpu.VMEM((ts, W), bf16),
                        pltpu.VMEM((n_chunks * 8, W), f32),
                        pltpu.VMEM((G, ts // SUB, SUB, SUB), bf16),
                        pltpu.VMEM((G, ts, SUB // CHUNK * K), bf16),
                        pltpu.VMEM((G, ts, SUB // CHUNK * K), bf16),
                        pltpu.VMEM((G, K, n_chunks * K), f32),
                        pltpu.VMEM((G, K, n_chunks * K), bf16),
                        pltpu.VMEM((ts, W), f32)],
        compiler_params=_params(("arbitrary", "arbitrary", "arbitrary"), est),
        name="hgrn2_mixer",
    )(z, z, z, z, lower_bounds.astype(f32), norm_g.reshape(1, D_HGRN).astype(f32))


def _xattn_body(h_ref, ss_ref, wq_ref, k_ref, v_ref, o_ref):
    D = h_ref.shape[1]
    inv_rms = lax.rsqrt(jnp.sum(ss_ref[...], axis=-1, keepdims=True) * (1.0 / D) + EPS)
    q = (jnp.dot(h_ref[...], wq_ref[...], preferred_element_type=f32) * inv_rms).astype(bf16)
    s = lax.dot_general(q, k_ref[...], (((1,), (1,)), ((), ())), preferred_element_type=f32)
    s = s * (X_HEAD_DIM ** -0.5)
    s = s - jnp.max(s, axis=-1, keepdims=True)
    p = jnp.exp(s)
    p = p / jnp.sum(p, axis=-1, keepdims=True)
    o_ref[...] = jnp.dot(p.astype(bf16), v_ref[...], preferred_element_type=f32).astype(o_ref.dtype)


def _xattn(hb, ss, wq, k, v, batch, seq, n_mem, tq=1024):
    M, D = hb.shape
    n_s = seq // tq
    dh = X_HEAD_DIM
    est = (2 * (tq * D * 2 + tq * ss.shape[1] * 4 + D * dh * 2 + 2 * n_mem * dh * 2 + tq * dh * 2)
           + 3 * tq * dh * 4)
    return pl.pallas_call(
        _xattn_body,
        grid=(X_HEADS, batch, n_s),
        in_specs=[pl.BlockSpec((tq, D), lambda h, b, s: (b * n_s + s, 0)),
                  pl.BlockSpec((tq, ss.shape[1]), lambda h, b, s: (b * n_s + s, 0)),
                  pl.BlockSpec((D, dh), lambda h, b, s: (0, h)),
                  pl.BlockSpec((n_mem, dh), lambda h, b, s: (b, h)),
                  pl.BlockSpec((n_mem, dh), lambda h, b, s: (b, h))],
        out_specs=pl.BlockSpec((tq, dh), lambda h, b, s: (b * n_s + s, h)),
        out_shape=jax.ShapeDtypeStruct((M, D), bf16),
        compiler_params=_params(("arbitrary", "arbitrary", "arbitrary"), est),
        name="xattn_qproj",
    )(hb, ss, wq, k, v)


u32 = jnp.uint32
ROUTE_GATE_LANE = 0
ROUTE_EID_LANE = 2


def _pack_bf16_pair(lo, hi):
    lo32 = lax.bitcast_convert_type(lo.astype(f32), u32) >> 16
    hi32 = lax.bitcast_convert_type(hi.astype(f32), u32) & jnp.uint32(0xFFFF0000)
    return lo32 | hi32


def _unpack_bf16_pair(w):
    lo = lax.bitcast_convert_type(w << 16, f32).astype(bf16)
    hi = lax.bitcast_convert_type(w & jnp.uint32(0xFFFF0000), f32).astype(bf16)
    return lo, hi


def _unpack_pair_f32(w):
    return (lax.bitcast_convert_type(w << 16, f32),
            lax.bitcast_convert_type(w & jnp.uint32(0xFFFF0000), f32))


def _first_lane_of(mask, lane):
    return jnp.min(jnp.where(mask, lane, V7X_LANES), axis=-1, keepdims=True)


def _router_body(h_ref, g_ref, wr_ref, br_ref, xm_ref, r_ref):
    x = h_ref[...]
    D = x.shape[1]
    ms = jnp.mean(x * x, axis=-1, keepdims=True)
    xn = x * lax.rsqrt(ms + EPS) * g_ref[...]
    xh, xm = _split2(xn)
    wh, wm = _split2(wr_ref[...])
    dot = functools.partial(jnp.dot, preferred_element_type=f32)
    acc = dot(xm, wh) + dot(xh, wm)
    acc = acc + dot(xh, wh)
    logits = acc + br_ref[...]

    xm_ref[...] = _pack_bf16_pair(xh[:, :D // 2], xh[:, D // 2:])

    lane = lax.broadcasted_iota(jnp.int32, logits.shape, 1)
    neg = jnp.float32(-jnp.inf)
    is_g = lane < N_GROUPS
    gl = jnp.where(is_g, logits, neg)
    gmax = jnp.max(gl, axis=-1, keepdims=True)
    grp = _first_lane_of(gl == gmax, lane)
    p_grp = 1.0 / jnp.sum(jnp.where(is_g, jnp.exp(logits - gmax), 0.0), axis=-1, keepdims=True)
    e0 = N_GROUPS + grp * EXPERTS_PER_GROUP
    el = jnp.where(jnp.logical_and(lane >= e0, lane < e0 + EXPERTS_PER_GROUP), logits, neg)
    m1 = jnp.max(el, axis=-1, keepdims=True)
    i1 = _first_lane_of(el == m1, lane)
    el2 = jnp.where(lane == i1, neg, el)
    m2 = jnp.max(el2, axis=-1, keepdims=True)
    i2 = _first_lane_of(el2 == m2, lane)
    t = jnp.exp(m2 - m1)
    g1 = p_grp * (1.0 / (1.0 + t))
    g2 = p_grp * (t / (1.0 + t))
    rec = jnp.where(lane == ROUTE_GATE_LANE, g1, 0.0)
    rec = jnp.where(lane == ROUTE_GATE_LANE + 1, g2, rec)
    rec = jnp.where(lane == ROUTE_EID_LANE, (i1 - N_GROUPS).astype(f32), rec)
    rec = jnp.where(lane == ROUTE_EID_LANE + 1, (i2 - N_GROUPS).astype(f32), rec)
    r_ref[...] = rec


def _router(h, n_moe, w_group, b_group, w_router, b_router, tm=256):
    M, D = h.shape
    n_log = N_GROUPS + N_EXPERTS
    wr = jnp.concatenate([w_group] + [w_router[g] for g in range(N_GROUPS)], axis=1)
    wr = jnp.pad(wr.astype(f32), ((0, 0), (0, V7X_LANES - n_log)))
    br = jnp.pad(jnp.concatenate([b_group, b_router.reshape(-1)]).astype(f32), (0, V7X_LANES - n_log))
    est = 2 * tm * D * 4 + 2 * D * V7X_LANES * 4 + 2 * tm * D * 2 + 6 * tm * D * 4
    return pl.pallas_call(
        _router_body,
        grid=(M // tm,),
        in_specs=[pl.BlockSpec((tm, D), lambda i: (i, 0)),
                  pl.BlockSpec((1, D), lambda i: (0, 0)),
                  pl.BlockSpec((D, V7X_LANES), lambda i: (0, 0)),
                  pl.BlockSpec((1, V7X_LANES), lambda i: (0, 0))],
        out_specs=[pl.BlockSpec((tm, D // 2), lambda i: (i, 0)),
                   pl.BlockSpec((tm, V7X_LANES), lambda i: (i, 0))],
        out_shape=[jax.ShapeDtypeStruct((M, D // 2), u32),
                   jax.ShapeDtypeStruct((M, V7X_LANES), f32)],
        compiler_params=_params(("arbitrary",), est),
        name="moe_router",
    )(h, n_moe.reshape(1, D).astype(f32), wr, br.reshape(1, V7X_LANES))


def _row_copy(src_hbm, row, dst, sem):
    return pltpu.make_async_copy(src_hbm.at[pl.ds(row, 1)], dst, sem)


PAD_BITS = tuple(1 << k for k in reversed(range((MOE_ROWS - 1).bit_length())))


def _dispatch_body(dest_ref, padend_ref, padlen_ref, xm_ref, xs_hbm, buf, zbuf, sem, zsem):
    R = GATHER_ROWS
    i = pl.program_id(0)
    n_steps = pl.num_programs(0)
    slot = i % 2

    def wait_slot(s):
        for _ in range(TOP_K):
            pltpu.make_async_copy(buf.at[s], xs_hbm.at[pl.ds(0, R)], sem.at[s]).wait()

    def pad_copies(fn):
        def per_expert(e, c):
            ln = padlen_ref[e]
            end = padend_ref[e]
            for bit in PAD_BITS:
                rows_per_copy = bit if bit >= V7X_SUBLANES else 1
                for j in range(bit // rows_per_copy):
                    @pl.when((ln & bit) != 0)
                    def _(bit=bit, end=end, j=j, n=rows_per_copy):
                        start = end - bit + j * n
                        if n > 1:
                            start = pl.multiple_of(start, V7X_SUBLANES)
                        fn(pltpu.make_async_copy(zbuf.at[pl.ds(0, n)], xs_hbm.at[pl.ds(start, n)], zsem))
                end = end - (ln & bit)
            return c
        lax.fori_loop(0, N_EXPERTS, per_expert, 0)

    @pl.when(i == 0)
    def _():
        zbuf[...] = jnp.zeros_like(zbuf)
        pad_copies(lambda cp: cp.start())

    @pl.when(i >= 2)
    def _():
        wait_slot(slot)

    buf[slot] = xm_ref[...]

    def issue(r, c):
        for k in range(TOP_K):
            d = dest_ref[(i * R + r) * TOP_K + k]
            pltpu.make_async_copy(buf.at[slot, pl.ds(r, 1)], xs_hbm.at[pl.ds(d, 1)], sem.at[slot]).start()
        return c

    lax.fori_loop(0, R, issue, 0, unroll=8)

    @pl.when(i == n_steps - 1)
    def _():
        if n_steps >= 2:
            wait_slot(1 - slot)
        wait_slot(slot)
        pad_copies(lambda cp: cp.wait())


def _moe_dispatch(xm, dest, pad_end, pad_len, P):
    T, Dh = xm.shape
    R = GATHER_ROWS
    est = 2 * R * Dh * 4 + 2 * R * Dh * 4 + PAD_BITS[0] * Dh * 4
    return pl.pallas_call(
        _dispatch_body,
        grid_spec=pltpu.PrefetchScalarGridSpec(
            num_scalar_prefetch=3,
            grid=(T // R,),
            in_specs=[pl.BlockSpec((R, Dh), lambda i, dest, pend, plen: (i, 0))],
            out_specs=pl.BlockSpec(memory_space=pl.ANY),
            scratch_shapes=[pltpu.VMEM((2, R, Dh), u32), pltpu.VMEM((PAD_BITS[0], Dh), u32),
                            pltpu.SemaphoreType.DMA((2,)), pltpu.SemaphoreType.DMA],
        ),
        out_shape=jax.ShapeDtypeStruct((P, Dh), u32),
        compiler_params=_params(("arbitrary",), est),
        name="moe_dispatch",
    )(dest, pad_end, pad_len, xm)


class _WeightStream:
    def __init__(self, w_hbms, wf_ref, wsem, col_tile):
        self.w_hbms, self.wf, self.wsem, self.ct = w_hbms, wf_ref, wsem, col_tile
        self.rows = wf_ref.shape[2]
        self.cr = self.rows // MOE_W_CHUNKS

    def _chunk(self, m, c, step, slot):
        e = step % N_EXPERTS
        t = step // N_EXPERTS
        r = pl.ds(pl.multiple_of(c * self.cr, self.cr), self.cr)
        cols = pl.ds(pl.multiple_of(t * self.ct, V7X_LANES), self.ct)
        return pltpu.make_async_copy(self.w_hbms[m].at[e, r, cols], self.wf.at[slot, m, r], self.wsem.at[slot])

    def start_chunk(self, c, step):
        for m in range(len(self.w_hbms)):
            self._chunk(m, c, step, step % 2).start()

    def start_chunks(self, c0, c1, step):
        lax.fori_loop(c0, c1, lambda c, carry: (self.start_chunk(c, step), carry)[1], 0)

    def wait(self, step):
        slot = step % 2
        for m in range(len(self.w_hbms)):
            for c in range(MOE_W_CHUNKS):
                self._chunk(m, c, step, slot).wait()

    def tile(self, m, step):
        return self.wf.at[step % 2, m]


def _expert_block_loop(row0, n_blocks, in_hbm, out_hbm, out_col0, ibuf, obuf, isem, osem, compute,
                       per_block):
    Bm = MOE_ROWS
    out_cols = obuf.shape[2]

    def rows(j):
        return pl.ds(pl.multiple_of(row0 + j * Bm, Bm), Bm)

    def in_copy(j, slot):
        return pltpu.make_async_copy(in_hbm.at[rows(j)], ibuf.at[slot], isem.at[slot])

    def out_copy(j, slot):
        return pltpu.make_async_copy(
            obuf.at[slot], out_hbm.at[rows(j), pl.ds(pl.multiple_of(out_col0, V7X_LANES), out_cols)],
            osem.at[slot])

    @pl.when(n_blocks > 0)
    def _():
        in_copy(0, 0).start()

    def block(j, carry):
        slot = j % 2

        @pl.when(j + 1 < n_blocks)
        def _():
            in_copy(j + 1, 1 - slot).start()

        per_block(j)
        in_copy(j, slot).wait()

        @pl.when(j >= 2)
        def _():
            out_copy(j - 2, slot).wait()

        obuf[slot] = compute(ibuf[slot])
        out_copy(j, slot).start()
        return carry

    lax.fori_loop(0, n_blocks, block, 0)

    @pl.when(n_blocks >= 2)
    def _():
        out_copy(n_blocks - 2, n_blocks % 2).wait()

    @pl.when(n_blocks >= 1)
    def _():
        out_copy(n_blocks - 1, (n_blocks - 1) % 2).wait()


def _expert_step(stream, n_blocks, cast, run_blocks):
    step = pl.program_id(0) * N_EXPERTS + pl.program_id(1)
    n_steps = pl.num_programs(0) * N_EXPERTS
    has_next = step + 1 < n_steps

    @pl.when(step == 0)
    def _():
        stream.start_chunks(0, MOE_W_CHUNKS, step)

    stream.wait(step)

    @pl.when(n_blocks > 0)
    def _():
        cast(step)

    per = (MOE_W_CHUNKS + jnp.maximum(n_blocks, 1) - 1) // jnp.maximum(n_blocks, 1)

    @pl.when(jnp.logical_and(has_next, n_blocks == 0))
    def _():
        stream.start_chunks(0, MOE_W_CHUNKS, step + 1)

    def per_block(j):
        @pl.when(has_next)
        def _():
            stream.start_chunks(jnp.minimum(j * per, MOE_W_CHUNKS), jnp.minimum((j + 1) * per, MOE_W_CHUNKS),
                                step + 1)

    run_blocks(per_block)


def _moe_hidden_body(blk0_ref, nblk_ref, w1_hbm, w3_hbm, xs_hbm, hid_hbm,
                     wf_ref, w1b_ref, w3b_ref, ibuf, obuf, wsem, isem, osem):
    f = pl.program_id(0)
    e = pl.program_id(1)
    n_blocks = nblk_ref[e]
    half = ibuf.shape[2]
    stream = _WeightStream((w1_hbm, w3_hbm), wf_ref, wsem, MOE_F_TILE)

    def cast(step):
        _cast_tile(stream.tile(0, step), w1b_ref)
        _cast_tile(stream.tile(1, step), w3b_ref)

    def compute(xw):
        x_lo, x_hi = _unpack_bf16_pair(xw)

        def proj(wb_ref):
            return (jnp.dot(x_lo, wb_ref[:half, :], preferred_element_type=f32)
                    + jnp.dot(x_hi, wb_ref[half:, :], preferred_element_type=f32))

        return (jax.nn.silu(proj(w1b_ref)) * proj(w3b_ref)).astype(obuf.dtype)

    _expert_step(stream, n_blocks, cast, functools.partial(
        _expert_block_loop, blk0_ref[e] * MOE_ROWS, n_blocks, xs_hbm, hid_hbm, f * MOE_F_TILE,
        ibuf, obuf, isem, osem, compute))


def _moe_hidden(xs, w1, w3, blk0, nblk):
    P, Dh = xs.shape
    D = w1.shape[1]
    Bm, Ft = MOE_ROWS, MOE_F_TILE
    est = 2 * 2 * D * Ft * 4 + 2 * D * Ft * 2 + 2 * Bm * Dh * 4 + 2 * Bm * Ft * 2 + Bm * D * 2 + 3 * Bm * Ft * 4
    return pl.pallas_call(
        _moe_hidden_body,
        grid_spec=pltpu.PrefetchScalarGridSpec(
            num_scalar_prefetch=2,
            grid=(D_EXPERT // Ft, N_EXPERTS),
            in_specs=[pl.BlockSpec(memory_space=pl.ANY)] * 3,
            out_specs=pl.BlockSpec(memory_space=pl.ANY),
            scratch_shapes=[pltpu.VMEM((2, 2, D, Ft), f32),
                            pltpu.VMEM((D, Ft), bf16), pltpu.VMEM((D, Ft), bf16),
                            pltpu.VMEM((2, Bm, Dh), u32), pltpu.VMEM((2, Bm, Ft), bf16),
                            pltpu.SemaphoreType.DMA((2,)),
                            pltpu.SemaphoreType.DMA((2,)), pltpu.SemaphoreType.DMA((2,))],
        ),
        out_shape=jax.ShapeDtypeStruct((P, D_EXPERT), bf16),
        compiler_params=_params(("arbitrary", "arbitrary"), est),
        name="moe_expert_hidden",
    )(blk0, nblk, w1, w3, xs)


def _moe_out_body(blk0_ref, nblk_ref, w2_hbm, hid_hbm, y_hbm, wf_ref, w2b_ref, ibuf, obuf,
                  wsem, isem, osem):
    n = pl.program_id(0)
    e = pl.program_id(1)
    n_blocks = nblk_ref[e]
    stream = _WeightStream((w2_hbm,), wf_ref, wsem, MOE_N_TILE)

    def cast(step):
        _cast_tile(stream.tile(0, step), w2b_ref)

    def compute(h):
        y = jnp.dot(h, w2b_ref[...], preferred_element_type=f32)
        half = y.shape[1] // 2
        return _pack_bf16_pair(y[:, :half].astype(bf16), y[:, half:].astype(bf16))

    _expert_step(stream, n_blocks, cast, functools.partial(
        _expert_block_loop, blk0_ref[e] * MOE_ROWS, n_blocks, hid_hbm, y_hbm, n * (MOE_N_TILE // 2),
        ibuf, obuf, isem, osem, compute))


def _moe_out(hid, w2, blk0, nblk):
    P, F = hid.shape
    D = w2.shape[2]
    Bm, Nt = MOE_ROWS, MOE_N_TILE
    assert Nt == D, "the packed output pairs feature j with j + D/2: the column tile must span all of D"
    est = 2 * F * Nt * 4 + F * Nt * 2 + 2 * Bm * F * 2 + 2 * Bm * Nt * 2 + 2 * Bm * Nt * 4
    return pl.pallas_call(
        _moe_out_body,
        grid_spec=pltpu.PrefetchScalarGridSpec(
            num_scalar_prefetch=2,
            grid=(D // Nt, N_EXPERTS),
            in_specs=[pl.BlockSpec(memory_space=pl.ANY)] * 2,
            out_specs=pl.BlockSpec(memory_space=pl.ANY),
            scratch_shapes=[pltpu.VMEM((2, 1, F, Nt), f32), pltpu.VMEM((F, Nt), bf16),
                            pltpu.VMEM((2, Bm, F), bf16), pltpu.VMEM((2, Bm, Nt // 2), u32),
                            pltpu.SemaphoreType.DMA((2,)),
                            pltpu.SemaphoreType.DMA((2,)), pltpu.SemaphoreType.DMA((2,))],
        ),
        out_shape=jax.ShapeDtypeStruct((P, D // 2), u32),
        compiler_params=_params(("arbitrary", "arbitrary"), est),
        name="moe_expert_out",
    )(blk0, nblk, w2, hid)


def _combine_body(dest_ref, h_ref, gate_ref, g_ref, y_hbm, o_ref, buf, ss_s, sem):
    R = GATHER_ROWS
    i = pl.program_id(0)
    n_steps = pl.num_programs(0)

    def issue_rows(step, slot, r0, n):
        for r in range(n):
            for k in range(TOP_K):
                _row_copy(y_hbm, dest_ref[(step * R + r0 + r) * TOP_K + k],
                          buf.at[slot, k, pl.ds(r0 + r, 1)], sem.at[slot]).start()

    @pl.when(i == 0)
    def _():
        lax.fori_loop(0, R // V7X_SUBLANES,
                      lambda c, carry: (issue_rows(0, 0, c * V7X_SUBLANES, V7X_SUBLANES), carry)[1], 0)

    slot = i % 2
    for k in range(TOP_K):
        pltpu.make_async_copy(y_hbm.at[pl.ds(0, R)], buf.at[slot, k], sem.at[slot]).wait()
    g_fin = g_ref[...]
    D = h_ref.shape[1]
    RC = V7X_SUBLANES

    def chunk(c):
        return pl.ds(pl.multiple_of(c * RC, RC), RC)

    def passes(prefetch):
        def accumulate(c, carry):
            rs = chunk(c)
            gate = gate_ref[rs, :]
            g1 = gate[:, ROUTE_GATE_LANE:ROUTE_GATE_LANE + 1]
            g2 = gate[:, ROUTE_GATE_LANE + 1:ROUTE_GATE_LANE + 2]
            y1_lo, y1_hi = _unpack_pair_f32(buf[slot, 0, rs, :])
            y2_lo, y2_hi = _unpack_pair_f32(buf[slot, 1, rs, :])
            ss = None
            for cols, y1, y2 in ((slice(0, D // 2), y1_lo, y2_lo), (slice(D // 2, D), y1_hi, y2_hi)):
                h = h_ref[rs, cols] + y1 * g1 + y2 * g2
                o_ref[rs, cols] = h
                sq = h * h
                for j in range(sq.shape[1] // V7X_LANES):
                    part = sq[:, j * V7X_LANES:(j + 1) * V7X_LANES]
                    ss = part if ss is None else ss + part
            ss_s[rs, :] = ss
            if prefetch:
                issue_rows(i + 1, 1 - slot, c * RC, RC // 2)
            return carry

        lax.fori_loop(0, R // RC, accumulate, 0, unroll=4)

        def normalize(c, carry):
            rs = chunk(c)
            ms = jnp.sum(ss_s[rs, :], axis=-1, keepdims=True) * (1.0 / D)
            o_ref[rs, :] = o_ref[rs, :] * lax.rsqrt(ms + EPS) * g_fin
            if prefetch:
                issue_rows(i + 1, 1 - slot, c * RC + RC // 2, RC // 2)
            return carry

        lax.fori_loop(0, R // RC, normalize, 0, unroll=4)

    @pl.when(i + 1 < n_steps)
    def _():
        passes(True)

    @pl.when(i + 1 == n_steps)
    def _():
        passes(False)


def _moe_combine(h, y, dest, gate, n_final):
    T, D = h.shape
    R = GATHER_ROWS
    est = 2 * TOP_K * R * D * 4 + 4 * R * D * 4 + 3 * R * D * 4
    return pl.pallas_call(
        _combine_body,
        grid_spec=pltpu.PrefetchScalarGridSpec(
            num_scalar_prefetch=1,
            grid=(T // R,),
            in_specs=[pl.BlockSpec((R, D), lambda i, dest: (i, 0)),
                      pl.BlockSpec((R, V7X_LANES), lambda i, dest: (i, 0)),
                      pl.BlockSpec((1, D), lambda i, dest: (0, 0)),
                      pl.BlockSpec(memory_space=pl.ANY)],
            out_specs=pl.BlockSpec((R, D), lambda i, dest: (i, 0)),
            scratch_shapes=[pltpu.VMEM((2, TOP_K, R, D // 2), u32), pltpu.VMEM((R, V7X_LANES), f32),
                            pltpu.SemaphoreType.DMA((2,))],
        ),
        out_shape=jax.ShapeDtypeStruct((T, D), f32),
        compiler_params=_params(("arbitrary",), est),
        name="moe_combine_norm",
    )(dest, h, gate, n_final.reshape(1, D).astype(f32), y)


def _slot_layout(rec):
    T = rec.shape[0]
    A = T * TOP_K
    eid = rec[:, ROUTE_EID_LANE:ROUTE_EID_LANE + TOP_K].astype(jnp.int32).reshape(A)
    onehot = (eid[:, None] == jnp.arange(N_EXPERTS, dtype=jnp.int32)[None, :]).astype(jnp.int32)
    csum = jnp.cumsum(onehot, axis=0)
    rank = jnp.sum(onehot * csum, axis=1) - 1
    counts = csum[-1]
    pcounts = (counts + MOE_ROWS - 1) // MOE_ROWS * MOE_ROWS
    pend = jnp.cumsum(pcounts)
    pstart = pend - pcounts
    dest = (pstart[eid] + rank).astype(jnp.int32)
    n_blocks = -(-(A + N_EXPERTS * (MOE_ROWS - 1)) // MOE_ROWS)
    P = n_blocks * MOE_ROWS
    blk0 = (pstart // MOE_ROWS).astype(jnp.int32)
    nblk = (pcounts // MOE_ROWS).astype(jnp.int32)
    pad_end = pend.astype(jnp.int32)
    pad_len = (pcounts - counts).astype(jnp.int32)
    return dest, blk0, nblk, pad_end, pad_len, P


def kernel(x, mem, n_mix, w_in, gmlp_ln_g, gmlp_ln_b, gmlp_w_s, gmlp_b_s, hgrn_lower_bounds,
           hgrn_norm_g, w_out, n_cross, n_mem, w_q_x, w_k_x, w_v_x, w_o_x, n_moe, w_group,
           b_group, w_router, b_router, w1_e, w3_e, w2_e, n_final):
    B, S, D = x.shape
    n_mem_tok = mem.shape[1]
    T = B * S
    xt = x.reshape(T, D)

    hn = _rmsnorm(xt, n_mix[0], bf16)
    z = _matmul(hn, w_in[0], bf16, 1024, 512, "mm_w_in")
    ya = _gmlp(z, gmlp_ln_g[0], gmlp_ln_b[0], gmlp_w_s[0].astype(f32), gmlp_b_s[0].astype(f32))
    yb = _hgrn(z, hgrn_lower_bounds, hgrn_norm_g[0], B, S)
    h1, h1b, h1ss = _matmul_res(xt, [ya, yb], w_out[0], 1024, 512, "mm_w_out", with_norm_inputs=True)

    mn = _rmsnorm(mem.reshape(B * n_mem_tok, D), n_mem[0], bf16)
    kx = _matmul(mn, w_k_x[0], bf16, B * n_mem_tok, 512, "mm_w_k")
    vx = _matmul(mn, w_v_x[0], bf16, B * n_mem_tok, 512, "mm_w_v")
    wq_gained = (w_q_x[0] * n_cross[0].astype(f32)[:, None]).astype(bf16)
    ox = _xattn(h1b, h1ss, wq_gained, kx, vx, B, S, n_mem_tok)
    h2 = _matmul_res(h1, [ox], w_o_x[0], 1024, 512, "mm_w_o")

    xm, rec = _router(h2, n_moe[0], w_group[0], b_group[0], w_router[0], b_router[0])
    dest, blk0, nblk, pad_end, pad_len, P = _slot_layout(rec)
    xs = _moe_dispatch(xm, dest, pad_end, pad_len, P)
    hid = _moe_hidden(xs, w1_e[0], w3_e[0], blk0, nblk)
    y = _moe_out(hid, w2_e[0], blk0, nblk)
    out = _moe_combine(h2, y, dest, rec, n_final)
    return out.reshape(B, S, D)
```

```python
import functools

import jax
import jax.numpy as jnp
from jax import lax
from jax.experimental import pallas as pl
from jax.experimental.pallas import tpu as pltpu

f32 = jnp.float32
bf16 = jnp.bfloat16

EPS = 1e-6
D_MODEL = 4096
CHUNK = 64
GMLP_HEADS = 4
GMLP_HEAD_DIM = 512
GMLP_BLOCK = 128
D_GMLP = GMLP_HEADS * GMLP_HEAD_DIM
HGRN_HEAD_DIM = 128
HGRN_HEADS = 16
D_HGRN = HGRN_HEADS * HGRN_HEAD_DIM
X_HEADS = 4
X_HEAD_DIM = 1024
N_GROUPS = 4
EXPERTS_PER_GROUP = 8
N_EXPERTS = N_GROUPS * EXPERTS_PER_GROUP
TOP_K = 2
D_EXPERT = 1024

V7X_LANES = 128
V7X_SUBLANES = 8
V7X_VMEM_LIMIT_BYTES = 60000 * 1024

MOE_ROWS = 256
MOE_F_TILE = 512
MOE_N_TILE = 4096
MOE_W_CHUNKS = 8
GATHER_ROWS = 256
HGRN_GROUP = 4
HGRN_ROWS = 512
HGRN_SUB = 128


def _params(sem, est_bytes):
    limit = min(int(est_bytes) + (8 << 20), V7X_VMEM_LIMIT_BYTES)
    return pltpu.CompilerParams(dimension_semantics=sem, vmem_limit_bytes=limit)


def _rms_body(x_ref, g_ref, o_ref):
    x = x_ref[...].astype(f32)
    ms = jnp.mean(x * x, axis=-1, keepdims=True)
    o_ref[...] = (x * lax.rsqrt(ms + EPS) * g_ref[...]).astype(o_ref.dtype)


def _rmsnorm(x, g, out_dtype, tm=256):
    M, D = x.shape
    est = 2 * tm * D * (x.dtype.itemsize + jnp.dtype(out_dtype).itemsize) + 2 * tm * D * 4
    return pl.pallas_call(
        _rms_body,
        grid=(M // tm,),
        in_specs=[pl.BlockSpec((tm, D), lambda i: (i, 0)),
                  pl.BlockSpec((1, D), lambda i: (0, 0))],
        out_specs=pl.BlockSpec((tm, D), lambda i: (i, 0)),
        out_shape=jax.ShapeDtypeStruct((M, D), out_dtype),
        compiler_params=_params(("arbitrary",), est),
        name="rmsnorm",
    )(x, g.reshape(1, D).astype(f32))


CAST_ROWS = 512


def _cast_tile(src_ref, dst_ref, dst_row0=0):
    rows = min(CAST_ROWS, src_ref.shape[0])
    n = src_ref.shape[0] // rows

    def piece(p, c):
        r = pl.multiple_of(p * rows, rows)
        d = pl.multiple_of(dst_row0 + p * rows, rows)
        dst_ref[pl.ds(d, rows), :] = src_ref[pl.ds(r, rows), :].astype(bf16)
        return c

    lax.fori_loop(0, n, piece, 0)


def _mm_body(a_ref, w_ref, o_ref, wb_ref):
    @pl.when(pl.program_id(1) == 0)
    def _():
        _cast_tile(w_ref, wb_ref)

    o_ref[...] = jnp.dot(a_ref[...], wb_ref[...], preferred_element_type=f32).astype(o_ref.dtype)


def _matmul(a, w, out_dtype, tm, tn, name):
    M, K = a.shape
    N = w.shape[1]
    osz = jnp.dtype(out_dtype).itemsize
    est = 2 * (tm * K * 2 + K * tn * 4 + tm * tn * osz) + K * tn * 2 + tm * tn * 4
    return pl.pallas_call(
        _mm_body,
        grid=(N // tn, M // tm),
        in_specs=[pl.BlockSpec((tm, K), lambda j, i: (i, 0)),
                  pl.BlockSpec((K, tn), lambda j, i: (0, j))],
        out_specs=pl.BlockSpec((tm, tn), lambda j, i: (i, j)),
        out_shape=jax.ShapeDtypeStruct((M, N), out_dtype),
        scratch_shapes=[pltpu.VMEM((K, tn), bf16)],
        compiler_params=_params(("arbitrary", "arbitrary"), est),
        name=name,
    )(a, w)


def _mm_res_body(n_parts, with_norm_inputs, r_ref, *refs):
    a_refs = refs[:n_parts]
    w_refs = refs[n_parts:2 * n_parts]
    out_refs = refs[2 * n_parts:-1]
    wb_ref = refs[-1]
    kp = w_refs[0].shape[0]

    @pl.when(pl.program_id(1) == 0)
    def _():
        for p in range(n_parts):
            _cast_tile(w_refs[p], wb_ref, p * kp)

    acc = r_ref[...]
    for p in range(n_parts):
        acc = acc + jnp.dot(a_refs[p][...], wb_ref[p * kp:(p + 1) * kp, :], preferred_element_type=f32)
    out_refs[0][...] = acc
    if with_norm_inputs:
        out_refs[1][...] = acc.astype(bf16)
        sq = acc * acc
        ss = sq[:, 0:V7X_LANES]
        for g in range(1, acc.shape[1] // V7X_LANES):
            ss = ss + sq[:, g * V7X_LANES:(g + 1) * V7X_LANES]
        out_refs[2][...] = ss


def _matmul_res(res, a_list, w, tm, tn, name, with_norm_inputs=False):
    M, N = res.shape
    n = len(a_list)
    Kp = a_list[0].shape[1]
    a_specs = [pl.BlockSpec((tm, Kp), lambda j, i: (i, 0)) for _ in range(n)]
    w_specs = [pl.BlockSpec((Kp, tn), functools.partial(lambda j, i, p: (p, j), p=p)) for p in range(n)]
    est = 2 * (n * tm * Kp * 2 + n * Kp * tn * 4 + 2 * tm * tn * 4) + n * Kp * tn * 2 + 2 * tm * tn * 4
    tile = pl.BlockSpec((tm, tn), lambda j, i: (i, j))
    out_specs, out_shape = [tile], [jax.ShapeDtypeStruct((M, N), f32)]
    if with_norm_inputs:
        est += 2 * tm * tn * 2 + 2 * tm * V7X_LANES * 4 + tm * tn * 4
        out_specs += [tile, pl.BlockSpec((tm, V7X_LANES), lambda j, i: (i, j))]
        out_shape += [jax.ShapeDtypeStruct((M, N), bf16),
                      jax.ShapeDtypeStruct((M, N // tn * V7X_LANES), f32)]
    outs = pl.pallas_call(
        functools.partial(_mm_res_body, n, with_norm_inputs),
        grid=(N // tn, M // tm),
        in_specs=[tile] + a_specs + w_specs,
        out_specs=out_specs,
        out_shape=out_shape,
        scratch_shapes=[pltpu.VMEM((n * Kp, tn), bf16)],
        compiler_params=_params(("arbitrary", "arbitrary"), est),
        name=name,
    )(res, *a_list, *([w] * n))
    return outs if with_norm_inputs else outs[0]


def _gelu(x):
    return 0.5 * x * (1.0 + lax.erf(x * 0.7071067811865476))


def _gmlp_body(u_ref, v_ref, lng_ref, lnb_ref, ws_ref, bs_ref, o_ref):
    rows = u_ref.shape[0]
    gv = _gelu(v_ref[...].astype(f32))
    mu = jnp.mean(gv, axis=-1, keepdims=True)
    xc = gv - mu
    var = jnp.mean(xc * xc, axis=-1, keepdims=True)
    vn = (xc * lax.rsqrt(var + EPS) * lng_ref[...] + lnb_ref[...]).astype(bf16)
    ci = lax.broadcasted_iota(jnp.int32, (GMLP_BLOCK, GMLP_BLOCK), 0) // CHUNK
    cj = lax.broadcasted_iota(jnp.int32, (GMLP_BLOCK, GMLP_BLOCK), 1) // CHUNK
    w = jnp.where(cj <= ci, ws_ref[0], 0.0).astype(bf16)
    bias = bs_ref[0]
    for blk in range(rows // GMLP_BLOCK):
        sl = slice(blk * GMLP_BLOCK, (blk + 1) * GMLP_BLOCK)
        s = jnp.dot(w, vn[sl], preferred_element_type=f32) + bias
        o_ref[sl, :] = (_gelu(u_ref[sl, :].astype(f32)) * s).astype(o_ref.dtype)


def _gmlp(z, ln_g, ln_b, w_s, b_s, tr=1024):
    M = z.shape[0]
    C = GMLP_HEAD_DIM
    est = 2 * 3 * tr * C * 2 + 8 * tr * C * 4
    return pl.pallas_call(
        _gmlp_body,
        grid=(M // tr, GMLP_HEADS),
        in_specs=[pl.BlockSpec((tr, C), lambda i, h: (i, h)),
                  pl.BlockSpec((tr, C), lambda i, h: (i, GMLP_HEADS + h)),
                  pl.BlockSpec((1, C), lambda i, h: (0, h)),
                  pl.BlockSpec((1, C), lambda i, h: (0, h)),
                  pl.BlockSpec((1, GMLP_BLOCK, GMLP_BLOCK), lambda i, h: (h, 0, 0)),
                  pl.BlockSpec((1, GMLP_BLOCK, 1), lambda i, h: (h, 0, 0))],
        out_specs=pl.BlockSpec((tr, C), lambda i, h: (i, h)),
        out_shape=jax.ShapeDtypeStruct((M, D_GMLP), bf16),
        compiler_params=_params(("arbitrary", "arbitrary"), est),
        name="gmlp_mixer",
    )(z, z, ln_g.reshape(1, D_GMLP), ln_b.reshape(1, D_GMLP), w_s,
      b_s.reshape(GMLP_HEADS, GMLP_BLOCK, 1))


def _split3(x):
    hi = x.astype(bf16)
    r1 = x - hi.astype(f32)
    mid = r1.astype(bf16)
    lo = (r1 - mid.astype(f32)).astype(bf16)
    return hi, mid, lo


HGRN_STRIP = 2 * CHUNK
_NT = (((1,), (1,)), ((), ()))
_TN = (((0,), (0,)), ((), ()))


def _split2(x):
    hi = x.astype(bf16)
    mid = (x - hi.astype(f32)).astype(bf16)
    return hi, mid


def _hgrn_body(q_ref, f_ref, i_ref, g_ref, lbr_ref, ng_ref, o_ref,
               st_ref, qin_s, kin_s, qex_s, kdec_s, dec_s, am_s, xq_s, xk_s, inc_s, smat_s, oacc_s):
    C, K, G = CHUNK, HGRN_HEAD_DIM, HGRN_GROUP
    rows, W = q_ref.shape
    n_chunks = rows // C

    @pl.when(pl.program_id(2) == 0)
    def _():
        st_ref[...] = jnp.zeros_like(st_ref)

    r0 = lbr_ref[0:1, :]
    r1 = lbr_ref[1:2, :]
    m = jnp.maximum(r0, r1)
    e0 = jnp.exp(r0 - m)
    e1 = jnp.exp(r1 - m)
    lb = e0 / (e0 + e1)

    ti = lax.broadcasted_iota(jnp.int32, (C, C), 0)
    si = lax.broadcasted_iota(jnp.int32, (C, C), 1)
    tril = (si <= ti).astype(bf16)

    def chunk_rows(c):
        return pl.ds(pl.multiple_of(c * C, C), C)

    def gates(c, carry):
        rs = chunk_rows(c)
        fg = lb + (1.0 - lb) * jax.nn.sigmoid(f_ref[rs, :].astype(f32))
        hi, mid, lo = _split3(jnp.log(fg))
        b = (jnp.dot(tril, hi, preferred_element_type=f32)
             + jnp.dot(tril, mid, preferred_element_type=f32)
             + jnp.dot(tril, lo, preferred_element_type=f32))
        b_mid = b[C // 2:C // 2 + 1, :]
        b_last = b[C - 1:C, :]
        q_in = jax.nn.silu(q_ref[rs, :].astype(f32)) * jnp.exp(b - b_mid)
        k_in = (1.0 - fg) * jnp.exp(b_mid - b)
        qin_s[rs, :] = q_in.astype(bf16)
        kin_s[rs, :] = k_in.astype(bf16)
        qex_s[rs, :] = (q_in * jnp.exp(b_mid)).astype(bf16)
        kdec_s[rs, :] = (k_in * jnp.exp(b_last - b_mid)).astype(bf16)
        dec_s[pl.ds(pl.multiple_of(c * 8, 8), 8), :] = jnp.broadcast_to(jnp.exp(b_last), (8, W))
        return carry

    lax.fori_loop(0, n_chunks, gates, 0, unroll=2)

    @pl.when((pl.program_id(0) == 0) & (pl.program_id(1) == 0) & (pl.program_id(2) == 0))
    def _():
        am_s[...] = jnp.zeros_like(am_s)
        xq_s[...] = jnp.zeros_like(xq_s)
        xk_s[...] = jnp.zeros_like(xk_s)

    S = HGRN_STRIP
    tt = lax.broadcasted_iota(jnp.int32, (S, S), 0)
    ss = lax.broadcasted_iota(jnp.int32, (S, S), 1)
    keep = jnp.logical_and(ss <= tt, (ss // C) == (tt // C))

    SUB = HGRN_SUB
    cps = SUB // C
    lanes = [slice(h * K, (h + 1) * K) for h in range(G)]
    subs = [slice(u * SUB, (u + 1) * SUB) for u in range(rows // SUB)]
    for h, ls in enumerate(lanes):
        for c in range(n_chunks):
            cb = slice((c % cps) * K, (c % cps + 1) * K)
            xq_s[h, c * C:(c + 1) * C, cb] = qex_s[c * C:(c + 1) * C, ls]
            xk_s[h, c * C:(c + 1) * C, cb] = kdec_s[c * C:(c + 1) * C, ls]

    scores = [[lax.dot_general(qin_s[rs, ls], kin_s[rs, ls], _NT, preferred_element_type=f32) for rs in subs]
              for ls in lanes]
    for h, ls in enumerate(lanes):
        for u, rs in enumerate(subs):
            inc_s[h, :, u * cps * K:(u + 1) * cps * K] = lax.dot_general(
                i_ref[rs, ls], xk_s[h, rs, :], _TN, preferred_element_type=f32)
    for h in range(G):
        for u in range(len(subs)):
            for r in range(SUB // S):
                sl = slice(r * S, (r + 1) * S)
                am_s[h, u, sl, sl] = jnp.where(keep, scores[h][u][sl, sl], 0.0).astype(bf16)

    for h, ls in enumerate(lanes):
        st = st_ref[h]
        for c in range(n_chunks):
            cs = slice(c * K, (c + 1) * K)
            smat_s[h, :, cs] = st.astype(bf16)
            st = st * dec_s[c * 8:c * 8 + 1, ls] + inc_s[h, :, cs]
        st_ref[h] = st

    for h, ls in enumerate(lanes):
        for u, rs in enumerate(subs):
            oacc_s[rs, ls] = (
                jnp.dot(am_s[h, u], i_ref[rs, ls], preferred_element_type=f32)
                + lax.dot_general(xq_s[h, rs, :], smat_s[h, :, u * cps * K:(u + 1) * cps * K], _NT,
                                  preferred_element_type=f32))

    ng = ng_ref[...]

    def outputs(c, carry):
        rs = chunk_rows(c)
        for h in range(G):
            ls = slice(h * K, (h + 1) * K)
            o = oacc_s[rs, ls]
            ms = jnp.mean(o * o, axis=-1, keepdims=True)
            o = o * lax.rsqrt(ms + EPS) * ng[:, ls]
            o = o * jax.nn.silu(g_ref[rs, ls].astype(f32))
            o_ref[rs, ls] = o.astype(o_ref.dtype)
        return carry

    lax.fori_loop(0, n_chunks, outputs, 0, unroll=2)


def _hgrn(z, lower_bounds, norm_g, batch, seq):
    M = z.shape[0]
    W = HGRN_GROUP * HGRN_HEAD_DIM
    ts = HGRN_ROWS
    n_s = seq // ts
    n_hg = D_HGRN // W
    col0 = 2 * D_GMLP // W

    def zspec(part):
        return pl.BlockSpec((ts, W), lambda b, hg, s, part=part: (b * n_s + s, col0 + part * n_hg + hg))

    G, K, SUB = HGRN_GROUP, HGRN_HEAD_DIM, HGRN_SUB
    n_chunks = ts // CHUNK
    est = (2 * 5 * ts * W * 2 + 4 * ts * W * 2 + ts * W * 4 + G * ts * ts * 2 + 2 * G * ts * n_chunks * K * 2
           + G * K * n_chunks * K * 6 + G * K * K * 4 + (G + 2) * ts * ts * 4)
    return pl.pallas_call(
        _hgrn_body,
        grid=(batch, n_hg, n_s),
        in_specs=[zspec(0), zspec(1), zspec(2), zspec(3),
                  pl.BlockSpec((2, W), lambda b, hg, s: (0, hg)),
                  pl.BlockSpec((1, W), lambda b, hg, s: (0, hg))],
        out_specs=pl.BlockSpec((ts, W), lambda b, hg, s: (b * n_s + s, hg)),
        out_shape=jax.ShapeDtypeStruct((M, D_HGRN), bf16),
        scratch_shapes=[pltpu.VMEM((G, K, K), f32),
                        pltpu.VMEM((ts, W), bf16),
                        pltpu.VMEM((ts, W), bf16),
                        pltpu.VMEM((ts, W), bf16),
                        pltpu.VMEM((ts, W), bf16),
                        pltpu.VMEM((n_chunks * 8, W), f32),
                        pltpu.VMEM((G, ts // SUB, SUB, SUB), bf16),
                        pltpu.VMEM((G, ts, SUB // CHUNK * K), bf16),
                        pltpu.VMEM((G, ts, SUB // CHUNK * K), bf16),
                        pltpu.VMEM((G, K, n_chunks * K), f32),
                        pltpu.VMEM((G, K, n_chunks * K), bf16),
                        pltpu.VMEM((ts, W), f32)],
        compiler_params=_params(("arbitrary", "arbitrary", "arbitrary"), est),
        name="hgrn2_mixer",
    )(z, z, z, z, lower_bounds.astype(f32), norm_g.reshape(1, D_HGRN).astype(f32))


def _xattn_body(h_ref, ss_ref, wq_ref, k_ref, v_ref, o_ref):
    D = h_ref.shape[1]
    inv_rms = lax.rsqrt(jnp.sum(ss_ref[...], axis=-1, keepdims=True) * (1.0 / D) + EPS)
    q = (jnp.dot(h_ref[...], wq_ref[...], preferred_element_type=f32) * inv_rms).astype(bf16)
    s = lax.dot_general(q, k_ref[...], (((1,), (1,)), ((), ())), preferred_element_type=f32)
    s = s * (X_HEAD_DIM ** -0.5)
    s = s - jnp.max(s, axis=-1, keepdims=True)
    p = jnp.exp(s)
    p = p / jnp.sum(p, axis=-1, keepdims=True)
    o_ref[...] = jnp.dot(p.astype(bf16), v_ref[...], preferred_element_type=f32).astype(o_ref.dtype)


def _xattn(hb, ss, wq, k, v, batch, seq, n_mem, tq=1024):
    M, D = hb.shape
    n_s = seq // tq
    dh = X_HEAD_DIM
    est = (2 * (tq * D * 2 + tq * ss.shape[1] * 4 + D * dh * 2 + 2 * n_mem * dh * 2 + tq * dh * 2)
           + 3 * tq * dh * 4)
    return pl.pallas_call(
        _xattn_body,
        grid=(X_HEADS, batch, n_s),
        in_specs=[pl.BlockSpec((tq, D), lambda h, b, s: (b * n_s + s, 0)),
                  pl.BlockSpec((tq, ss.shape[1]), lambda h, b, s: (b * n_s + s, 0)),
                  pl.BlockSpec((D, dh), lambda h, b, s: (0, h)),
                  pl.BlockSpec((n_mem, dh), lambda h, b, s: (b, h)),
                  pl.BlockSpec((n_mem, dh), lambda h, b, s: (b, h))],
        out_specs=pl.BlockSpec((tq, dh), lambda h, b, s: (b * n_s + s, h)),
        out_shape=jax.ShapeDtypeStruct((M, D), bf16),
        compiler_params=_params(("arbitrary", "arbitrary", "arbitrary"), est),
        name="xattn_qproj",
    )(hb, ss, wq, k, v)


u32 = jnp.uint32
ROUTE_GATE_LANE = 0
ROUTE_EID_LANE = 2


def _pack_bf16_pair(lo, hi):
    lo32 = lax.bitcast_convert_type(lo.astype(f32), u32) >> 16
    hi32 = lax.bitcast_convert_type(hi.astype(f32), u32) & jnp.uint32(0xFFFF0000)
    return lo32 | hi32


def _unpack_bf16_pair(w):
    lo = lax.bitcast_convert_type(w << 16, f32).astype(bf16)
    hi = lax.bitcast_convert_type(w & jnp.uint32(0xFFFF0000), f32).astype(bf16)
    return lo, hi


def _unpack_pair_f32(w):
    return (lax.bitcast_convert_type(w << 16, f32),
            lax.bitcast_convert_type(w & jnp.uint32(0xFFFF0000), f32))


def _first_lane_of(mask, lane):
    return jnp.min(jnp.where(mask, lane, V7X_LANES), axis=-1, keepdims=True)


def _router_body(h_ref, g_ref, wr_ref, br_ref, xm_ref, r_ref):
    x = h_ref[...]
    D = x.shape[1]
    ms = jnp.mean(x * x, axis=-1, keepdims=True)
    xn = x * lax.rsqrt(ms + EPS) * g_ref[...]
    xh, xm = _split2(xn)
    wh, wm = _split2(wr_ref[...])
    dot = functools.partial(jnp.dot, preferred_element_type=f32)
    acc = dot(xm, wh) + dot(xh, wm)
    acc = acc + dot(xh, wh)
    logits = acc + br_ref[...]

    xm_ref[...] = _pack_bf16_pair(xh[:, :D // 2], xh[:, D // 2:])

    lane = lax.broadcasted_iota(jnp.int32, logits.shape, 1)
    neg = jnp.float32(-jnp.inf)
    is_g = lane < N_GROUPS
    gl = jnp.where(is_g, logits, neg)
    gmax = jnp.max(gl, axis=-1, keepdims=True)
    grp = _first_lane_of(gl == gmax, lane)
    p_grp = 1.0 / jnp.sum(jnp.where(is_g, jnp.exp(logits - gmax), 0.0), axis=-1, keepdims=True)
    e0 = N_GROUPS + grp * EXPERTS_PER_GROUP
    el = jnp.where(jnp.logical_and(lane >= e0, lane < e0 + EXPERTS_PER_GROUP), logits, neg)
    m1 = jnp.max(el, axis=-1, keepdims=True)
    i1 = _first_lane_of(el == m1, lane)
    el2 = jnp.where(lane == i1, neg, el)
    m2 = jnp.max(el2, axis=-1, keepdims=True)
    i2 = _first_lane_of(el2 == m2, lane)
    t = jnp.exp(m2 - m1)
    g1 = p_grp * (1.0 / (1.0 + t))
    g2 = p_grp * (t / (1.0 + t))
    rec = jnp.where(lane == ROUTE_GATE_LANE, g1, 0.0)
    rec = jnp.where(lane == ROUTE_GATE_LANE + 1, g2, rec)
    rec = jnp.where(lane == ROUTE_EID_LANE, (i1 - N_GROUPS).astype(f32), rec)
    rec = jnp.where(lane == ROUTE_EID_LANE + 1, (i2 - N_GROUPS).astype(f32), rec)
    r_ref[...] = rec


def _router(h, n_moe, w_group, b_group, w_router, b_router, tm=256):
    M, D = h.shape
    n_log = N_GROUPS + N_EXPERTS
    wr = jnp.concatenate([w_group] + [w_router[g] for g in range(N_GROUPS)], axis=1)
    wr = jnp.pad(wr.astype(f32), ((0, 0), (0, V7X_LANES - n_log)))
    br = jnp.pad(jnp.concatenate([b_group, b_router.reshape(-1)]).astype(f32), (0, V7X_LANES - n_log))
    est = 2 * tm * D * 4 + 2 * D * V7X_LANES * 4 + 2 * tm * D * 2 + 6 * tm * D * 4
    return pl.pallas_call(
        _router_body,
        grid=(M // tm,),
        in_specs=[pl.BlockSpec((tm, D), lambda i: (i, 0)),
                  pl.BlockSpec((1, D), lambda i: (0, 0)),
                  pl.BlockSpec((D, V7X_LANES), lambda i: (0, 0)),
                  pl.BlockSpec((1, V7X_LANES), lambda i: (0, 0))],
        out_specs=[pl.BlockSpec((tm, D // 2), lambda i: (i, 0)),
                   pl.BlockSpec((tm, V7X_LANES), lambda i: (i, 0))],
        out_shape=[jax.ShapeDtypeStruct((M, D // 2), u32),
                   jax.ShapeDtypeStruct((M, V7X_LANES), f32)],
        compiler_params=_params(("arbitrary",), est),
        name="moe_router",
    )(h, n_moe.reshape(1, D).astype(f32), wr, br.reshape(1, V7X_LANES))


def _row_copy(src_hbm, row, dst, sem):
    return pltpu.make_async_copy(src_hbm.at[pl.ds(row, 1)], dst, sem)


PAD_BITS = tuple(1 << k for k in reversed(range((MOE_ROWS - 1).bit_length())))


def _dispatch_body(dest_ref, padend_ref, padlen_ref, xm_ref, xs_hbm, buf, zbuf, sem, zsem):
    R = GATHER_ROWS
    i = pl.program_id(0)
    n_steps = pl.num_programs(0)
    slot = i % 2

    def wait_slot(s):
        for _ in range(TOP_K):
            pltpu.make_async_copy(buf.at[s], xs_hbm.at[pl.ds(0, R)], sem.at[s]).wait()

    def pad_copies(fn):
        def per_expert(e, c):
            ln = padlen_ref[e]
            end = padend_ref[e]
            for bit in PAD_BITS:
                rows_per_copy = bit if bit >= V7X_SUBLANES else 1
                for j in range(bit // rows_per_copy):
                    @pl.when((ln & bit) != 0)
                    def _(bit=bit, end=end, j=j, n=rows_per_copy):
                        start = end - bit + j * n
                        if n > 1:
                            start = pl.multiple_of(start, V7X_SUBLANES)
                        fn(pltpu.make_async_copy(zbuf.at[pl.ds(0, n)], xs_hbm.at[pl.ds(start, n)], zsem))
                end = end - (ln & bit)
            return c
        lax.fori_loop(0, N_EXPERTS, per_expert, 0)

    @pl.when(i == 0)
    def _():
        zbuf[...] = jnp.zeros_like(zbuf)
        pad_copies(lambda cp: cp.start())

    @pl.when(i >= 2)
    def _():
        wait_slot(slot)

    buf[slot] = xm_ref[...]

    def issue(r, c):
        for k in range(TOP_K):
            d = dest_ref[(i * R + r) * TOP_K + k]
            pltpu.make_async_copy(buf.at[slot, pl.ds(r, 1)], xs_hbm.at[pl.ds(d, 1)], sem.at[slot]).start()
        return c

    lax.fori_loop(0, R, issue, 0, unroll=8)

    @pl.when(i == n_steps - 1)
    def _():
        if n_steps >= 2:
            wait_slot(1 - slot)
        wait_slot(slot)
        pad_copies(lambda cp: cp.wait())


def _moe_dispatch(xm, dest, pad_end, pad_len, P):
    T, Dh = xm.shape
    R = GATHER_ROWS
    est = 2 * R * Dh * 4 + 2 * R * Dh * 4 + PAD_BITS[0] * Dh * 4
    return pl.pallas_call(
        _dispatch_body,
        grid_spec=pltpu.PrefetchScalarGridSpec(
            num_scalar_prefetch=3,
            grid=(T // R,),
            in_specs=[pl.BlockSpec((R, Dh), lambda i, dest, pend, plen: (i, 0))],
            out_specs=pl.BlockSpec(memory_space=pl.ANY),
            scratch_shapes=[pltpu.VMEM((2, R, Dh), u32), pltpu.VMEM((PAD_BITS[0], Dh), u32),
                            pltpu.SemaphoreType.DMA((2,)), pltpu.SemaphoreType.DMA],
        ),
        out_shape=jax.ShapeDtypeStruct((P, Dh), u32),
        compiler_params=_params(("arbitrary",), est),
        name="moe_dispatch",
    )(dest, pad_end, pad_len, xm)


class _WeightStream:
    def __init__(self, w_hbms, wf_ref, wsem, col_tile):
        self.w_hbms, self.wf, self.wsem, self.ct = w_hbms, wf_ref, wsem, col_tile
        self.rows = wf_ref.shape[2]
        self.cr = self.rows // MOE_W_CHUNKS

    def _chunk(self, m, c, step, slot):
        e = step % N_EXPERTS
        t = step // N_EXPERTS
        r = pl.ds(pl.multiple_of(c * self.cr, self.cr), self.cr)
        cols = pl.ds(pl.multiple_of(t * self.ct, V7X_LANES), self.ct)
        return pltpu.make_async_copy(self.w_hbms[m].at[e, r, cols], self.wf.at[slot, m, r], self.wsem.at[slot])

    def start_chunk(self, c, step):
        for m in range(len(self.w_hbms)):
            self._chunk(m, c, step, step % 2).start()

    def start_chunks(self, c0, c1, step):
        lax.fori_loop(c0, c1, lambda c, carry: (self.start_chunk(c, step), carry)[1], 0)

    def wait(self, step):
        slot = step % 2
        for m in range(len(self.w_hbms)):
            for c in range(MOE_W_CHUNKS):
                self._chunk(m, c, step, slot).wait()

    def tile(self, m, step):
        return self.wf.at[step % 2, m]


def _expert_block_loop(row0, n_blocks, in_hbm, out_hbm, out_col0, ibuf, obuf, isem, osem, compute,
                       per_block):
    Bm = MOE_ROWS
    out_cols = obuf.shape[2]

    def rows(j):
        return pl.ds(pl.multiple_of(row0 + j * Bm, Bm), Bm)

    def in_copy(j, slot):
        return pltpu.make_async_copy(in_hbm.at[rows(j)], ibuf.at[slot], isem.at[slot])

    def out_copy(j, slot):
        return pltpu.make_async_copy(
            obuf.at[slot], out_hbm.at[rows(j), pl.ds(pl.multiple_of(out_col0, V7X_LANES), out_cols)],
            osem.at[slot])

    if per_block is None:
        @pl.when(n_blocks > 0)
        def _():
            in_copy(0, 0).start()
        return

    def block(j, carry):
        slot = j % 2

        @pl.when(j + 1 < n_blocks)
        def _():
            in_copy(j + 1, 1 - slot).start()

        per_block(j)
        in_copy(j, slot).wait()

        @pl.when(j >= 2)
        def _():
            out_copy(j - 2, slot).wait()

        obuf[slot] = compute(ibuf[slot])
        out_copy(j, slot).start()
        return carry

    lax.fori_loop(0, n_blocks, block, 0)

    @pl.when(n_blocks >= 2)
    def _():
        out_copy(n_blocks - 2, n_blocks % 2).wait()

    @pl.when(n_blocks >= 1)
    def _():
        out_copy(n_blocks - 1, (n_blocks - 1) % 2).wait()


def _expert_step(stream, n_blocks, cast, run_blocks):
    step = pl.program_id(0) * N_EXPERTS + pl.program_id(1)
    n_steps = pl.num_programs(0) * N_EXPERTS
    has_next = step + 1 < n_steps

    @pl.when(step == 0)
    def _():
        stream.start_chunks(0, MOE_W_CHUNKS, step)

    run_blocks(None)
    stream.wait(step)

    @pl.when(n_blocks > 0)
    def _():
        cast(step)

    spread = jnp.maximum((n_blocks + 1) // 2, 1)
    per = (MOE_W_CHUNKS + spread - 1) // spread

    @pl.when(jnp.logical_and(has_next, n_blocks == 0))
    def _():
        stream.start_chunks(0, MOE_W_CHUNKS, step + 1)

    def per_block(j):
        @pl.when(has_next)
        def _():
            stream.start_chunks(jnp.minimum(j * per, MOE_W_CHUNKS), jnp.minimum((j + 1) * per, MOE_W_CHUNKS),
                                step + 1)

    run_blocks(per_block)


def _moe_hidden_body(blk0_ref, nblk_ref, w1_hbm, w3_hbm, xs_hbm, hid_hbm,
                     wf_ref, w1b_ref, w3b_ref, ibuf, obuf, wsem, isem, osem):
    f = pl.program_id(0)
    e = pl.program_id(1)
    n_blocks = nblk_ref[e]
    half = ibuf.shape[2]
    stream = _WeightStream((w1_hbm, w3_hbm), wf_ref, wsem, MOE_F_TILE)

    def cast(step):
        _cast_tile(stream.tile(0, step), w1b_ref)
        _cast_tile(stream.tile(1, step), w3b_ref)

    def compute(xw):
        x_lo, x_hi = _unpack_bf16_pair(xw)

        def proj(wb_ref):
            return (jnp.dot(x_lo, wb_ref[:half, :], preferred_element_type=f32)
                    + jnp.dot(x_hi, wb_ref[half:, :], preferred_element_type=f32))

        return (jax.nn.silu(proj(w1b_ref)) * proj(w3b_ref)).astype(obuf.dtype)

    _expert_step(stream, n_blocks, cast, functools.partial(
        _expert_block_loop, blk0_ref[e] * MOE_ROWS, n_blocks, xs_hbm, hid_hbm, f * MOE_F_TILE,
        ibuf, obuf, isem, osem, compute))


def _moe_hidden(xs, w1, w3, blk0, nblk):
    P, Dh = xs.shape
    D = w1.shape[1]
    Bm, Ft = MOE_ROWS, MOE_F_TILE
    est = 2 * 2 * D * Ft * 4 + 2 * D * Ft * 2 + 2 * Bm * Dh * 4 + 2 * Bm * Ft * 2 + Bm * D * 2 + 3 * Bm * Ft * 4
    return pl.pallas_call(
        _moe_hidden_body,
        grid_spec=pltpu.PrefetchScalarGridSpec(
            num_scalar_prefetch=2,
            grid=(D_EXPERT // Ft, N_EXPERTS),
            in_specs=[pl.BlockSpec(memory_space=pl.ANY)] * 3,
            out_specs=pl.BlockSpec(memory_space=pl.ANY),
            scratch_shapes=[pltpu.VMEM((2, 2, D, Ft), f32),
                            pltpu.VMEM((D, Ft), bf16), pltpu.VMEM((D, Ft), bf16),
                            pltpu.VMEM((2, Bm, Dh), u32), pltpu.VMEM((2, Bm, Ft), bf16),
                            pltpu.SemaphoreType.DMA((2,)),
                            pltpu.SemaphoreType.DMA((2,)), pltpu.SemaphoreType.DMA((2,))],
        ),
        out_shape=jax.ShapeDtypeStruct((P, D_EXPERT), bf16),
        compiler_params=_params(("arbitrary", "arbitrary"), est),
        name="moe_expert_hidden",
    )(blk0, nblk, w1, w3, xs)


def _moe_out_body(blk0_ref, nblk_ref, w2_hbm, hid_hbm, y_hbm, wf_ref, w2b_ref, ibuf, obuf,
                  wsem, isem, osem):
    n = pl.program_id(0)
    e = pl.program_id(1)
    n_blocks = nblk_ref[e]
    stream = _WeightStream((w2_hbm,), wf_ref, wsem, MOE_N_TILE)

    def cast(step):
        _cast_tile(stream.tile(0, step), w2b_ref)

    def compute(h):
        y = jnp.dot(h, w2b_ref[...], preferred_element_type=f32)
        half = y.shape[1] // 2
        return _pack_bf16_pair(y[:, :half].astype(bf16), y[:, half:].astype(bf16))

    _expert_step(stream, n_blocks, cast, functools.partial(
        _expert_block_loop, blk0_ref[e] * MOE_ROWS, n_blocks, hid_hbm, y_hbm, n * (MOE_N_TILE // 2),
        ibuf, obuf, isem, osem, compute))


def _moe_out(hid, w2, blk0, nblk):
    P, F = hid.shape
    D = w2.shape[2]
    Bm, Nt = MOE_ROWS, MOE_N_TILE
    assert Nt == D, "the packed output pairs feature j with j + D/2: the column tile must span all of D"
    est = 2 * F * Nt * 4 + F * Nt * 2 + 2 * Bm * F * 2 + 2 * Bm * Nt * 2 + 2 * Bm * Nt * 4
    return pl.pallas_call(
        _moe_out_body,
        grid_spec=pltpu.PrefetchScalarGridSpec(
            num_scalar_prefetch=2,
            grid=(D // Nt, N_EXPERTS),
            in_specs=[pl.BlockSpec(memory_space=pl.ANY)] * 2,
            out_specs=pl.BlockSpec(memory_space=pl.ANY),
            scratch_shapes=[pltpu.VMEM((2, 1, F, Nt), f32), pltpu.VMEM((F, Nt), bf16),
                            pltpu.VMEM((2, Bm, F), bf16), pltpu.VMEM((2, Bm, Nt // 2), u32),
                            pltpu.SemaphoreType.DMA((2,)),
                            pltpu.SemaphoreType.DMA((2,)), pltpu.SemaphoreType.DMA((2,))],
        ),
        out_shape=jax.ShapeDtypeStruct((P, D // 2), u32),
        compiler_params=_params(("arbitrary", "arbitrary"), est),
        name="moe_expert_out",
    )(blk0, nblk, w2, hid)


def _combine_body(dest_ref, h_ref, gate_ref, g_ref, y_hbm, o_ref, buf, ss_s, sem):
    R = GATHER_ROWS
    i = pl.program_id(0)
    n_steps = pl.num_programs(0)

    def issue_rows(step, slot, r0, n):
        for r in range(n):
            for k in range(TOP_K):
                _row_copy(y_hbm, dest_ref[(step * R + r0 + r) * TOP_K + k],
                          buf.at[slot, k, pl.ds(r0 + r, 1)], sem.at[slot]).start()

    @pl.when(i == 0)
    def _():
        lax.fori_loop(0, R // V7X_SUBLANES,
                      lambda c, carry: (issue_rows(0, 0, c * V7X_SUBLANES, V7X_SUBLANES), carry)[1], 0)

    slot = i % 2
    for k in range(TOP_K):
        pltpu.make_async_copy(y_hbm.at[pl.ds(0, R)], buf.at[slot, k], sem.at[slot]).wait()
    g_fin = g_ref[...]
    D = h_ref.shape[1]
    RC = V7X_SUBLANES

    def chunk(c):
        return pl.ds(pl.multiple_of(c * RC, RC), RC)

    def passes(prefetch):
        def accumulate(c, carry):
            rs = chunk(c)
            gate = gate_ref[rs, :]
            g1 = gate[:, ROUTE_GATE_LANE:ROUTE_GATE_LANE + 1]
            g2 = gate[:, ROUTE_GATE_LANE + 1:ROUTE_GATE_LANE + 2]
            y1_lo, y1_hi = _unpack_pair_f32(buf[slot, 0, rs, :])
            y2_lo, y2_hi = _unpack_pair_f32(buf[slot, 1, rs, :])
            ss = None
            for cols, y1, y2 in ((slice(0, D // 2), y1_lo, y2_lo), (slice(D // 2, D), y1_hi, y2_hi)):
                h = h_ref[rs, cols] + y1 * g1 + y2 * g2
                o_ref[rs, cols] = h
                sq = h * h
                for j in range(sq.shape[1] // V7X_LANES):
                    part = sq[:, j * V7X_LANES:(j + 1) * V7X_LANES]
                    ss = part if ss is None else ss + part
            ss_s[rs, :] = ss
            if prefetch:
                issue_rows(i + 1, 1 - slot, c * RC, RC // 2)
            return carry

        lax.fori_loop(0, R // RC, accumulate, 0, unroll=4)

        def normalize(c, carry):
            rs = chunk(c)
            ms = jnp.sum(ss_s[rs, :], axis=-1, keepdims=True) * (1.0 / D)
            o_ref[rs, :] = o_ref[rs, :] * lax.rsqrt(ms + EPS) * g_fin
            if prefetch:
                issue_rows(i + 1, 1 - slot, c * RC + RC // 2, RC // 2)
            return carry

        lax.fori_loop(0, R // RC, normalize, 0, unroll=4)

    @pl.when(i + 1 < n_steps)
    def _():
        passes(True)

    @pl.when(i + 1 == n_steps)
    def _():
        passes(False)


def _moe_combine(h, y, dest, gate, n_final):
    T, D = h.shape
    R = GATHER_ROWS
    est = 2 * TOP_K * R * D * 4 + 4 * R * D * 4 + 3 * R * D * 4
    return pl.pallas_call(
        _combine_body,
        grid_spec=pltpu.PrefetchScalarGridSpec(
            num_scalar_prefetch=1,
            grid=(T // R,),
            in_specs=[pl.BlockSpec((R, D), lambda i, dest: (i, 0)),
                      pl.BlockSpec((R, V7X_LANES), lambda i, dest: (i, 0)),
                      pl.BlockSpec((1, D), lambda i, dest: (0, 0)),
                      pl.BlockSpec(memory_space=pl.ANY)],
            out_specs=pl.BlockSpec((R, D), lambda i, dest: (i, 0)),
            scratch_shapes=[pltpu.VMEM((2, TOP_K, R, D // 2), u32), pltpu.VMEM((R, V7X_LANES), f32),
                            pltpu.SemaphoreType.DMA((2,))],
        ),
        out_shape=jax.ShapeDtypeStruct((T, D), f32),
        compiler_params=_params(("arbitrary",), est),
        name="moe_combine_norm",
    )(dest, h, gate, n_final.reshape(1, D).astype(f32), y)


def _slot_layout(rec):
    T = rec.shape[0]
    A = T * TOP_K
    eid = rec[:, ROUTE_EID_LANE:ROUTE_EID_LANE + TOP_K].astype(jnp.int32).reshape(A)
    onehot = (eid[:, None] == jnp.arange(N_EXPERTS, dtype=jnp.int32)[None, :]).astype(jnp.int32)
    csum = jnp.cumsum(onehot, axis=0)
    rank = jnp.sum(onehot * csum, axis=1) - 1
    counts = csum[-1]
    pcounts = (counts + MOE_ROWS - 1) // MOE_ROWS * MOE_ROWS
    pend = jnp.cumsum(pcounts)
    pstart = pend - pcounts
    dest = (pstart[eid] + rank).astype(jnp.int32)
    n_blocks = -(-(A + N_EXPERTS * (MOE_ROWS - 1)) // MOE_ROWS)
    P = n_blocks * MOE_ROWS
    blk0 = (pstart // MOE_ROWS).astype(jnp.int32)
    nblk = (pcounts // MOE_ROWS).astype(jnp.int32)
    pad_end = pend.astype(jnp.int32)
    pad_len = (pcounts - counts).astype(jnp.int32)
    return dest, blk0, nblk, pad_end, pad_len, P


def kernel(x, mem, n_mix, w_in, gmlp_ln_g, gmlp_ln_b, gmlp_w_s, gmlp_b_s, hgrn_lower_bounds,
           hgrn_norm_g, w_out, n_cross, n_mem, w_q_x, w_k_x, w_v_x, w_o_x, n_moe, w_group,
           b_group, w_router, b_router, w1_e, w3_e, w2_e, n_final):
    B, S, D = x.shape
    n_mem_tok = mem.shape[1]
    T = B * S
    xt = x.reshape(T, D)

    hn = _rmsnorm(xt, n_mix[0], bf16)
    z = _matmul(hn, w_in[0], bf16, 1024, 512, "mm_w_in")
    ya = _gmlp(z, gmlp_ln_g[0], gmlp_ln_b[0], gmlp_w_s[0].astype(f32), gmlp_b_s[0].astype(f32))
    yb = _hgrn(z, hgrn_lower_bounds, hgrn_norm_g[0], B, S)
    h1, h1b, h1ss = _matmul_res(xt, [ya, yb], w_out[0], 1024, 512, "mm_w_out", with_norm_inputs=True)

    mn = _rmsnorm(mem.reshape(B * n_mem_tok, D), n_mem[0], bf16)
    kx = _matmul(mn, w_k_x[0], bf16, B * n_mem_tok, 512, "mm_w_k")
    vx = _matmul(mn, w_v_x[0], bf16, B * n_mem_tok, 512, "mm_w_v")
    wq_gained = (w_q_x[0] * n_cross[0].astype(f32)[:, None]).astype(bf16)
    ox = _xattn(h1b, h1ss, wq_gained, kx, vx, B, S, n_mem_tok)
    h2 = _matmul_res(h1, [ox], w_o_x[0], 1024, 512, "mm_w_o")

    xm, rec = _router(h2, n_moe[0], w_group[0], b_group[0], w_router[0], b_router[0])
    dest, blk0, nblk, pad_end, pad_len, P = _slot_layout(rec)
    xs = _moe_dispatch(xm, dest, pad_end, pad_len, P)
    hid = _moe_hidden(xs, w1_e[0], w3_e[0], blk0, nblk)
    y = _moe_out(hid, w2_e[0], blk0, nblk)
    out = _moe_combine(h2, y, dest, rec, n_final)
    return out.reshape(B, S, D)
```

```python
import functools

import jax
import jax.numpy as jnp
from jax import lax
from jax.experimental import pallas as pl
from jax.experimental.pallas import tpu as pltpu

f32 = jnp.float32
bf16 = jnp.bfloat16

EPS = 1e-6
D_MODEL = 4096
CHUNK = 64
GMLP_HEADS = 4
GMLP_HEAD_DIM = 512
GMLP_BLOCK = 128
D_GMLP = GMLP_HEADS * GMLP_HEAD_DIM
HGRN_HEAD_DIM = 128
HGRN_HEADS = 16
D_HGRN = HGRN_HEADS * HGRN_HEAD_DIM
X_HEADS = 4
X_HEAD_DIM = 1024
N_GROUPS = 4
EXPERTS_PER_GROUP = 8
N_EXPERTS = N_GROUPS * EXPERTS_PER_GROUP
TOP_K = 2
D_EXPERT = 1024

V7X_LANES = 128
V7X_SUBLANES = 8
V7X_VMEM_LIMIT_BYTES = 60000 * 1024

MOE_ROWS = 256
MOE_F_TILE = 512
MOE_N_TILE = 4096
MOE_W_CHUNKS = 8
GATHER_ROWS = 512
HGRN_GROUP = 4
HGRN_ROWS = 512
HGRN_SUB = 128


def _params(sem, est_bytes):
    limit = min(int(est_bytes) + (8 << 20), V7X_VMEM_LIMIT_BYTES)
    return pltpu.CompilerParams(dimension_semantics=sem, vmem_limit_bytes=limit)


def _rms_body(x_ref, g_ref, o_ref):
    x = x_ref[...].astype(f32)
    ms = jnp.mean(x * x, axis=-1, keepdims=True)
    o_ref[...] = (x * lax.rsqrt(ms + EPS) * g_ref[...]).astype(o_ref.dtype)


def _rmsnorm(x, g, out_dtype, tm=256):
    M, D = x.shape
    est = 2 * tm * D * (x.dtype.itemsize + jnp.dtype(out_dtype).itemsize) + 2 * tm * D * 4
    return pl.pallas_call(
        _rms_body,
        grid=(M // tm,),
        in_specs=[pl.BlockSpec((tm, D), lambda i: (i, 0)),
                  pl.BlockSpec((1, D), lambda i: (0, 0))],
        out_specs=pl.BlockSpec((tm, D), lambda i: (i, 0)),
        out_shape=jax.ShapeDtypeStruct((M, D), out_dtype),
        compiler_params=_params(("arbitrary",), est),
        name="rmsnorm",
    )(x, g.reshape(1, D).astype(f32))


CAST_ROWS = 512


def _cast_tile(src_ref, dst_ref, dst_row0=0):
    rows = min(CAST_ROWS, src_ref.shape[0])
    n = src_ref.shape[0] // rows

    def piece(p, c):
        r = pl.multiple_of(p * rows, rows)
        d = pl.multiple_of(dst_row0 + p * rows, rows)
        dst_ref[pl.ds(d, rows), :] = src_ref[pl.ds(r, rows), :].astype(bf16)
        return c

    lax.fori_loop(0, n, piece, 0)


def _mm_body(a_ref, w_ref, o_ref, wb_ref):
    @pl.when(pl.program_id(1) == 0)
    def _():
        _cast_tile(w_ref, wb_ref)

    o_ref[...] = jnp.dot(a_ref[...], wb_ref[...], preferred_element_type=f32).astype(o_ref.dtype)


def _matmul(a, w, out_dtype, tm, tn, name):
    M, K = a.shape
    N = w.shape[1]
    osz = jnp.dtype(out_dtype).itemsize
    est = 2 * (tm * K * 2 + K * tn * 4 + tm * tn * osz) + K * tn * 2 + tm * tn * 4
    return pl.pallas_call(
        _mm_body,
        grid=(N // tn, M // tm),
        in_specs=[pl.BlockSpec((tm, K), lambda j, i: (i, 0)),
                  pl.BlockSpec((K, tn), lambda j, i: (0, j))],
        out_specs=pl.BlockSpec((tm, tn), lambda j, i: (i, j)),
        out_shape=jax.ShapeDtypeStruct((M, N), out_dtype),
        scratch_shapes=[pltpu.VMEM((K, tn), bf16)],
        compiler_params=_params(("arbitrary", "arbitrary"), est),
        name=name,
    )(a, w)


def _mm_res_body(n_parts, with_norm_inputs, r_ref, *refs):
    a_refs = refs[:n_parts]
    w_refs = refs[n_parts:2 * n_parts]
    out_refs = refs[2 * n_parts:-1]
    wb_ref = refs[-1]
    kp = w_refs[0].shape[0]

    @pl.when(pl.program_id(1) == 0)
    def _():
        for p in range(n_parts):
            _cast_tile(w_refs[p], wb_ref, p * kp)

    acc = r_ref[...]
    for p in range(n_parts):
        acc = acc + jnp.dot(a_refs[p][...], wb_ref[p * kp:(p + 1) * kp, :], preferred_element_type=f32)
    out_refs[0][...] = acc
    if with_norm_inputs:
        out_refs[1][...] = acc.astype(bf16)
        sq = acc * acc
        ss = sq[:, 0:V7X_LANES]
        for g in range(1, acc.shape[1] // V7X_LANES):
            ss = ss + sq[:, g * V7X_LANES:(g + 1) * V7X_LANES]
        out_refs[2][...] = ss


def _matmul_res(res, a_list, w, tm, tn, name, with_norm_inputs=False):
    M, N = res.shape
    n = len(a_list)
    Kp = a_list[0].shape[1]
    a_specs = [pl.BlockSpec((tm, Kp), lambda j, i: (i, 0)) for _ in range(n)]
    w_specs = [pl.BlockSpec((Kp, tn), functools.partial(lambda j, i, p: (p, j), p=p)) for p in range(n)]
    est = 2 * (n * tm * Kp * 2 + n * Kp * tn * 4 + 2 * tm * tn * 4) + n * Kp * tn * 2 + 2 * tm * tn * 4
    tile = pl.BlockSpec((tm, tn), lambda j, i: (i, j))
    out_specs, out_shape = [tile], [jax.ShapeDtypeStruct((M, N), f32)]
    if with_norm_inputs:
        est += 2 * tm * tn * 2 + 2 * tm * V7X_LANES * 4 + tm * tn * 4
        out_specs += [tile, pl.BlockSpec((tm, V7X_LANES), lambda j, i: (i, j))]
        out_shape += [jax.ShapeDtypeStruct((M, N), bf16),
                      jax.ShapeDtypeStruct((M, N // tn * V7X_LANES), f32)]
    outs = pl.pallas_call(
        functools.partial(_mm_res_body, n, with_norm_inputs),
        grid=(N // tn, M // tm),
        in_specs=[tile] + a_specs + w_specs,
        out_specs=out_specs,
        out_shape=out_shape,
        scratch_shapes=[pltpu.VMEM((n * Kp, tn), bf16)],
        compiler_params=_params(("arbitrary", "arbitrary"), est),
        name=name,
    )(res, *a_list, *([w] * n))
    return outs if with_norm_inputs else outs[0]


def _gelu(x):
    return 0.5 * x * (1.0 + lax.erf(x * 0.7071067811865476))


def _gmlp_body(u_ref, v_ref, lng_ref, lnb_ref, ws_ref, bs_ref, o_ref):
    rows = u_ref.shape[0]
    gv = _gelu(v_ref[...].astype(f32))
    mu = jnp.mean(gv, axis=-1, keepdims=True)
    xc = gv - mu
    var = jnp.mean(xc * xc, axis=-1, keepdims=True)
    vn = (xc * lax.rsqrt(var + EPS) * lng_ref[...] + lnb_ref[...]).astype(bf16)
    ci = lax.broadcasted_iota(jnp.int32, (GMLP_BLOCK, GMLP_BLOCK), 0) // CHUNK
    cj = lax.broadcasted_iota(jnp.int32, (GMLP_BLOCK, GMLP_BLOCK), 1) // CHUNK
    w = jnp.where(cj <= ci, ws_ref[0], 0.0).astype(bf16)
    bias = bs_ref[0]
    for blk in range(rows // GMLP_BLOCK):
        sl = slice(blk * GMLP_BLOCK, (blk + 1) * GMLP_BLOCK)
        s = jnp.dot(w, vn[sl], preferred_element_type=f32) + bias
        o_ref[sl, :] = (_gelu(u_ref[sl, :].astype(f32)) * s).astype(o_ref.dtype)


def _gmlp(z, ln_g, ln_b, w_s, b_s, tr=1024):
    M = z.shape[0]
    C = GMLP_HEAD_DIM
    est = 2 * 3 * tr * C * 2 + 8 * tr * C * 4
    return pl.pallas_call(
        _gmlp_body,
        grid=(M // tr, GMLP_HEADS),
        in_specs=[pl.BlockSpec((tr, C), lambda i, h: (i, h)),
                  pl.BlockSpec((tr, C), lambda i, h: (i, GMLP_HEADS + h)),
                  pl.BlockSpec((1, C), lambda i, h: (0, h)),
                  pl.BlockSpec((1, C), lambda i, h: (0, h)),
                  pl.BlockSpec((1, GMLP_BLOCK, GMLP_BLOCK), lambda i, h: (h, 0, 0)),
                  pl.BlockSpec((1, GMLP_BLOCK, 1), lambda i, h: (h, 0, 0))],
        out_specs=pl.BlockSpec((tr, C), lambda i, h: (i, h)),
        out_shape=jax.ShapeDtypeStruct((M, D_GMLP), bf16),
        compiler_params=_params(("arbitrary", "arbitrary"), est),
        name="gmlp_mixer",
    )(z, z, ln_g.reshape(1, D_GMLP), ln_b.reshape(1, D_GMLP), w_s,
      b_s.reshape(GMLP_HEADS, GMLP_BLOCK, 1))


def _split3(x):
    hi = x.astype(bf16)
    r1 = x - hi.astype(f32)
    mid = r1.astype(bf16)
    lo = (r1 - mid.astype(f32)).astype(bf16)
    return hi, mid, lo


HGRN_STRIP = 2 * CHUNK
_NT = (((1,), (1,)), ((), ()))
_TN = (((0,), (0,)), ((), ()))


def _split2(x):
    hi = x.astype(bf16)
    mid = (x - hi.astype(f32)).astype(bf16)
    return hi, mid


def _hgrn_body(q_ref, f_ref, i_ref, g_ref, lbr_ref, ng_ref, o_ref,
               st_ref, qin_s, kin_s, qex_s, kdec_s, dec_s, am_s, xq_s, xk_s, inc_s, smat_s, oacc_s):
    C, K, G = CHUNK, HGRN_HEAD_DIM, HGRN_GROUP
    rows, W = q_ref.shape
    n_chunks = rows // C

    @pl.when(pl.program_id(2) == 0)
    def _():
        st_ref[...] = jnp.zeros_like(st_ref)

    r0 = lbr_ref[0:1, :]
    r1 = lbr_ref[1:2, :]
    m = jnp.maximum(r0, r1)
    e0 = jnp.exp(r0 - m)
    e1 = jnp.exp(r1 - m)
    lb = e0 / (e0 + e1)

    ti = lax.broadcasted_iota(jnp.int32, (C, C), 0)
    si = lax.broadcasted_iota(jnp.int32, (C, C), 1)
    tril = (si <= ti).astype(bf16)

    def chunk_rows(c):
        return pl.ds(pl.multiple_of(c * C, C), C)

    def gates(c, carry):
        rs = chunk_rows(c)
        fg = lb + (1.0 - lb) * jax.nn.sigmoid(f_ref[rs, :].astype(f32))
        hi, mid, lo = _split3(jnp.log(fg))
        b = (jnp.dot(tril, hi, preferred_element_type=f32)
             + jnp.dot(tril, mid, preferred_element_type=f32)
             + jnp.dot(tril, lo, preferred_element_type=f32))
        b_mid = b[C // 2:C // 2 + 1, :]
        b_last = b[C - 1:C, :]
        q_in = jax.nn.silu(q_ref[rs, :].astype(f32)) * jnp.exp(b - b_mid)
        k_in = (1.0 - fg) * jnp.exp(b_mid - b)
        qin_s[rs, :] = q_in.astype(bf16)
        kin_s[rs, :] = k_in.astype(bf16)
        qex_s[rs, :] = (q_in * jnp.exp(b_mid)).astype(bf16)
        kdec_s[rs, :] = (k_in * jnp.exp(b_last - b_mid)).astype(bf16)
        dec_s[pl.ds(pl.multiple_of(c * 8, 8), 8), :] = jnp.broadcast_to(jnp.exp(b_last), (8, W))
        return carry

    lax.fori_loop(0, n_chunks, gates, 0, unroll=2)

    @pl.when((pl.program_id(0) == 0) & (pl.program_id(1) == 0) & (pl.program_id(2) == 0))
    def _():
        am_s[...] = jnp.zeros_like(am_s)
        xq_s[...] = jnp.zeros_like(xq_s)
        xk_s[...] = jnp.zeros_like(xk_s)

    S = HGRN_STRIP
    tt = lax.broadcasted_iota(jnp.int32, (S, S), 0)
    ss = lax.broadcasted_iota(jnp.int32, (S, S), 1)
    keep = jnp.logical_and(ss <= tt, (ss // C) == (tt // C))

    SUB = HGRN_SUB
    cps = SUB // C
    lanes = [slice(h * K, (h + 1) * K) for h in range(G)]
    subs = [slice(u * SUB, (u + 1) * SUB) for u in range(rows // SUB)]
    for h, ls in enumerate(lanes):
        for c in range(n_chunks):
            cb = slice((c % cps) * K, (c % cps + 1) * K)
            xq_s[h, c * C:(c + 1) * C, cb] = qex_s[c * C:(c + 1) * C, ls]
            xk_s[h, c * C:(c + 1) * C, cb] = kdec_s[c * C:(c + 1) * C, ls]

    scores = [[lax.dot_general(qin_s[rs, ls], kin_s[rs, ls], _NT, preferred_element_type=f32) for rs in subs]
              for ls in lanes]
    for h, ls in enumerate(lanes):
        for u, rs in enumerate(subs):
            inc_s[h, :, u * cps * K:(u + 1) * cps * K] = lax.dot_general(
                i_ref[rs, ls], xk_s[h, rs, :], _TN, preferred_element_type=f32)
    for h in range(G):
        for u in range(len(subs)):
            for r in range(SUB // S):
                sl = slice(r * S, (r + 1) * S)
                am_s[h, u, sl, sl] = jnp.where(keep, scores[h][u][sl, sl], 0.0).astype(bf16)

    for h, ls in enumerate(lanes):
        st = st_ref[h]
        for c in range(n_chunks):
            cs = slice(c * K, (c + 1) * K)
            smat_s[h, :, cs] = st.astype(bf16)
            st = st * dec_s[c * 8:c * 8 + 1, ls] + inc_s[h, :, cs]
        st_ref[h] = st

    for h, ls in enumerate(lanes):
        for u, rs in enumerate(subs):
            oacc_s[rs, ls] = (
                jnp.dot(am_s[h, u], i_ref[rs, ls], preferred_element_type=f32)
                + lax.dot_general(xq_s[h, rs, :], smat_s[h, :, u * cps * K:(u + 1) * cps * K], _NT,
                                  preferred_element_type=f32))

    ng = ng_ref[...]

    def outputs(c, carry):
        rs = chunk_rows(c)
        for h in range(G):
            ls = slice(h * K, (h + 1) * K)
            o = oacc_s[rs, ls]
            ms = jnp.mean(o * o, axis=-1, keepdims=True)
            o = o * lax.rsqrt(ms + EPS) * ng[:, ls]
            o = o * jax.nn.silu(g_ref[rs, ls].astype(f32))
            o_ref[rs, ls] = o.astype(o_ref.dtype)
        return carry

    lax.fori_loop(0, n_chunks, outputs, 0, unroll=2)


def _hgrn(z, lower_bounds, norm_g, batch, seq):
    M = z.shape[0]
    W = HGRN_GROUP * HGRN_HEAD_DIM
    ts = HGRN_ROWS
    n_s = seq // ts
    n_hg = D_HGRN // W
    col0 = 2 * D_GMLP // W

    def zspec(part):
        return pl.BlockSpec((ts, W), lambda b, hg, s, part=part: (b * n_s + s, col0 + part * n_hg + hg))

    G, K, SUB = HGRN_GROUP, HGRN_HEAD_DIM, HGRN_SUB
    n_chunks = ts // CHUNK
    est = (2 * 5 * ts * W * 2 + 4 * ts * W * 2 + ts * W * 4 + G * ts * ts * 2 + 2 * G * ts * n_chunks * K * 2
           + G * K * n_chunks * K * 6 + G * K * K * 4 + (G + 2) * ts * ts * 4)
    return pl.pallas_call(
        _hgrn_body,
        grid=(batch, n_hg, n_s),
        in_specs=[zspec(0), zspec(1), zspec(2), zspec(3),
                  pl.BlockSpec((2, W), lambda b, hg, s: (0, hg)),
                  pl.BlockSpec((1, W), lambda b, hg, s: (0, hg))],
        out_specs=pl.BlockSpec((ts, W), lambda b, hg, s: (b * n_s + s, hg)),
        out_shape=jax.ShapeDtypeStruct((M, D_HGRN), bf16),
        scratch_shapes=[pltpu.VMEM((G, K, K), f32),
                        pltpu.VMEM((ts, W), bf16),
                        pltpu.VMEM((ts, W), bf16),
                        pltpu.VMEM((ts, W), bf16),
                        pltpu.VMEM((ts, W), bf16),
                        pltpu.VMEM((n_chunks * 8, W), f32),
                        pltpu.VMEM((G, ts // SUB, SUB, SUB), bf16),
                        pltpu.VMEM((G, ts, SUB // CHUNK * K), bf16),
                        pltpu.VMEM((G, ts, SUB // CHUNK * K), bf16),
                        pltpu.VMEM((G, K, n_chunks * K), f32),
                        pltpu.VMEM((G, K, n_chunks * K), bf16),
                        pltpu.VMEM((ts, W), f32)],
        compiler_params=_params(("arbitrary", "arbitrary", "arbitrary"), est),
        name="hgrn2_mixer",
    )(z, z, z, z, lower_bounds.astype(f32), norm_g.reshape(1, D_HGRN).astype(f32))


def _xattn_body(h_ref, ss_ref, wq_ref, k_ref, v_ref, o_ref):
    D = h_ref.shape[1]
    inv_rms = lax.rsqrt(jnp.sum(ss_ref[...], axis=-1, keepdims=True) * (1.0 / D) + EPS)
    q = (jnp.dot(h_ref[...], wq_ref[...], preferred_element_type=f32) * inv_rms).astype(bf16)
    s = lax.dot_general(q, k_ref[...], (((1,), (1,)), ((), ())), preferred_element_type=f32)
    s = s * (X_HEAD_DIM ** -0.5)
    s = s - jnp.max(s, axis=-1, keepdims=True)
    p = jnp.exp(s)
    p = p / jnp.sum(p, axis=-1, keepdims=True)
    o_ref[...] = jnp.dot(p.astype(bf16), v_ref[...], preferred_element_type=f32).astype(o_ref.dtype)


def _xattn(hb, ss, wq, k, v, batch, seq, n_mem, tq=1024):
    M, D = hb.shape
    n_s = seq // tq
    dh = X_HEAD_DIM
    est = (2 * (tq * D * 2 + tq * ss.shape[1] * 4 + D * dh * 2 + 2 * n_mem * dh * 2 + tq * dh * 2)
           + 3 * tq * dh * 4)
    return pl.pallas_call(
        _xattn_body,
        grid=(X_HEADS, batch, n_s),
        in_specs=[pl.BlockSpec((tq, D), lambda h, b, s: (b * n_s + s, 0)),
                  pl.BlockSpec((tq, ss.shape[1]), lambda h, b, s: (b * n_s + s, 0)),
                  pl.BlockSpec((D, dh), lambda h, b, s: (0, h)),
                  pl.BlockSpec((n_mem, dh), lambda h, b, s: (b, h)),
                  pl.BlockSpec((n_mem, dh), lambda h, b, s: (b, h))],
        out_specs=pl.BlockSpec((tq, dh), lambda h, b, s: (b * n_s + s, h)),
        out_shape=jax.ShapeDtypeStruct((M, D), bf16),
        compiler_params=_params(("arbitrary", "arbitrary", "arbitrary"), est),
        name="xattn_qproj",
    )(hb, ss, wq, k, v)


u32 = jnp.uint32
ROUTE_GATE_LANE = 0
ROUTE_EID_LANE = 2


def _pack_bf16_pair(lo, hi):
    lo32 = lax.bitcast_convert_type(lo.astype(f32), u32) >> 16
    hi32 = lax.bitcast_convert_type(hi.astype(f32), u32) & jnp.uint32(0xFFFF0000)
    return lo32 | hi32


def _unpack_bf16_pair(w):
    lo = lax.bitcast_convert_type(w << 16, f32).astype(bf16)
    hi = lax.bitcast_convert_type(w & jnp.uint32(0xFFFF0000), f32).astype(bf16)
    return lo, hi


def _unpack_pair_f32(w):
    return (lax.bitcast_convert_type(w << 16, f32),
            lax.bitcast_convert_type(w & jnp.uint32(0xFFFF0000), f32))


def _first_lane_of(mask, lane):
    return jnp.min(jnp.where(mask, lane, V7X_LANES), axis=-1, keepdims=True)


def _router_body(h_ref, g_ref, wr_ref, br_ref, xm_ref, r_ref):
    x = h_ref[...]
    D = x.shape[1]
    ms = jnp.mean(x * x, axis=-1, keepdims=True)
    xn = x * lax.rsqrt(ms + EPS) * g_ref[...]
    xh, xm = _split2(xn)
    wh, wm = _split2(wr_ref[...])
    dot = functools.partial(jnp.dot, preferred_element_type=f32)
    nl = wh.shape[1]
    hi = dot(xh, jnp.concatenate([wh, wm], axis=1))
    logits = (dot(xm, wh) + hi[:, nl:]) + hi[:, :nl] + br_ref[...]

    xm_ref[...] = _pack_bf16_pair(xh[:, :D // 2], xh[:, D // 2:])

    lane = lax.broadcasted_iota(jnp.int32, logits.shape, 1)
    neg = jnp.float32(-jnp.inf)
    is_g = lane < N_GROUPS
    gl = jnp.where(is_g, logits, neg)
    gmax = jnp.max(gl, axis=-1, keepdims=True)
    grp = _first_lane_of(gl == gmax, lane)
    p_grp = 1.0 / jnp.sum(jnp.where(is_g, jnp.exp(logits - gmax), 0.0), axis=-1, keepdims=True)
    e0 = N_GROUPS + grp * EXPERTS_PER_GROUP
    el = jnp.where(jnp.logical_and(lane >= e0, lane < e0 + EXPERTS_PER_GROUP), logits, neg)
    m1 = jnp.max(el, axis=-1, keepdims=True)
    i1 = _first_lane_of(el == m1, lane)
    el2 = jnp.where(lane == i1, neg, el)
    m2 = jnp.max(el2, axis=-1, keepdims=True)
    i2 = _first_lane_of(el2 == m2, lane)
    t = jnp.exp(m2 - m1)
    g1 = p_grp * (1.0 / (1.0 + t))
    g2 = p_grp * (t / (1.0 + t))
    rec = jnp.where(lane == ROUTE_GATE_LANE, g1, 0.0)
    rec = jnp.where(lane == ROUTE_GATE_LANE + 1, g2, rec)
    rec = jnp.where(lane == ROUTE_EID_LANE, (i1 - N_GROUPS).astype(f32), rec)
    rec = jnp.where(lane == ROUTE_EID_LANE + 1, (i2 - N_GROUPS).astype(f32), rec)
    r_ref[...] = rec


def _router(h, n_moe, w_group, b_group, w_router, b_router, tm=512):
    M, D = h.shape
    n_log = N_GROUPS + N_EXPERTS
    wr = jnp.concatenate([w_group] + [w_router[g] for g in range(N_GROUPS)], axis=1)
    wr = jnp.pad(wr.astype(f32), ((0, 0), (0, V7X_LANES - n_log)))
    br = jnp.pad(jnp.concatenate([b_group, b_router.reshape(-1)]).astype(f32), (0, V7X_LANES - n_log))
    est = 2 * tm * D * 4 + 2 * D * V7X_LANES * 4 + 2 * tm * D * 2 + 6 * tm * D * 4
    return pl.pallas_call(
        _router_body,
        grid=(M // tm,),
        in_specs=[pl.BlockSpec((tm, D), lambda i: (i, 0)),
                  pl.BlockSpec((1, D), lambda i: (0, 0)),
                  pl.BlockSpec((D, V7X_LANES), lambda i: (0, 0)),
                  pl.BlockSpec((1, V7X_LANES), lambda i: (0, 0))],
        out_specs=[pl.BlockSpec((tm, D // 2), lambda i: (i, 0)),
                   pl.BlockSpec((tm, V7X_LANES), lambda i: (i, 0))],
        out_shape=[jax.ShapeDtypeStruct((M, D // 2), u32),
                   jax.ShapeDtypeStruct((M, V7X_LANES), f32)],
        compiler_params=_params(("arbitrary",), est),
        name="moe_router",
    )(h, n_moe.reshape(1, D).astype(f32), wr, br.reshape(1, V7X_LANES))


def _row_copy(src_hbm, row, dst, sem):
    return pltpu.make_async_copy(src_hbm.at[pl.ds(row, 1)], dst, sem)


PAD_BITS = tuple(1 << k for k in reversed(range((MOE_ROWS - 1).bit_length())))


def _dispatch_body(dest_ref, padend_ref, padlen_ref, xm_ref, xs_hbm, buf, zbuf, sem, zsem):
    R = GATHER_ROWS
    i = pl.program_id(0)
    n_steps = pl.num_programs(0)
    slot = i % 2

    def wait_slot(s):
        for _ in range(TOP_K):
            pltpu.make_async_copy(buf.at[s], xs_hbm.at[pl.ds(0, R)], sem.at[s]).wait()

    def pad_copies(fn):
        def per_expert(e, c):
            ln = padlen_ref[e]
            end = padend_ref[e]
            for bit in PAD_BITS:
                rows_per_copy = bit if bit >= V7X_SUBLANES else 1
                for j in range(bit // rows_per_copy):
                    @pl.when((ln & bit) != 0)
                    def _(bit=bit, end=end, j=j, n=rows_per_copy):
                        start = end - bit + j * n
                        if n > 1:
                            start = pl.multiple_of(start, V7X_SUBLANES)
                        fn(pltpu.make_async_copy(zbuf.at[pl.ds(0, n)], xs_hbm.at[pl.ds(start, n)], zsem))
                end = end - (ln & bit)
            return c
        lax.fori_loop(0, N_EXPERTS, per_expert, 0)

    @pl.when(i == 0)
    def _():
        zbuf[...] = jnp.zeros_like(zbuf)
        pad_copies(lambda cp: cp.start())

    @pl.when(i >= 2)
    def _():
        wait_slot(slot)

    buf[slot] = xm_ref[...]

    def issue(r, c):
        for k in range(TOP_K):
            d = dest_ref[(i * R + r) * TOP_K + k]
            pltpu.make_async_copy(buf.at[slot, pl.ds(r, 1)], xs_hbm.at[pl.ds(d, 1)], sem.at[slot]).start()
        return c

    lax.fori_loop(0, R, issue, 0, unroll=8)

    @pl.when(i == n_steps - 1)
    def _():
        if n_steps >= 2:
            wait_slot(1 - slot)
        wait_slot(slot)
        pad_copies(lambda cp: cp.wait())


def _moe_dispatch(xm, dest, pad_end, pad_len, P):
    T, Dh = xm.shape
    R = GATHER_ROWS
    est = 2 * R * Dh * 4 + 2 * R * Dh * 4 + PAD_BITS[0] * Dh * 4
    return pl.pallas_call(
        _dispatch_body,
        grid_spec=pltpu.PrefetchScalarGridSpec(
            num_scalar_prefetch=3,
            grid=(T // R,),
            in_specs=[pl.BlockSpec((R, Dh), lambda i, dest, pend, plen: (i, 0))],
            out_specs=pl.BlockSpec(memory_space=pl.ANY),
            scratch_shapes=[pltpu.VMEM((2, R, Dh), u32), pltpu.VMEM((PAD_BITS[0], Dh), u32),
                            pltpu.SemaphoreType.DMA((2,)), pltpu.SemaphoreType.DMA],
        ),
        out_shape=jax.ShapeDtypeStruct((P, Dh), u32),
        compiler_params=_params(("arbitrary",), est),
        name="moe_dispatch",
    )(dest, pad_end, pad_len, xm)


class _WeightStream:
    def __init__(self, w_hbms, wf_ref, wsem, col_tile):
        self.w_hbms, self.wf, self.wsem, self.ct = w_hbms, wf_ref, wsem, col_tile
        self.rows = wf_ref.shape[2]
        self.cr = self.rows // MOE_W_CHUNKS

    def _chunk(self, m, c, step, slot):
        e = step % N_EXPERTS
        t = step // N_EXPERTS
        r = pl.ds(pl.multiple_of(c * self.cr, self.cr), self.cr)
        cols = pl.ds(pl.multiple_of(t * self.ct, V7X_LANES), self.ct)
        return pltpu.make_async_copy(self.w_hbms[m].at[e, r, cols], self.wf.at[slot, m, r], self.wsem.at[slot])

    def start_chunk(self, c, step):
        for m in range(len(self.w_hbms)):
            self._chunk(m, c, step, step % 2).start()

    def start_chunks(self, c0, c1, step):
        lax.fori_loop(c0, c1, lambda c, carry: (self.start_chunk(c, step), carry)[1], 0)

    def wait(self, step):
        slot = step % 2
        for m in range(len(self.w_hbms)):
            for c in range(MOE_W_CHUNKS):
                self._chunk(m, c, step, slot).wait()

    def tile(self, m, step):
        return self.wf.at[step % 2, m]


def _expert_block_loop(row0, n_blocks, in_hbm, out_hbm, out_col0, ibuf, obuf, isem, osem, compute,
                       per_block):
    Bm = MOE_ROWS
    out_cols = obuf.shape[2]

    def rows(j):
        return pl.ds(pl.multiple_of(row0 + j * Bm, Bm), Bm)

    def in_copy(j, slot):
        return pltpu.make_async_copy(in_hbm.at[rows(j)], ibuf.at[slot], isem.at[slot])

    def out_copy(j, slot):
        return pltpu.make_async_copy(
            obuf.at[slot], out_hbm.at[rows(j), pl.ds(pl.multiple_of(out_col0, V7X_LANES), out_cols)],
            osem.at[slot])

    if per_block is None:
        @pl.when(n_blocks > 0)
        def _():
            in_copy(0, 0).start()
        return

    def block(j, carry):
        slot = j % 2

        @pl.when(j + 1 < n_blocks)
        def _():
            in_copy(j + 1, 1 - slot).start()

        per_block(j)
        in_copy(j, slot).wait()

        @pl.when(j >= 2)
        def _():
            out_copy(j - 2, slot).wait()

        obuf[slot] = compute(ibuf[slot])
        out_copy(j, slot).start()
        return carry

    lax.fori_loop(0, n_blocks, block, 0)

    @pl.when(n_blocks >= 2)
    def _():
        out_copy(n_blocks - 2, n_blocks % 2).wait()

    @pl.when(n_blocks >= 1)
    def _():
        out_copy(n_blocks - 1, (n_blocks - 1) % 2).wait()


def _expert_step(stream, n_blocks, cast, run_blocks):
    step = pl.program_id(0) * N_EXPERTS + pl.program_id(1)
    n_steps = pl.num_programs(0) * N_EXPERTS
    has_next = step + 1 < n_steps

    @pl.when(step == 0)
    def _():
        stream.start_chunks(0, MOE_W_CHUNKS, step)

    run_blocks(None)
    stream.wait(step)

    @pl.when(n_blocks > 0)
    def _():
        cast(step)

    spread = jnp.maximum((n_blocks + 1) // 2, 1)
    per = (MOE_W_CHUNKS + spread - 1) // spread

    @pl.when(jnp.logical_and(has_next, n_blocks == 0))
    def _():
        stream.start_chunks(0, MOE_W_CHUNKS, step + 1)

    def per_block(j):
        @pl.when(has_next)
        def _():
            stream.start_chunks(jnp.minimum(j * per, MOE_W_CHUNKS), jnp.minimum((j + 1) * per, MOE_W_CHUNKS),
                                step + 1)

    run_blocks(per_block)


def _moe_hidden_body(blk0_ref, nblk_ref, w1_hbm, w3_hbm, xs_hbm, hid_hbm,
                     wf_ref, w1b_ref, w3b_ref, ibuf, obuf, wsem, isem, osem):
    f = pl.program_id(0)
    e = pl.program_id(1)
    n_blocks = nblk_ref[e]
    half = ibuf.shape[2]
    stream = _WeightStream((w1_hbm, w3_hbm), wf_ref, wsem, MOE_F_TILE)

    def cast(step):
        _cast_tile(stream.tile(0, step), w1b_ref)
        _cast_tile(stream.tile(1, step), w3b_ref)

    def compute(xw):
        x_lo, x_hi = _unpack_bf16_pair(xw)

        def proj(wb_ref):
            return (jnp.dot(x_lo, wb_ref[:half, :], preferred_element_type=f32)
                    + jnp.dot(x_hi, wb_ref[half:, :], preferred_element_type=f32))

        return (jax.nn.silu(proj(w1b_ref)) * proj(w3b_ref)).astype(obuf.dtype)

    _expert_step(stream, n_blocks, cast, functools.partial(
        _expert_block_loop, blk0_ref[e] * MOE_ROWS, n_blocks, xs_hbm, hid_hbm, f * MOE_F_TILE,
        ibuf, obuf, isem, osem, compute))


def _moe_hidden(xs, w1, w3, blk0, nblk):
    P, Dh = xs.shape
    D = w1.shape[1]
    Bm, Ft = MOE_ROWS, MOE_F_TILE
    est = 2 * 2 * D * Ft * 4 + 2 * D * Ft * 2 + 2 * Bm * Dh * 4 + 2 * Bm * Ft * 2 + Bm * D * 2 + 3 * Bm * Ft * 4
    return pl.pallas_call(
        _moe_hidden_body,
        grid_spec=pltpu.PrefetchScalarGridSpec(
            num_scalar_prefetch=2,
            grid=(D_EXPERT // Ft, N_EXPERTS),
            in_specs=[pl.BlockSpec(memory_space=pl.ANY)] * 3,
            out_specs=pl.BlockSpec(memory_space=pl.ANY),
            scratch_shapes=[pltpu.VMEM((2, 2, D, Ft), f32),
                            pltpu.VMEM((D, Ft), bf16), pltpu.VMEM((D, Ft), bf16),
                            pltpu.VMEM((2, Bm, Dh), u32), pltpu.VMEM((2, Bm, Ft), bf16),
                            pltpu.SemaphoreType.DMA((2,)),
                            pltpu.SemaphoreType.DMA((2,)), pltpu.SemaphoreType.DMA((2,))],
        ),
        out_shape=jax.ShapeDtypeStruct((P, D_EXPERT), bf16),
        compiler_params=_params(("arbitrary", "arbitrary"), est),
        name="moe_expert_hidden",
    )(blk0, nblk, w1, w3, xs)


def _moe_out_body(blk0_ref, nblk_ref, w2_hbm, hid_hbm, y_hbm, wf_ref, w2b_ref, ibuf, obuf,
                  wsem, isem, osem):
    n = pl.program_id(0)
    e = pl.program_id(1)
    n_blocks = nblk_ref[e]
    stream = _WeightStream((w2_hbm,), wf_ref, wsem, MOE_N_TILE)

    def cast(step):
        _cast_tile(stream.tile(0, step), w2b_ref)

    def compute(h):
        y = jnp.dot(h, w2b_ref[...], preferred_element_type=f32)
        half = y.shape[1] // 2
        return _pack_bf16_pair(y[:, :half].astype(bf16), y[:, half:].astype(bf16))

    _expert_step(stream, n_blocks, cast, functools.partial(
        _expert_block_loop, blk0_ref[e] * MOE_ROWS, n_blocks, hid_hbm, y_hbm, n * (MOE_N_TILE // 2),
        ibuf, obuf, isem, osem, compute))


def _moe_out(hid, w2, blk0, nblk):
    P, F = hid.shape
    D = w2.shape[2]
    Bm, Nt = MOE_ROWS, MOE_N_TILE
    assert Nt == D, "the packed output pairs feature j with j + D/2: the column tile must span all of D"
    est = 2 * F * Nt * 4 + F * Nt * 2 + 2 * Bm * F * 2 + 2 * Bm * Nt * 2 + 2 * Bm * Nt * 4
    return pl.pallas_call(
        _moe_out_body,
        grid_spec=pltpu.PrefetchScalarGridSpec(
            num_scalar_prefetch=2,
            grid=(D // Nt, N_EXPERTS),
            in_specs=[pl.BlockSpec(memory_space=pl.ANY)] * 2,
            out_specs=pl.BlockSpec(memory_space=pl.ANY),
            scratch_shapes=[pltpu.VMEM((2, 1, F, Nt), f32), pltpu.VMEM((F, Nt), bf16),
                            pltpu.VMEM((2, Bm, F), bf16), pltpu.VMEM((2, Bm, Nt // 2), u32),
                            pltpu.SemaphoreType.DMA((2,)),
                            pltpu.SemaphoreType.DMA((2,)), pltpu.SemaphoreType.DMA((2,))],
        ),
        out_shape=jax.ShapeDtypeStruct((P, D // 2), u32),
        compiler_params=_params(("arbitrary", "arbitrary"), est),
        name="moe_expert_out",
    )(blk0, nblk, w2, hid)


def _combine_body(dest_ref, h_ref, gate_ref, g_ref, y_hbm, o_ref, buf, ss_s, sem):
    R = GATHER_ROWS
    i = pl.program_id(0)
    n_steps = pl.num_programs(0)

    def issue_rows(step, slot, r0, n):
        for r in range(n):
            for k in range(TOP_K):
                _row_copy(y_hbm, dest_ref[(step * R + r0 + r) * TOP_K + k],
                          buf.at[slot, k, pl.ds(r0 + r, 1)], sem.at[slot]).start()

    @pl.when(i == 0)
    def _():
        lax.fori_loop(0, R // V7X_SUBLANES,
                      lambda c, carry: (issue_rows(0, 0, c * V7X_SUBLANES, V7X_SUBLANES), carry)[1], 0)

    slot = i % 2
    for k in range(TOP_K):
        pltpu.make_async_copy(y_hbm.at[pl.ds(0, R)], buf.at[slot, k], sem.at[slot]).wait()
    g_fin = g_ref[...]
    D = h_ref.shape[1]
    RC = V7X_SUBLANES

    def chunk(c):
        return pl.ds(pl.multiple_of(c * RC, RC), RC)

    def passes(prefetch):
        def accumulate(c, carry):
            rs = chunk(c)
            gate = gate_ref[rs, :]
            g1 = gate[:, ROUTE_GATE_LANE:ROUTE_GATE_LANE + 1]
            g2 = gate[:, ROUTE_GATE_LANE + 1:ROUTE_GATE_LANE + 2]
            y1_lo, y1_hi = _unpack_pair_f32(buf[slot, 0, rs, :])
            y2_lo, y2_hi = _unpack_pair_f32(buf[slot, 1, rs, :])
            ss = None
            for cols, y1, y2 in ((slice(0, D // 2), y1_lo, y2_lo), (slice(D // 2, D), y1_hi, y2_hi)):
                h = h_ref[rs, cols] + y1 * g1 + y2 * g2
                o_ref[rs, cols] = h
                sq = h * h
                for j in range(sq.shape[1] // V7X_LANES):
                    part = sq[:, j * V7X_LANES:(j + 1) * V7X_LANES]
                    ss = part if ss is None else ss + part
            ss_s[rs, :] = ss
            if prefetch:
                issue_rows(i + 1, 1 - slot, c * RC, RC // 2)
            return carry

        lax.fori_loop(0, R // RC, accumulate, 0, unroll=4)

        def normalize(c, carry):
            rs = chunk(c)
            ms = jnp.sum(ss_s[rs, :], axis=-1, keepdims=True) * (1.0 / D)
            o_ref[rs, :] = o_ref[rs, :] * lax.rsqrt(ms + EPS) * g_fin
            if prefetch:
                issue_rows(i + 1, 1 - slot, c * RC + RC // 2, RC // 2)
            return carry

        lax.fori_loop(0, R // RC, normalize, 0, unroll=4)

    @pl.when(i + 1 < n_steps)
    def _():
        passes(True)

    @pl.when(i + 1 == n_steps)
    def _():
        passes(False)


def _moe_combine(h, y, dest, gate, n_final):
    T, D = h.shape
    R = GATHER_ROWS
    est = 2 * TOP_K * R * D * 4 + 4 * R * D * 4 + 3 * R * D * 4
    return pl.pallas_call(
        _combine_body,
        grid_spec=pltpu.PrefetchScalarGridSpec(
            num_scalar_prefetch=1,
            grid=(T // R,),
            in_specs=[pl.BlockSpec((R, D), lambda i, dest: (i, 0)),
                      pl.BlockSpec((R, V7X_LANES), lambda i, dest: (i, 0)),
                      pl.BlockSpec((1, D), lambda i, dest: (0, 0)),
                      pl.BlockSpec(memory_space=pl.ANY)],
            out_specs=pl.BlockSpec((R, D), lambda i, dest: (i, 0)),
            scratch_shapes=[pltpu.VMEM((2, TOP_K, R, D // 2), u32), pltpu.VMEM((R, V7X_LANES), f32),
                            pltpu.SemaphoreType.DMA((2,))],
        ),
        out_shape=jax.ShapeDtypeStruct((T, D), f32),
        compiler_params=_params(("arbitrary",), est),
        name="moe_combine_norm",
    )(dest, h, gate, n_final.reshape(1, D).astype(f32), y)


def _slot_layout(rec):
    T = rec.shape[0]
    A = T * TOP_K
    eid = rec[:, ROUTE_EID_LANE:ROUTE_EID_LANE + TOP_K].astype(jnp.int32).reshape(A)
    onehot = (eid[:, None] == jnp.arange(N_EXPERTS, dtype=jnp.int32)[None, :]).astype(jnp.int32)
    csum = jnp.cumsum(onehot, axis=0)
    rank = jnp.sum(onehot * csum, axis=1) - 1
    counts = csum[-1]
    pcounts = (counts + MOE_ROWS - 1) // MOE_ROWS * MOE_ROWS
    pend = jnp.cumsum(pcounts)
    pstart = pend - pcounts
    dest = (pstart[eid] + rank).astype(jnp.int32)
    n_blocks = -(-(A + N_EXPERTS * (MOE_ROWS - 1)) // MOE_ROWS)
    P = n_blocks * MOE_ROWS
    blk0 = (pstart // MOE_ROWS).astype(jnp.int32)
    nblk = (pcounts // MOE_ROWS).astype(jnp.int32)
    pad_end = pend.astype(jnp.int32)
    pad_len = (pcounts - counts).astype(jnp.int32)
    return dest, blk0, nblk, pad_end, pad_len, P


def kernel(x, mem, n_mix, w_in, gmlp_ln_g, gmlp_ln_b, gmlp_w_s, gmlp_b_s, hgrn_lower_bounds,
           hgrn_norm_g, w_out, n_cross, n_mem, w_q_x, w_k_x, w_v_x, w_o_x, n_moe, w_group,
           b_group, w_router, b_router, w1_e, w3_e, w2_e, n_final):
    B, S, D = x.shape
    n_mem_tok = mem.shape[1]
    T = B * S
    xt = x.reshape(T, D)

    hn = _rmsnorm(xt, n_mix[0], bf16)
    z = _matmul(hn, w_in[0], bf16, 1024, 512, "mm_w_in")
    ya = _gmlp(z, gmlp_ln_g[0], gmlp_ln_b[0], gmlp_w_s[0].astype(f32), gmlp_b_s[0].astype(f32))
    yb = _hgrn(z, hgrn_lower_bounds, hgrn_norm_g[0], B, S)
    h1, h1b, h1ss = _matmul_res(xt, [ya, yb], w_out[0], 1024, 512, "mm_w_out", with_norm_inputs=True)

    mn = _rmsnorm(mem.reshape(B * n_mem_tok, D), n_mem[0], bf16)
    kx = _matmul(mn, w_k_x[0], bf16, B * n_mem_tok, 512, "mm_w_k")
    vx = _matmul(mn, w_v_x[0], bf16, B * n_mem_tok, 512, "mm_w_v")
    wq_gained = (w_q_x[0] * n_cross[0].astype(f32)[:, None]).astype(bf16)
    ox = _xattn(h1b, h1ss, wq_gained, kx, vx, B, S, n_mem_tok)
    h2 = _matmul_res(h1, [ox], w_o_x[0], 1024, 512, "mm_w_o")

    xm, rec = _router(h2, n_moe[0], w_group[0], b_group[0], w_router[0], b_router[0])
    dest, blk0, nblk, pad_end, pad_len, P = _slot_layout(rec)
    xs = _moe_dispatch(xm, dest, pad_end, pad_len, P)
    hid = _moe_hidden(xs, w1_e[0], w3_e[0], blk0, nblk)
    y = _moe_out(hid, w2_e[0], blk0, nblk)
    out = _moe_combine(h2, y, dest, rec, n_final)
    return out.reshape(B, S, D)
```

```python
import functools

import jax
import jax.numpy as jnp
from jax import lax
from jax.experimental import pallas as pl
from jax.experimental.pallas import tpu as pltpu

f32 = jnp.float32
bf16 = jnp.bfloat16

EPS = 1e-6
D_MODEL = 4096
CHUNK = 64
GMLP_HEADS = 4
GMLP_HEAD_DIM = 512
GMLP_BLOCK = 128
D_GMLP = GMLP_HEADS * GMLP_HEAD_DIM
HGRN_HEAD_DIM = 128
HGRN_HEADS = 16
D_HGRN = HGRN_HEADS * HGRN_HEAD_DIM
X_HEADS = 4
X_HEAD_DIM = 1024
N_GROUPS = 4
EXPERTS_PER_GROUP = 8
N_EXPERTS = N_GROUPS * EXPERTS_PER_GROUP
TOP_K = 2
D_EXPERT = 1024

V7X_LANES = 128
V7X_SUBLANES = 8
V7X_VMEM_LIMIT_BYTES = 60000 * 1024

MOE_ROWS = 256
MOE_F_TILE = 512
MOE_N_TILE = 4096
MOE_W_CHUNKS = 8
GATHER_ROWS = 512
HGRN_GROUP = 4
HGRN_ROWS = 512
HGRN_SUB = 128


def _params(sem, est_bytes):
    limit = min(int(est_bytes) + (8 << 20), V7X_VMEM_LIMIT_BYTES)
    return pltpu.CompilerParams(dimension_semantics=sem, vmem_limit_bytes=limit)


def _rms_body(x_ref, g_ref, o_ref):
    x = x_ref[...].astype(f32)
    ms = jnp.mean(x * x, axis=-1, keepdims=True)
    o_ref[...] = (x * lax.rsqrt(ms + EPS) * g_ref[...]).astype(o_ref.dtype)


def _rmsnorm(x, g, out_dtype, tm=256):
    M, D = x.shape
    est = 2 * tm * D * (x.dtype.itemsize + jnp.dtype(out_dtype).itemsize) + 2 * tm * D * 4
    return pl.pallas_call(
        _rms_body,
        grid=(M // tm,),
        in_specs=[pl.BlockSpec((tm, D), lambda i: (i, 0)),
                  pl.BlockSpec((1, D), lambda i: (0, 0))],
        out_specs=pl.BlockSpec((tm, D), lambda i: (i, 0)),
        out_shape=jax.ShapeDtypeStruct((M, D), out_dtype),
        compiler_params=_params(("arbitrary",), est),
        name="rmsnorm",
    )(x, g.reshape(1, D).astype(f32))


CAST_ROWS = 512


def _cast_tile(src_ref, dst_ref, dst_row0=0):
    rows = min(CAST_ROWS, src_ref.shape[0])
    n = src_ref.shape[0] // rows

    def piece(p, c):
        r = pl.multiple_of(p * rows, rows)
        d = pl.multiple_of(dst_row0 + p * rows, rows)
        dst_ref[pl.ds(d, rows), :] = src_ref[pl.ds(r, rows), :].astype(bf16)
        return c

    lax.fori_loop(0, n, piece, 0)


def _mm_body(a_ref, w_ref, o_ref, wb_ref):
    @pl.when(pl.program_id(1) == 0)
    def _():
        _cast_tile(w_ref, wb_ref)

    o_ref[...] = jnp.dot(a_ref[...], wb_ref[...], preferred_element_type=f32).astype(o_ref.dtype)


def _matmul(a, w, out_dtype, tm, tn, name):
    M, K = a.shape
    N = w.shape[1]
    osz = jnp.dtype(out_dtype).itemsize
    est = 2 * (tm * K * 2 + K * tn * 4 + tm * tn * osz) + K * tn * 2 + tm * tn * 4
    return pl.pallas_call(
        _mm_body,
        grid=(N // tn, M // tm),
        in_specs=[pl.BlockSpec((tm, K), lambda j, i: (i, 0)),
                  pl.BlockSpec((K, tn), lambda j, i: (0, j))],
        out_specs=pl.BlockSpec((tm, tn), lambda j, i: (i, j)),
        out_shape=jax.ShapeDtypeStruct((M, N), out_dtype),
        scratch_shapes=[pltpu.VMEM((K, tn), bf16)],
        compiler_params=_params(("arbitrary", "arbitrary"), est),
        name=name,
    )(a, w)


def _mm_res_body(n_parts, with_norm_inputs, r_ref, *refs):
    a_refs = refs[:n_parts]
    w_refs = refs[n_parts:2 * n_parts]
    out_refs = refs[2 * n_parts:-1]
    wb_ref = refs[-1]
    kp = w_refs[0].shape[0]

    @pl.when(pl.program_id(1) == 0)
    def _():
        for p in range(n_parts):
            _cast_tile(w_refs[p], wb_ref, p * kp)

    acc = r_ref[...]
    for p in range(n_parts):
        acc = acc + jnp.dot(a_refs[p][...], wb_ref[p * kp:(p + 1) * kp, :], preferred_element_type=f32)
    out_refs[0][...] = acc
    if with_norm_inputs:
        out_refs[1][...] = acc.astype(bf16)
        sq = acc * acc
        ss = sq[:, 0:V7X_LANES]
        for g in range(1, acc.shape[1] // V7X_LANES):
            ss = ss + sq[:, g * V7X_LANES:(g + 1) * V7X_LANES]
        out_refs[2][...] = ss


def _matmul_res(res, a_list, w, tm, tn, name, with_norm_inputs=False):
    M, N = res.shape
    n = len(a_list)
    Kp = a_list[0].shape[1]
    a_specs = [pl.BlockSpec((tm, Kp), lambda j, i: (i, 0)) for _ in range(n)]
    w_specs = [pl.BlockSpec((Kp, tn), functools.partial(lambda j, i, p: (p, j), p=p)) for p in range(n)]
    est = 2 * (n * tm * Kp * 2 + n * Kp * tn * 4 + 2 * tm * tn * 4) + n * Kp * tn * 2 + 2 * tm * tn * 4
    tile = pl.BlockSpec((tm, tn), lambda j, i: (i, j))
    out_specs, out_shape = [tile], [jax.ShapeDtypeStruct((M, N), f32)]
    if with_norm_inputs:
        est += 2 * tm * tn * 2 + 2 * tm * V7X_LANES * 4 + tm * tn * 4
        out_specs += [tile, pl.BlockSpec((tm, V7X_LANES), lambda j, i: (i, j))]
        out_shape += [jax.ShapeDtypeStruct((M, N), bf16),
                      jax.ShapeDtypeStruct((M, N // tn * V7X_LANES), f32)]
    outs = pl.pallas_call(
        functools.partial(_mm_res_body, n, with_norm_inputs),
        grid=(N // tn, M // tm),
        in_specs=[tile] + a_specs + w_specs,
        out_specs=out_specs,
        out_shape=out_shape,
        scratch_shapes=[pltpu.VMEM((n * Kp, tn), bf16)],
        compiler_params=_params(("arbitrary", "arbitrary"), est),
        name=name,
    )(res, *a_list, *([w] * n))
    return outs if with_norm_inputs else outs[0]


def _gelu(x):
    return 0.5 * x * (1.0 + lax.erf(x * 0.7071067811865476))


def _gmlp_body(u_ref, v_ref, lng_ref, lnb_ref, ws_ref, bs_ref, o_ref):
    rows = u_ref.shape[0]
    gv = _gelu(v_ref[...].astype(f32))
    mu = jnp.mean(gv, axis=-1, keepdims=True)
    xc = gv - mu
    var = jnp.mean(xc * xc, axis=-1, keepdims=True)
    vn = (xc * lax.rsqrt(var + EPS) * lng_ref[...] + lnb_ref[...]).astype(bf16)
    ci = lax.broadcasted_iota(jnp.int32, (GMLP_BLOCK, GMLP_BLOCK), 0) // CHUNK
    cj = lax.broadcasted_iota(jnp.int32, (GMLP_BLOCK, GMLP_BLOCK), 1) // CHUNK
    w = jnp.where(cj <= ci, ws_ref[0], 0.0).astype(bf16)
    bias = bs_ref[0]
    for blk in range(rows // GMLP_BLOCK):
        sl = slice(blk * GMLP_BLOCK, (blk + 1) * GMLP_BLOCK)
        s = jnp.dot(w, vn[sl], preferred_element_type=f32) + bias
        o_ref[sl, :] = (_gelu(u_ref[sl, :].astype(f32)) * s).astype(o_ref.dtype)


def _gmlp(z, ln_g, ln_b, w_s, b_s, tr=1024):
    M = z.shape[0]
    C = GMLP_HEAD_DIM
    est = 2 * 3 * tr * C * 2 + 8 * tr * C * 4
    return pl.pallas_call(
        _gmlp_body,
        grid=(M // tr, GMLP_HEADS),
        in_specs=[pl.BlockSpec((tr, C), lambda i, h: (i, h)),
                  pl.BlockSpec((tr, C), lambda i, h: (i, GMLP_HEADS + h)),
                  pl.BlockSpec((1, C), lambda i, h: (0, h)),
                  pl.BlockSpec((1, C), lambda i, h: (0, h)),
                  pl.BlockSpec((1, GMLP_BLOCK, GMLP_BLOCK), lambda i, h: (h, 0, 0)),
                  pl.BlockSpec((1, GMLP_BLOCK, 1), lambda i, h: (h, 0, 0))],
        out_specs=pl.BlockSpec((tr, C), lambda i, h: (i, h)),
        out_shape=jax.ShapeDtypeStruct((M, D_GMLP), bf16),
        compiler_params=_params(("arbitrary", "arbitrary"), est),
        name="gmlp_mixer",
    )(z, z, ln_g.reshape(1, D_GMLP), ln_b.reshape(1, D_GMLP), w_s,
      b_s.reshape(GMLP_HEADS, GMLP_BLOCK, 1))


def _split3(x):
    hi = x.astype(bf16)
    r1 = x - hi.astype(f32)
    mid = r1.astype(bf16)
    lo = (r1 - mid.astype(f32)).astype(bf16)
    return hi, mid, lo


HGRN_STRIP = 2 * CHUNK
_NT = (((1,), (1,)), ((), ()))
_TN = (((0,), (0,)), ((), ()))


def _split2(x):
    hi = x.astype(bf16)
    mid = (x - hi.astype(f32)).astype(bf16)
    return hi, mid


def _hgrn_body(q_ref, f_ref, i_ref, g_ref, lbr_ref, ng_ref, o_ref,
               st_ref, qin_s, kin_s, qex_s, kdec_s, dec_s, am_s, xq_s, xk_s, inc_s, smat_s, oacc_s):
    C, K, G = CHUNK, HGRN_HEAD_DIM, HGRN_GROUP
    rows, W = q_ref.shape
    n_chunks = rows // C

    @pl.when(pl.program_id(2) == 0)
    def _():
        st_ref[...] = jnp.zeros_like(st_ref)

    r0 = lbr_ref[0:1, :]
    r1 = lbr_ref[1:2, :]
    m = jnp.maximum(r0, r1)
    e0 = jnp.exp(r0 - m)
    e1 = jnp.exp(r1 - m)
    lb = e0 / (e0 + e1)

    ti = lax.broadcasted_iota(jnp.int32, (C, C), 0)
    si = lax.broadcasted_iota(jnp.int32, (C, C), 1)
    tril = (si <= ti).astype(bf16)

    def chunk_rows(c):
        return pl.ds(pl.multiple_of(c * C, C), C)

    def gates(c, carry):
        rs = chunk_rows(c)
        fg = lb + (1.0 - lb) * jax.nn.sigmoid(f_ref[rs, :].astype(f32))
        hi, mid, lo = _split3(jnp.log(fg))
        b = (jnp.dot(tril, hi, preferred_element_type=f32)
             + jnp.dot(tril, mid, preferred_element_type=f32)
             + jnp.dot(tril, lo, preferred_element_type=f32))
        b_mid = b[C // 2:C // 2 + 1, :]
        b_last = b[C - 1:C, :]
        q_in = jax.nn.silu(q_ref[rs, :].astype(f32)) * jnp.exp(b - b_mid)
        k_in = (1.0 - fg) * jnp.exp(b_mid - b)
        qin_s[rs, :] = q_in.astype(bf16)
        kin_s[rs, :] = k_in.astype(bf16)
        qex_s[rs, :] = (q_in * jnp.exp(b_mid)).astype(bf16)
        kdec_s[rs, :] = (k_in * jnp.exp(b_last - b_mid)).astype(bf16)
        dec_s[pl.ds(pl.multiple_of(c * 8, 8), 8), :] = jnp.broadcast_to(jnp.exp(b_last), (8, W))
        return carry

    lax.fori_loop(0, n_chunks, gates, 0, unroll=2)

    @pl.when((pl.program_id(0) == 0) & (pl.program_id(1) == 0) & (pl.program_id(2) == 0))
    def _():
        am_s[...] = jnp.zeros_like(am_s)
        xq_s[...] = jnp.zeros_like(xq_s)
        xk_s[...] = jnp.zeros_like(xk_s)

    S = HGRN_STRIP
    tt = lax.broadcasted_iota(jnp.int32, (S, S), 0)
    ss = lax.broadcasted_iota(jnp.int32, (S, S), 1)
    keep = jnp.logical_and(ss <= tt, (ss // C) == (tt // C))

    SUB = HGRN_SUB
    cps = SUB // C
    lanes = [slice(h * K, (h + 1) * K) for h in range(G)]
    subs = [slice(u * SUB, (u + 1) * SUB) for u in range(rows // SUB)]
    for h, ls in enumerate(lanes):
        for c in range(n_chunks):
            cb = slice((c % cps) * K, (c % cps + 1) * K)
            xq_s[h, c * C:(c + 1) * C, cb] = qex_s[c * C:(c + 1) * C, ls]
            xk_s[h, c * C:(c + 1) * C, cb] = kdec_s[c * C:(c + 1) * C, ls]

    scores = [[lax.dot_general(qin_s[rs, ls], kin_s[rs, ls], _NT, preferred_element_type=f32) for rs in subs]
              for ls in lanes]
    for h, ls in enumerate(lanes):
        for u, rs in enumerate(subs):
            inc_s[h, :, u * cps * K:(u + 1) * cps * K] = lax.dot_general(
                i_ref[rs, ls], xk_s[h, rs, :], _TN, preferred_element_type=f32)
    for h in range(G):
        for u in range(len(subs)):
            for r in range(SUB // S):
                sl = slice(r * S, (r + 1) * S)
                am_s[h, u, sl, sl] = jnp.where(keep, scores[h][u][sl, sl], 0.0).astype(bf16)

    for h, ls in enumerate(lanes):
        st = st_ref[h]
        for c in range(n_chunks):
            cs = slice(c * K, (c + 1) * K)
            smat_s[h, :, cs] = st.astype(bf16)
            st = st * dec_s[c * 8:c * 8 + 1, ls] + inc_s[h, :, cs]
        st_ref[h] = st

    for h, ls in enumerate(lanes):
        for u, rs in enumerate(subs):
            oacc_s[rs, ls] = (
                jnp.dot(am_s[h, u], i_ref[rs, ls], preferred_element_type=f32)
                + lax.dot_general(xq_s[h, rs, :], smat_s[h, :, u * cps * K:(u + 1) * cps * K], _NT,
                                  preferred_element_type=f32))

    ng = ng_ref[...]

    def outputs(c, carry):
        rs = chunk_rows(c)
        for h in range(G):
            ls = slice(h * K, (h + 1) * K)
            o = oacc_s[rs, ls]
            ms = jnp.mean(o * o, axis=-1, keepdims=True)
            o = o * lax.rsqrt(ms + EPS) * ng[:, ls]
            o = o * jax.nn.silu(g_ref[rs, ls].astype(f32))
            o_ref[rs, ls] = o.astype(o_ref.dtype)
        return carry

    lax.fori_loop(0, n_chunks, outputs, 0, unroll=2)


def _hgrn(z, lower_bounds, norm_g, batch, seq):
    M = z.shape[0]
    W = HGRN_GROUP * HGRN_HEAD_DIM
    ts = HGRN_ROWS
    n_s = seq // ts
    n_hg = D_HGRN // W
    col0 = 2 * D_GMLP // W

    def zspec(part):
        return pl.BlockSpec((ts, W), lambda b, hg, s, part=part: (b * n_s + s, col0 + part * n_hg + hg))

    G, K, SUB = HGRN_GROUP, HGRN_HEAD_DIM, HGRN_SUB
    n_chunks = ts // CHUNK
    est = (2 * 5 * ts * W * 2 + 4 * ts * W * 2 + ts * W * 4 + G * ts * ts * 2 + 2 * G * ts * n_chunks * K * 2
           + G * K * n_chunks * K * 6 + G * K * K * 4 + (G + 2) * ts * ts * 4)
    return pl.pallas_call(
        _hgrn_body,
        grid=(batch, n_hg, n_s),
        in_specs=[zspec(0), zspec(1), zspec(2), zspec(3),
                  pl.BlockSpec((2, W), lambda b, hg, s: (0, hg)),
                  pl.BlockSpec((1, W), lambda b, hg, s: (0, hg))],
        out_specs=pl.BlockSpec((ts, W), lambda b, hg, s: (b * n_s + s, hg)),
        out_shape=jax.ShapeDtypeStruct((M, D_HGRN), bf16),
        scratch_shapes=[pltpu.VMEM((G, K, K), f32),
                        pltpu.VMEM((ts, W), bf16),
                        pltpu.VMEM((ts, W), bf16),
                        pltpu.VMEM((ts, W), bf16),
                        pltpu.VMEM((ts, W), bf16),
                        pltpu.VMEM((n_chunks * 8, W), f32),
                        pltpu.VMEM((G, ts // SUB, SUB, SUB), bf16),
                        pltpu.VMEM((G, ts, SUB // CHUNK * K), bf16),
                        pltpu.VMEM((G, ts, SUB // CHUNK * K), bf16),
                        pltpu.VMEM((G, K, n_chunks * K), f32),
                        pltpu.VMEM((G, K, n_chunks * K), bf16),
                        pltpu.VMEM((ts, W), f32)],
        compiler_params=_params(("arbitrary", "arbitrary", "arbitrary"), est),
        name="hgrn2_mixer",
    )(z, z, z, z, lower_bounds.astype(f32), norm_g.reshape(1, D_HGRN).astype(f32))


def _xattn_body(h_ref, ss_ref, wq_ref, k_ref, v_ref, o_ref):
    D = h_ref.shape[1]
    inv_rms = lax.rsqrt(jnp.sum(ss_ref[...], axis=-1, keepdims=True) * (1.0 / D) + EPS)
    q = (jnp.dot(h_ref[...], wq_ref[...], preferred_element_type=f32) * inv_rms).astype(bf16)
    s = lax.dot_general(q, k_ref[...], (((1,), (1,)), ((), ())), preferred_element_type=f32)
    s = s * (X_HEAD_DIM ** -0.5)
    s = s - jnp.max(s, axis=-1, keepdims=True)
    p = jnp.exp(s)
    p = p / jnp.sum(p, axis=-1, keepdims=True)
    o_ref[...] = jnp.dot(p.astype(bf16), v_ref[...], preferred_element_type=f32).astype(o_ref.dtype)


def _xattn(hb, ss, wq, k, v, batch, seq, n_mem, tq=1024):
    M, D = hb.shape
    n_s = seq // tq
    dh = X_HEAD_DIM
    est = (2 * (tq * D * 2 + tq * ss.shape[1] * 4 + D * dh * 2 + 2 * n_mem * dh * 2 + tq * dh * 2)
           + 3 * tq * dh * 4)
    return pl.pallas_call(
        _xattn_body,
        grid=(X_HEADS, batch, n_s),
        in_specs=[pl.BlockSpec((tq, D), lambda h, b, s: (b * n_s + s, 0)),
                  pl.BlockSpec((tq, ss.shape[1]), lambda h, b, s: (b * n_s + s, 0)),
                  pl.BlockSpec((D, dh), lambda h, b, s: (0, h)),
                  pl.BlockSpec((n_mem, dh), lambda h, b, s: (b, h)),
                  pl.BlockSpec((n_mem, dh), lambda h, b, s: (b, h))],
        out_specs=pl.BlockSpec((tq, dh), lambda h, b, s: (b * n_s + s, h)),
        out_shape=jax.ShapeDtypeStruct((M, D), bf16),
        compiler_params=_params(("arbitrary", "arbitrary", "arbitrary"), est),
        name="xattn_qproj",
    )(hb, ss, wq, k, v)


u32 = jnp.uint32
ROUTE_GATE_LANE = 0
ROUTE_EID_LANE = 2


def _pack_bf16_pair(lo, hi):
    lo32 = lax.bitcast_convert_type(lo.astype(f32), u32) >> 16
    hi32 = lax.bitcast_convert_type(hi.astype(f32), u32) & jnp.uint32(0xFFFF0000)
    return lo32 | hi32


def _unpack_bf16_pair(w):
    lo = lax.bitcast_convert_type(w << 16, f32).astype(bf16)
    hi = lax.bitcast_convert_type(w & jnp.uint32(0xFFFF0000), f32).astype(bf16)
    return lo, hi


def _unpack_pair_f32(w):
    return (lax.bitcast_convert_type(w << 16, f32),
            lax.bitcast_convert_type(w & jnp.uint32(0xFFFF0000), f32))


def _first_lane_of(mask, lane):
    return jnp.min(jnp.where(mask, lane, V7X_LANES), axis=-1, keepdims=True)


def _router_body(h_ref, g_ref, wr_ref, br_ref, xm_ref, r_ref):
    x = h_ref[...]
    D = x.shape[1]
    ms = jnp.mean(x * x, axis=-1, keepdims=True)
    xn = x * lax.rsqrt(ms + EPS) * g_ref[...]
    xh, xm = _split2(xn)
    wh, wm = _split2(wr_ref[...])
    dot = functools.partial(jnp.dot, preferred_element_type=f32)
    nl = wh.shape[1]
    hi = dot(xh, jnp.concatenate([wh, wm], axis=1))
    logits = (dot(xm, wh) + hi[:, nl:]) + hi[:, :nl] + br_ref[...]

    xm_ref[...] = _pack_bf16_pair(xh[:, :D // 2], xh[:, D // 2:])

    lane = lax.broadcasted_iota(jnp.int32, logits.shape, 1)
    neg = jnp.float32(-jnp.inf)
    is_g = lane < N_GROUPS
    gl = jnp.where(is_g, logits, neg)
    gmax = jnp.max(gl, axis=-1, keepdims=True)
    grp = _first_lane_of(gl == gmax, lane)
    p_grp = 1.0 / jnp.sum(jnp.where(is_g, jnp.exp(logits - gmax), 0.0), axis=-1, keepdims=True)
    e0 = N_GROUPS + grp * EXPERTS_PER_GROUP
    el = jnp.where(jnp.logical_and(lane >= e0, lane < e0 + EXPERTS_PER_GROUP), logits, neg)
    m1 = jnp.max(el, axis=-1, keepdims=True)
    i1 = _first_lane_of(el == m1, lane)
    el2 = jnp.where(lane == i1, neg, el)
    m2 = jnp.max(el2, axis=-1, keepdims=True)
    i2 = _first_lane_of(el2 == m2, lane)
    t = jnp.exp(m2 - m1)
    g1 = p_grp * (1.0 / (1.0 + t))
    g2 = p_grp * (t / (1.0 + t))
    rec = jnp.where(lane == ROUTE_GATE_LANE, g1, 0.0)
    rec = jnp.where(lane == ROUTE_GATE_LANE + 1, g2, rec)
    rec = jnp.where(lane == ROUTE_EID_LANE, (i1 - N_GROUPS).astype(f32), rec)
    rec = jnp.where(lane == ROUTE_EID_LANE + 1, (i2 - N_GROUPS).astype(f32), rec)
    r_ref[...] = rec


def _router(h, n_moe, w_group, b_group, w_router, b_router, tm=512):
    M, D = h.shape
    n_log = N_GROUPS + N_EXPERTS
    wr = jnp.concatenate([w_group] + [w_router[g] for g in range(N_GROUPS)], axis=1)
    wr = jnp.pad(wr.astype(f32), ((0, 0), (0, V7X_LANES - n_log)))
    br = jnp.pad(jnp.concatenate([b_group, b_router.reshape(-1)]).astype(f32), (0, V7X_LANES - n_log))
    est = 2 * tm * D * 4 + 2 * D * V7X_LANES * 4 + 2 * tm * D * 2 + 6 * tm * D * 4
    return pl.pallas_call(
        _router_body,
        grid=(M // tm,),
        in_specs=[pl.BlockSpec((tm, D), lambda i: (i, 0)),
                  pl.BlockSpec((1, D), lambda i: (0, 0)),
                  pl.BlockSpec((D, V7X_LANES), lambda i: (0, 0)),
                  pl.BlockSpec((1, V7X_LANES), lambda i: (0, 0))],
        out_specs=[pl.BlockSpec((tm, D // 2), lambda i: (i, 0)),
                   pl.BlockSpec((tm, V7X_LANES), lambda i: (i, 0))],
        out_shape=[jax.ShapeDtypeStruct((M, D // 2), u32),
                   jax.ShapeDtypeStruct((M, V7X_LANES), f32)],
        compiler_params=_params(("arbitrary",), est),
        name="moe_router",
    )(h, n_moe.reshape(1, D).astype(f32), wr, br.reshape(1, V7X_LANES))


def _row_copy(src_hbm, row, dst, sem):
    return pltpu.make_async_copy(src_hbm.at[pl.ds(row, 1)], dst, sem)


PAD_BITS = tuple(1 << k for k in reversed(range((MOE_ROWS - 1).bit_length())))


def _dispatch_body(dest_ref, padend_ref, padlen_ref, xm_ref, xs_hbm, buf, zbuf, sem, zsem):
    R = GATHER_ROWS
    i = pl.program_id(0)
    n_steps = pl.num_programs(0)
    slot = i % 2

    def wait_slot(s):
        for _ in range(TOP_K):
            pltpu.make_async_copy(buf.at[s], xs_hbm.at[pl.ds(0, R)], sem.at[s]).wait()

    def pad_copies(fn):
        def per_expert(e, c):
            ln = padlen_ref[e]
            end = padend_ref[e]
            for bit in PAD_BITS:
                rows_per_copy = bit if bit >= V7X_SUBLANES else 1
                for j in range(bit // rows_per_copy):
                    @pl.when((ln & bit) != 0)
                    def _(bit=bit, end=end, j=j, n=rows_per_copy):
                        start = end - bit + j * n
                        if n > 1:
                            start = pl.multiple_of(start, V7X_SUBLANES)
                        fn(pltpu.make_async_copy(zbuf.at[pl.ds(0, n)], xs_hbm.at[pl.ds(start, n)], zsem))
                end = end - (ln & bit)
            return c
        lax.fori_loop(0, N_EXPERTS, per_expert, 0)

    @pl.when(i == 0)
    def _():
        zbuf[...] = jnp.zeros_like(zbuf)
        pad_copies(lambda cp: cp.start())

    @pl.when(i >= 2)
    def _():
        wait_slot(slot)

    buf[slot] = xm_ref[...]

    def issue(r, c):
        for k in range(TOP_K):
            d = dest_ref[(i * R + r) * TOP_K + k]
            pltpu.make_async_copy(buf.at[slot, pl.ds(r, 1)], xs_hbm.at[pl.ds(d, 1)], sem.at[slot]).start()
        return c

    lax.fori_loop(0, R, issue, 0, unroll=8)

    @pl.when(i == n_steps - 1)
    def _():
        if n_steps >= 2:
            wait_slot(1 - slot)
        wait_slot(slot)
        pad_copies(lambda cp: cp.wait())


def _moe_dispatch(xm, dest, pad_end, pad_len, P):
    T, Dh = xm.shape
    R = GATHER_ROWS
    est = 2 * R * Dh * 4 + 2 * R * Dh * 4 + PAD_BITS[0] * Dh * 4
    return pl.pallas_call(
        _dispatch_body,
        grid_spec=pltpu.PrefetchScalarGridSpec(
            num_scalar_prefetch=3,
            grid=(T // R,),
            in_specs=[pl.BlockSpec((R, Dh), lambda i, dest, pend, plen: (i, 0))],
            out_specs=pl.BlockSpec(memory_space=pl.ANY),
            scratch_shapes=[pltpu.VMEM((2, R, Dh), u32), pltpu.VMEM((PAD_BITS[0], Dh), u32),
                            pltpu.SemaphoreType.DMA((2,)), pltpu.SemaphoreType.DMA],
        ),
        out_shape=jax.ShapeDtypeStruct((P, Dh), u32),
        compiler_params=_params(("arbitrary",), est),
        name="moe_dispatch",
    )(dest, pad_end, pad_len, xm)


class _WeightStream:
    def __init__(self, w_hbms, wf_ref, wsem, col_tile):
        self.w_hbms, self.wf, self.wsem, self.ct = w_hbms, wf_ref, wsem, col_tile
        self.rows = wf_ref.shape[2]
        self.cr = self.rows // MOE_W_CHUNKS

    def _chunk(self, m, c, step, slot):
        e = step % N_EXPERTS
        t = step // N_EXPERTS
        r = pl.ds(pl.multiple_of(c * self.cr, self.cr), self.cr)
        cols = pl.ds(pl.multiple_of(t * self.ct, V7X_LANES), self.ct)
        return pltpu.make_async_copy(self.w_hbms[m].at[e, r, cols], self.wf.at[slot, m, r], self.wsem.at[slot])

    def start_chunk(self, c, step):
        for m in range(len(self.w_hbms)):
            self._chunk(m, c, step, step % 2).start()

    def start_chunks(self, c0, c1, step):
        lax.fori_loop(c0, c1, lambda c, carry: (self.start_chunk(c, step), carry)[1], 0)

    def wait(self, step):
        slot = step % 2
        for m in range(len(self.w_hbms)):
            for c in range(MOE_W_CHUNKS):
                self._chunk(m, c, step, slot).wait()

    def tile(self, m, step):
        return self.wf.at[step % 2, m]


def _expert_block_loop(row0, n_blocks, in_hbm, out_hbm, out_col0, ibuf, obuf, isem, osem, compute,
                       per_block):
    Bm = MOE_ROWS
    out_cols = obuf.shape[2]

    def rows(j):
        return pl.ds(pl.multiple_of(row0 + j * Bm, Bm), Bm)

    def in_copy(j, slot):
        return pltpu.make_async_copy(in_hbm.at[rows(j)], ibuf.at[slot], isem.at[slot])

    def out_copy(j, slot):
        return pltpu.make_async_copy(
            obuf.at[slot], out_hbm.at[rows(j), pl.ds(pl.multiple_of(out_col0, V7X_LANES), out_cols)],
            osem.at[slot])

    if per_block is None:
        @pl.when(n_blocks > 0)
        def _():
            in_copy(0, 0).start()
        return

    def block(j, carry):
        slot = j % 2

        @pl.when(j + 1 < n_blocks)
        def _():
            in_copy(j + 1, 1 - slot).start()

        per_block(j)
        in_copy(j, slot).wait()

        @pl.when(j >= 2)
        def _():
            out_copy(j - 2, slot).wait()

        obuf[slot] = compute(ibuf[slot])
        out_copy(j, slot).start()
        return carry

    lax.fori_loop(0, n_blocks, block, 0)

    @pl.when(n_blocks >= 2)
    def _():
        out_copy(n_blocks - 2, n_blocks % 2).wait()

    @pl.when(n_blocks >= 1)
    def _():
        out_copy(n_blocks - 1, (n_blocks - 1) % 2).wait()


def _expert_step(stream, n_blocks, cast, run_blocks):
    step = pl.program_id(0) * N_EXPERTS + pl.program_id(1)
    n_steps = pl.num_programs(0) * N_EXPERTS
    has_next = step + 1 < n_steps

    @pl.when(step == 0)
    def _():
        stream.start_chunks(0, MOE_W_CHUNKS, step)

    run_blocks(None)
    stream.wait(step)

    @pl.when(n_blocks > 0)
    def _():
        cast(step)

    spread = jnp.maximum((n_blocks + 1) // 2, 1)
    per = (MOE_W_CHUNKS + spread - 1) // spread

    @pl.when(jnp.logical_and(has_next, n_blocks == 0))
    def _():
        stream.start_chunks(0, MOE_W_CHUNKS, step + 1)

    def per_block(j):
        @pl.when(has_next)
        def _():
            stream.start_chunks(jnp.minimum(j * per, MOE_W_CHUNKS), jnp.minimum((j + 1) * per, MOE_W_CHUNKS),
                                step + 1)

    run_blocks(per_block)


def _moe_hidden_body(blk0_ref, nblk_ref, w1_hbm, w3_hbm, xs_hbm, hid_hbm,
                     wf_ref, w1b_ref, w3b_ref, ibuf, obuf, wsem, isem, osem):
    f = pl.program_id(0)
    e = pl.program_id(1)
    n_blocks = nblk_ref[e]
    half = ibuf.shape[2]
    stream = _WeightStream((w1_hbm, w3_hbm), wf_ref, wsem, MOE_F_TILE)

    def cast(step):
        _cast_tile(stream.tile(0, step), w1b_ref)
        _cast_tile(stream.tile(1, step), w3b_ref)

    def compute(xw):
        x_lo, x_hi = _unpack_bf16_pair(xw)

        def proj(wb_ref):
            return (jnp.dot(x_lo, wb_ref[:half, :], preferred_element_type=f32)
                    + jnp.dot(x_hi, wb_ref[half:, :], preferred_element_type=f32))

        return (jax.nn.silu(proj(w1b_ref)) * proj(w3b_ref)).astype(obuf.dtype)

    _expert_step(stream, n_blocks, cast, functools.partial(
        _expert_block_loop, blk0_ref[e] * MOE_ROWS, n_blocks, xs_hbm, hid_hbm, f * MOE_F_TILE,
        ibuf, obuf, isem, osem, compute))


def _moe_hidden(xs, w1, w3, blk0, nblk):
    P, Dh = xs.shape
    D = w1.shape[1]
    Bm, Ft = MOE_ROWS, MOE_F_TILE
    est = 2 * 2 * D * Ft * 4 + 2 * D * Ft * 2 + 2 * Bm * Dh * 4 + 2 * Bm * Ft * 2 + Bm * D * 2 + 3 * Bm * Ft * 4
    return pl.pallas_call(
        _moe_hidden_body,
        grid_spec=pltpu.PrefetchScalarGridSpec(
            num_scalar_prefetch=2,
            grid=(D_EXPERT // Ft, N_EXPERTS),
            in_specs=[pl.BlockSpec(memory_space=pl.ANY)] * 3,
            out_specs=pl.BlockSpec(memory_space=pl.ANY),
            scratch_shapes=[pltpu.VMEM((2, 2, D, Ft), f32),
                            pltpu.VMEM((D, Ft), bf16), pltpu.VMEM((D, Ft), bf16),
                            pltpu.VMEM((2, Bm, Dh), u32), pltpu.VMEM((2, Bm, Ft), bf16),
                            pltpu.SemaphoreType.DMA((2,)),
                            pltpu.SemaphoreType.DMA((2,)), pltpu.SemaphoreType.DMA((2,))],
        ),
        out_shape=jax.ShapeDtypeStruct((P, D_EXPERT), bf16),
        compiler_params=_params(("arbitrary", "arbitrary"), est),
        name="moe_expert_hidden",
    )(blk0, nblk, w1, w3, xs)


def _moe_out_body(blk0_ref, nblk_ref, w2_hbm, hid_hbm, y_hbm, wf_ref, w2b_ref, ibuf, obuf,
                  wsem, isem, osem):
    n = pl.program_id(0)
    e = pl.program_id(1)
    n_blocks = nblk_ref[e]
    stream = _WeightStream((w2_hbm,), wf_ref, wsem, MOE_N_TILE)

    def cast(step):
        _cast_tile(stream.tile(0, step), w2b_ref)

    def compute(h):
        y = jnp.dot(h, w2b_ref[...], preferred_element_type=f32)
        half = y.shape[1] // 2
        return _pack_bf16_pair(y[:, :half].astype(bf16), y[:, half:].astype(bf16))

    _expert_step(stream, n_blocks, cast, functools.partial(
        _expert_block_loop, blk0_ref[e] * MOE_ROWS, n_blocks, hid_hbm, y_hbm, n * (MOE_N_TILE // 2),
        ibuf, obuf, isem, osem, compute))


def _moe_out(hid, w2, blk0, nblk):
    P, F = hid.shape
    D = w2.shape[2]
    Bm, Nt = MOE_ROWS, MOE_N_TILE
    assert Nt == D, "the packed output pairs feature j with j + D/2: the column tile must span all of D"
    est = 2 * F * Nt * 4 + F * Nt * 2 + 2 * Bm * F * 2 + 2 * Bm * Nt * 2 + 2 * Bm * Nt * 4
    return pl.pallas_call(
        _moe_out_body,
        grid_spec=pltpu.PrefetchScalarGridSpec(
            num_scalar_prefetch=2,
            grid=(D // Nt, N_EXPERTS),
            in_specs=[pl.BlockSpec(memory_space=pl.ANY)] * 2,
            out_specs=pl.BlockSpec(memory_space=pl.ANY),
            scratch_shapes=[pltpu.VMEM((2, 1, F, Nt), f32), pltpu.VMEM((F, Nt), bf16),
                            pltpu.VMEM((2, Bm, F), bf16), pltpu.VMEM((2, Bm, Nt // 2), u32),
                            pltpu.SemaphoreType.DMA((2,)),
                            pltpu.SemaphoreType.DMA((2,)), pltpu.SemaphoreType.DMA((2,))],
        ),
        out_shape=jax.ShapeDtypeStruct((P, D // 2), u32),
        compiler_params=_params(("arbitrary", "arbitrary"), est),
        name="moe_expert_out",
    )(blk0, nblk, w2, hid)


def _combine_body(dest_ref, h_ref, gate_ref, g_ref, y_hbm, o_ref, buf, ss_s, sem):
    R = GATHER_ROWS
    i = pl.program_id(0)
    n_steps = pl.num_programs(0)

    def issue_rows(step, slot, r0, n):
        for r in range(n):
            for k in range(TOP_K):
                _row_copy(y_hbm, dest_ref[(step * R + r0 + r) * TOP_K + k],
                          buf.at[slot, k, pl.ds(r0 + r, 1)], sem.at[slot]).start()

    @pl.when(i == 0)
    def _():
        lax.fori_loop(0, R // V7X_SUBLANES,
                      lambda c, carry: (issue_rows(0, 0, c * V7X_SUBLANES, V7X_SUBLANES), carry)[1], 0)

    slot = i % 2
    for k in range(TOP_K):
        pltpu.make_async_copy(y_hbm.at[pl.ds(0, R)], buf.at[slot, k], sem.at[slot]).wait()
    g_fin = g_ref[...]
    D = h_ref.shape[1]
    RC = V7X_SUBLANES

    def chunk(c):
        return pl.ds(pl.multiple_of(c * RC, RC), RC)

    def passes(prefetch):
        def accumulate(c, carry):
            rs = chunk(c)
            gate = gate_ref[rs, :]
            g1 = gate[:, ROUTE_GATE_LANE:ROUTE_GATE_LANE + 1]
            g2 = gate[:, ROUTE_GATE_LANE + 1:ROUTE_GATE_LANE + 2]
            y1_lo, y1_hi = _unpack_pair_f32(buf[slot, 0, rs, :])
            y2_lo, y2_hi = _unpack_pair_f32(buf[slot, 1, rs, :])
            ss = None
            for cols, y1, y2 in ((slice(0, D // 2), y1_lo, y2_lo), (slice(D // 2, D), y1_hi, y2_hi)):
                h = h_ref[rs, cols] + y1 * g1 + y2 * g2
                o_ref[rs, cols] = h
                sq = h * h
                for j in range(sq.shape[1] // V7X_LANES):
                    part = sq[:, j * V7X_LANES:(j + 1) * V7X_LANES]
                    ss = part if ss is None else ss + part
            ss_s[rs, :] = ss
            if prefetch:
                issue_rows(i + 1, 1 - slot, c * RC, RC // 2)
            return carry

        lax.fori_loop(0, R // RC, accumulate, 0, unroll=4)

        def normalize(c, carry):
            rs = chunk(c)
            ms = jnp.sum(ss_s[rs, :], axis=-1, keepdims=True) * (1.0 / D)
            o_ref[rs, :] = o_ref[rs, :] * lax.rsqrt(ms + EPS) * g_fin
            if prefetch:
                issue_rows(i + 1, 1 - slot, c * RC + RC // 2, RC // 2)
            return carry

        lax.fori_loop(0, R // RC, normalize, 0, unroll=4)

    @pl.when(i + 1 < n_steps)
    def _():
        passes(True)

    @pl.when(i + 1 == n_steps)
    def _():
        passes(False)


def _moe_combine(h, y, dest, gate, n_final):
    T, D = h.shape
    R = GATHER_ROWS
    est = 2 * TOP_K * R * D * 4 + 4 * R * D * 4 + 3 * R * D * 4
    return pl.pallas_call(
        _combine_body,
        grid_spec=pltpu.PrefetchScalarGridSpec(
            num_scalar_prefetch=1,
            grid=(T // R,),
            in_specs=[pl.BlockSpec((R, D), lambda i, dest: (i, 0)),
                      pl.BlockSpec((R, V7X_LANES), lambda i, dest: (i, 0)),
                      pl.BlockSpec((1, D), lambda i, dest: (0, 0)),
                      pl.BlockSpec(memory_space=pl.ANY)],
            out_specs=pl.BlockSpec((R, D), lambda i, dest: (i, 0)),
            scratch_shapes=[pltpu.VMEM((2, TOP_K, R, D // 2), u32), pltpu.VMEM((R, V7X_LANES), f32),
                            pltpu.SemaphoreType.DMA((2,))],
        ),
        out_shape=jax.ShapeDtypeStruct((T, D), f32),
        compiler_params=_params(("arbitrary",), est),
        name="moe_combine_norm",
    )(dest, h, gate, n_final.reshape(1, D).astype(f32), y)


def _slot_layout(rec):
    T = rec.shape[0]
    A = T * TOP_K
    eid = rec[:, ROUTE_EID_LANE:ROUTE_EID_LANE + TOP_K].astype(jnp.int32).reshape(A)
    onehot = (eid[:, None] == jnp.arange(N_EXPERTS, dtype=jnp.int32)[None, :]).astype(jnp.int32)
    csum = jnp.cumsum(onehot, axis=0)
    rank = jnp.sum(onehot * csum, axis=1) - 1
    counts = csum[-1]
    pcounts = (counts + MOE_ROWS - 1) // MOE_ROWS * MOE_ROWS
    pend = jnp.cumsum(pcounts)
    pstart = pend - pcounts
    dest = (pstart[eid] + rank).astype(jnp.int32)
    n_blocks = -(-(A + N_EXPERTS * (MOE_ROWS - 1)) // MOE_ROWS)
    P = n_blocks * MOE_ROWS
    blk0 = (pstart // MOE_ROWS).astype(jnp.int32)
    nblk = (pcounts // MOE_ROWS).astype(jnp.int32)
    pad_end = pend.astype(jnp.int32)
    pad_len = (pcounts - counts).astype(jnp.int32)
    return dest, blk0, nblk, pad_end, pad_len, P


def kernel(x, mem, n_mix, w_in, gmlp_ln_g, gmlp_ln_b, gmlp_w_s, gmlp_b_s, hgrn_lower_bounds,
           hgrn_norm_g, w_out, n_cross, n_mem, w_q_x, w_k_x, w_v_x, w_o_x, n_moe, w_group,
           b_group, w_router, b_router, w1_e, w3_e, w2_e, n_final):
    B, S, D = x.shape
    n_mem_tok = mem.shape[1]
    T = B * S
    xt = x.reshape(T, D)

    hn = _rmsnorm(xt, n_mix[0], bf16)
    z = _matmul(hn, w_in[0], bf16, 1024, 768, "mm_w_in")
    ya = _gmlp(z, gmlp_ln_g[0], gmlp_ln_b[0], gmlp_w_s[0].astype(f32), gmlp_b_s[0].astype(f32))
    yb = _hgrn(z, hgrn_lower_bounds, hgrn_norm_g[0], B, S)
    h1, h1b, h1ss = _matmul_res(xt, [ya, yb], w_out[0], 1024, 512, "mm_w_out", with_norm_inputs=True)

    mn = _rmsnorm(mem.reshape(B * n_mem_tok, D), n_mem[0], bf16)
    kx = _matmul(mn, w_k_x[0], bf16, B * n_mem_tok, 512, "mm_w_k")
    vx = _matmul(mn, w_v_x[0], bf16, B * n_mem_tok, 512, "mm_w_v")
    wq_gained = (w_q_x[0] * n_cross[0].astype(f32)[:, None]).astype(bf16)
    ox = _xattn(h1b, h1ss, wq_gained, kx, vx, B, S, n_mem_tok)
    h2 = _matmul_res(h1, [ox], w_o_x[0], 1024, 512, "mm_w_o")

    xm, rec = _router(h2, n_moe[0], w_group[0], b_group[0], w_router[0], b_router[0])
    dest, blk0, nblk, pad_end, pad_len, P = _slot_layout(rec)
    xs = _moe_dispatch(xm, dest, pad_end, pad_len, P)
    hid = _moe_hidden(xs, w1_e[0], w3_e[0], blk0, nblk)
    y = _moe_out(hid, w2_e[0], blk0, nblk)
    out = _moe_combine(h2, y, dest, rec, n_final)
    return out.reshape(B, S, D)
```

```python
import functools

import jax
import jax.numpy as jnp
from jax import lax
from jax.experimental import pallas as pl
from jax.experimental.pallas import tpu as pltpu

f32 = jnp.float32
bf16 = jnp.bfloat16

EPS = 1e-6
D_MODEL = 4096
CHUNK = 64
GMLP_HEADS = 4
GMLP_HEAD_DIM = 512
GMLP_BLOCK = 128
D_GMLP = GMLP_HEADS * GMLP_HEAD_DIM
HGRN_HEAD_DIM = 128
HGRN_HEADS = 16
D_HGRN = HGRN_HEADS * HGRN_HEAD_DIM
X_HEADS = 4
X_HEAD_DIM = 1024
N_GROUPS = 4
EXPERTS_PER_GROUP = 8
N_EXPERTS = N_GROUPS * EXPERTS_PER_GROUP
TOP_K = 2
D_EXPERT = 1024

V7X_LANES = 128
V7X_SUBLANES = 8
V7X_VMEM_LIMIT_BYTES = 60000 * 1024

MOE_ROWS = 256
MOE_F_TILE = 512
MOE_N_TILE = 4096
MOE_W_CHUNKS = 8
GATHER_ROWS = 512
HGRN_GROUP = 4
HGRN_ROWS = 512
HGRN_SUB = 128


def _params(sem, est_bytes):
    limit = min(int(est_bytes) + (8 << 20), V7X_VMEM_LIMIT_BYTES)
    return pltpu.CompilerParams(dimension_semantics=sem, vmem_limit_bytes=limit)


def _rms_body(x_ref, g_ref, o_ref):
    x = x_ref[...].astype(f32)
    ms = jnp.mean(x * x, axis=-1, keepdims=True)
    o_ref[...] = (x * lax.rsqrt(ms + EPS) * g_ref[...]).astype(o_ref.dtype)


def _rmsnorm(x, g, out_dtype, tm=256):
    M, D = x.shape
    est = 2 * tm * D * (x.dtype.itemsize + jnp.dtype(out_dtype).itemsize) + 2 * tm * D * 4
    return pl.pallas_call(
        _rms_body,
        grid=(M // tm,),
        in_specs=[pl.BlockSpec((tm, D), lambda i: (i, 0)),
                  pl.BlockSpec((1, D), lambda i: (0, 0))],
        out_specs=pl.BlockSpec((tm, D), lambda i: (i, 0)),
        out_shape=jax.ShapeDtypeStruct((M, D), out_dtype),
        compiler_params=_params(("arbitrary",), est),
        name="rmsnorm",
    )(x, g.reshape(1, D).astype(f32))


CAST_ROWS = 512


def _cast_tile(src_ref, dst_ref, dst_row0=0):
    rows = min(CAST_ROWS, src_ref.shape[0])
    n = src_ref.shape[0] // rows

    def piece(p, c):
        r = pl.multiple_of(p * rows, rows)
        d = pl.multiple_of(dst_row0 + p * rows, rows)
        dst_ref[pl.ds(d, rows), :] = src_ref[pl.ds(r, rows), :].astype(bf16)
        return c

    lax.fori_loop(0, n, piece, 0)


def _mm_body(a_ref, w_ref, o_ref, wb_ref):
    @pl.when(pl.program_id(1) == 0)
    def _():
        _cast_tile(w_ref, wb_ref)

    o_ref[...] = jnp.dot(a_ref[...], wb_ref[...], preferred_element_type=f32).astype(o_ref.dtype)


def _matmul(a, w, out_dtype, tm, tn, name):
    M, K = a.shape
    N = w.shape[1]
    osz = jnp.dtype(out_dtype).itemsize
    est = 2 * (tm * K * 2 + K * tn * 4 + tm * tn * osz) + K * tn * 2 + tm * tn * 4
    return pl.pallas_call(
        _mm_body,
        grid=(N // tn, M // tm),
        in_specs=[pl.BlockSpec((tm, K), lambda j, i: (i, 0)),
                  pl.BlockSpec((K, tn), lambda j, i: (0, j))],
        out_specs=pl.BlockSpec((tm, tn), lambda j, i: (i, j)),
        out_shape=jax.ShapeDtypeStruct((M, N), out_dtype),
        scratch_shapes=[pltpu.VMEM((K, tn), bf16)],
        compiler_params=_params(("arbitrary", "arbitrary"), est),
        name=name,
    )(a, w)


def _mm_res_body(n_parts, with_norm_inputs, r_ref, *refs):
    a_refs = refs[:n_parts]
    w_refs = refs[n_parts:2 * n_parts]
    out_refs = refs[2 * n_parts:-1]
    wb_ref = refs[-1]
    kp = w_refs[0].shape[0]

    @pl.when(pl.program_id(1) == 0)
    def _():
        for p in range(n_parts):
            _cast_tile(w_refs[p], wb_ref, p * kp)

    acc = r_ref[...]
    for p in range(n_parts):
        acc = acc + jnp.dot(a_refs[p][...], wb_ref[p * kp:(p + 1) * kp, :], preferred_element_type=f32)
    out_refs[0][...] = acc
    if with_norm_inputs:
        out_refs[1][...] = acc.astype(bf16)
        sq = acc * acc
        ss = sq[:, 0:V7X_LANES]
        for g in range(1, acc.shape[1] // V7X_LANES):
            ss = ss + sq[:, g * V7X_LANES:(g + 1) * V7X_LANES]
        out_refs[2][...] = ss


def _matmul_res(res, a_list, w, tm, tn, name, with_norm_inputs=False):
    M, N = res.shape
    n = len(a_list)
    Kp = a_list[0].shape[1]
    a_specs = [pl.BlockSpec((tm, Kp), lambda j, i: (i, 0)) for _ in range(n)]
    w_specs = [pl.BlockSpec((Kp, tn), functools.partial(lambda j, i, p: (p, j), p=p)) for p in range(n)]
    est = 2 * (n * tm * Kp * 2 + n * Kp * tn * 4 + 2 * tm * tn * 4) + n * Kp * tn * 2 + 2 * tm * tn * 4
    tile = pl.BlockSpec((tm, tn), lambda j, i: (i, j))
    out_specs, out_shape = [tile], [jax.ShapeDtypeStruct((M, N), f32)]
    if with_norm_inputs:
        est += 2 * tm * tn * 2 + 2 * tm * V7X_LANES * 4 + tm * tn * 4
        out_specs += [tile, pl.BlockSpec((tm, V7X_LANES), lambda j, i: (i, j))]
        out_shape += [jax.ShapeDtypeStruct((M, N), bf16),
                      jax.ShapeDtypeStruct((M, N // tn * V7X_LANES), f32)]
    outs = pl.pallas_call(
        functools.partial(_mm_res_body, n, with_norm_inputs),
        grid=(N // tn, M // tm),
        in_specs=[tile] + a_specs + w_specs,
        out_specs=out_specs,
        out_shape=out_shape,
        scratch_shapes=[pltpu.VMEM((n * Kp, tn), bf16)],
        compiler_params=_params(("arbitrary", "arbitrary"), est),
        name=name,
    )(res, *a_list, *([w] * n))
    return outs if with_norm_inputs else outs[0]


def _gelu(x):
    return 0.5 * x * (1.0 + lax.erf(x * 0.7071067811865476))


def _gmlp_body(u_ref, v_ref, lng_ref, lnb_ref, ws_ref, bs_ref, o_ref):
    rows = u_ref.shape[0]
    gv = _gelu(v_ref[...].astype(f32))
    mu = jnp.mean(gv, axis=-1, keepdims=True)
    xc = gv - mu
    var = jnp.mean(xc * xc, axis=-1, keepdims=True)
    vn = (xc * lax.rsqrt(var + EPS) * lng_ref[...] + lnb_ref[...]).astype(bf16)
    ci = lax.broadcasted_iota(jnp.int32, (GMLP_BLOCK, GMLP_BLOCK), 0) // CHUNK
    cj = lax.broadcasted_iota(jnp.int32, (GMLP_BLOCK, GMLP_BLOCK), 1) // CHUNK
    w = jnp.where(cj <= ci, ws_ref[0], 0.0).astype(bf16)
    bias = bs_ref[0]
    for blk in range(rows // GMLP_BLOCK):
        sl = slice(blk * GMLP_BLOCK, (blk + 1) * GMLP_BLOCK)
        s = jnp.dot(w, vn[sl], preferred_element_type=f32) + bias
        o_ref[sl, :] = (_gelu(u_ref[sl, :].astype(f32)) * s).astype(o_ref.dtype)


def _gmlp(z, ln_g, ln_b, w_s, b_s, tr=1024):
    M = z.shape[0]
    C = GMLP_HEAD_DIM
    est = 2 * 3 * tr * C * 2 + 8 * tr * C * 4
    return pl.pallas_call(
        _gmlp_body,
        grid=(M // tr, GMLP_HEADS),
        in_specs=[pl.BlockSpec((tr, C), lambda i, h: (i, h)),
                  pl.BlockSpec((tr, C), lambda i, h: (i, GMLP_HEADS + h)),
                  pl.BlockSpec((1, C), lambda i, h: (0, h)),
                  pl.BlockSpec((1, C), lambda i, h: (0, h)),
                  pl.BlockSpec((1, GMLP_BLOCK, GMLP_BLOCK), lambda i, h: (h, 0, 0)),
                  pl.BlockSpec((1, GMLP_BLOCK, 1), lambda i, h: (h, 0, 0))],
        out_specs=pl.BlockSpec((tr, C), lambda i, h: (i, h)),
        out_shape=jax.ShapeDtypeStruct((M, D_GMLP), bf16),
        compiler_params=_params(("arbitrary", "arbitrary"), est),
        name="gmlp_mixer",
    )(z, z, ln_g.reshape(1, D_GMLP), ln_b.reshape(1, D_GMLP), w_s,
      b_s.reshape(GMLP_HEADS, GMLP_BLOCK, 1))


def _split3(x):
    hi = x.astype(bf16)
    r1 = x - hi.astype(f32)
    mid = r1.astype(bf16)
    lo = (r1 - mid.astype(f32)).astype(bf16)
    return hi, mid, lo


HGRN_STRIP = 2 * CHUNK
_NT = (((1,), (1,)), ((), ()))
_TN = (((0,), (0,)), ((), ()))


def _split2(x):
    hi = x.astype(bf16)
    mid = (x - hi.astype(f32)).astype(bf16)
    return hi, mid


def _hgrn_body(q_ref, f_ref, i_ref, g_ref, lbr_ref, ng_ref, o_ref,
               st_ref, qin_s, kin_s, qex_s, kdec_s, dec_s, am_s, xq_s, xk_s, inc_s, smat_s, oacc_s):
    C, K, G = CHUNK, HGRN_HEAD_DIM, HGRN_GROUP
    rows, W = q_ref.shape
    n_chunks = rows // C

    @pl.when(pl.program_id(2) == 0)
    def _():
        st_ref[...] = jnp.zeros_like(st_ref)

    r0 = lbr_ref[0:1, :]
    r1 = lbr_ref[1:2, :]
    m = jnp.maximum(r0, r1)
    e0 = jnp.exp(r0 - m)
    e1 = jnp.exp(r1 - m)
    lb = e0 / (e0 + e1)

    ti = lax.broadcasted_iota(jnp.int32, (C, C), 0)
    si = lax.broadcasted_iota(jnp.int32, (C, C), 1)
    tril = (si <= ti).astype(bf16)

    def chunk_rows(c):
        return pl.ds(pl.multiple_of(c * C, C), C)

    def gates(c, carry):
        rs = chunk_rows(c)
        fg = lb + (1.0 - lb) * jax.nn.sigmoid(f_ref[rs, :].astype(f32))
        hi, mid, lo = _split3(jnp.log(fg))
        b = (jnp.dot(tril, hi, preferred_element_type=f32)
             + jnp.dot(tril, mid, preferred_element_type=f32)
             + jnp.dot(tril, lo, preferred_element_type=f32))
        b_mid = b[C // 2:C // 2 + 1, :]
        b_last = b[C - 1:C, :]
        q_in = jax.nn.silu(q_ref[rs, :].astype(f32)) * jnp.exp(b - b_mid)
        k_in = (1.0 - fg) * jnp.exp(b_mid - b)
        qin_s[rs, :] = q_in.astype(bf16)
        kin_s[rs, :] = k_in.astype(bf16)
        qex_s[rs, :] = (q_in * jnp.exp(b_mid)).astype(bf16)
        kdec_s[rs, :] = (k_in * jnp.exp(b_last - b_mid)).astype(bf16)
        dec_s[pl.ds(pl.multiple_of(c * 8, 8), 8), :] = jnp.broadcast_to(jnp.exp(b_last), (8, W))
        return carry

    lax.fori_loop(0, n_chunks, gates, 0, unroll=2)

    @pl.when((pl.program_id(0) == 0) & (pl.program_id(1) == 0) & (pl.program_id(2) == 0))
    def _():
        am_s[...] = jnp.zeros_like(am_s)
        xq_s[...] = jnp.zeros_like(xq_s)
        xk_s[...] = jnp.zeros_like(xk_s)

    S = HGRN_STRIP
    tt = lax.broadcasted_iota(jnp.int32, (S, S), 0)
    ss = lax.broadcasted_iota(jnp.int32, (S, S), 1)
    keep = jnp.logical_and(ss <= tt, (ss // C) == (tt // C))

    SUB = HGRN_SUB
    cps = SUB // C
    lanes = [slice(h * K, (h + 1) * K) for h in range(G)]
    subs = [slice(u * SUB, (u + 1) * SUB) for u in range(rows // SUB)]
    for h, ls in enumerate(lanes):
        for c in range(n_chunks):
            cb = slice((c % cps) * K, (c % cps + 1) * K)
            xq_s[h, c * C:(c + 1) * C, cb] = qex_s[c * C:(c + 1) * C, ls]
            xk_s[h, c * C:(c + 1) * C, cb] = kdec_s[c * C:(c + 1) * C, ls]

    scores = [[lax.dot_general(qin_s[rs, ls], kin_s[rs, ls], _NT, preferred_element_type=f32) for rs in subs]
              for ls in lanes]
    for h, ls in enumerate(lanes):
        for u, rs in enumerate(subs):
            inc_s[h, :, u * cps * K:(u + 1) * cps * K] = lax.dot_general(
                i_ref[rs, ls], xk_s[h, rs, :], _TN, preferred_element_type=f32)
    for h in range(G):
        for u in range(len(subs)):
            for r in range(SUB // S):
                sl = slice(r * S, (r + 1) * S)
                am_s[h, u, sl, sl] = jnp.where(keep, scores[h][u][sl, sl], 0.0).astype(bf16)

    for h, ls in enumerate(lanes):
        st = st_ref[h]
        for c in range(n_chunks):
            cs = slice(c * K, (c + 1) * K)
            smat_s[h, :, cs] = st.astype(bf16)
            st = st * dec_s[c * 8:c * 8 + 1, ls] + inc_s[h, :, cs]
        st_ref[h] = st

    for h, ls in enumerate(lanes):
        for u, rs in enumerate(subs):
            oacc_s[rs, ls] = (
                jnp.dot(am_s[h, u], i_ref[rs, ls], preferred_element_type=f32)
                + lax.dot_general(xq_s[h, rs, :], smat_s[h, :, u * cps * K:(u + 1) * cps * K], _NT,
                                  preferred_element_type=f32))

    ng = ng_ref[...]

    def outputs(c, carry):
        rs = chunk_rows(c)
        for h in range(G):
            ls = slice(h * K, (h + 1) * K)
            o = oacc_s[rs, ls]
            ms = jnp.mean(o * o, axis=-1, keepdims=True)
            o = o * lax.rsqrt(ms + EPS) * ng[:, ls]
            o = o * jax.nn.silu(g_ref[rs, ls].astype(f32))
            o_ref[rs, ls] = o.astype(o_ref.dtype)
        return carry

    lax.fori_loop(0, n_chunks, outputs, 0, unroll=2)


def _hgrn(z, lower_bounds, norm_g, batch, seq):
    M = z.shape[0]
    W = HGRN_GROUP * HGRN_HEAD_DIM
    ts = HGRN_ROWS
    n_s = seq // ts
    n_hg = D_HGRN // W
    col0 = 2 * D_GMLP // W

    def zspec(part):
        return pl.BlockSpec((ts, W), lambda b, hg, s, part=part: (b * n_s + s, col0 + part * n_hg + hg))

    G, K, SUB = HGRN_GROUP, HGRN_HEAD_DIM, HGRN_SUB
    n_chunks = ts // CHUNK
    est = (2 * 5 * ts * W * 2 + 4 * ts * W * 2 + ts * W * 4 + G * ts * ts * 2 + 2 * G * ts * n_chunks * K * 2
           + G * K * n_chunks * K * 6 + G * K * K * 4 + (G + 2) * ts * ts * 4)
    return pl.pallas_call(
        _hgrn_body,
        grid=(batch, n_hg, n_s),
        in_specs=[zspec(0), zspec(1), zspec(2), zspec(3),
                  pl.BlockSpec((2, W), lambda b, hg, s: (0, hg)),
                  pl.BlockSpec((1, W), lambda b, hg, s: (0, hg))],
        out_specs=pl.BlockSpec((ts, W), lambda b, hg, s: (b * n_s + s, hg)),
        out_shape=jax.ShapeDtypeStruct((M, D_HGRN), bf16),
        scratch_shapes=[pltpu.VMEM((G, K, K), f32),
                        pltpu.VMEM((ts, W), bf16),
                        pltpu.VMEM((ts, W), bf16),
                        pltpu.VMEM((ts, W), bf16),
                        pltpu.VMEM((ts, W), bf16),
                        pltpu.VMEM((n_chunks * 8, W), f32),
                        pltpu.VMEM((G, ts // SUB, SUB, SUB), bf16),
                        pltpu.VMEM((G, ts, SUB // CHUNK * K), bf16),
                        pltpu.VMEM((G, ts, SUB // CHUNK * K), bf16),
                        pltpu.VMEM((G, K, n_chunks * K), f32),
                        pltpu.VMEM((G, K, n_chunks * K), bf16),
                        pltpu.VMEM((ts, W), f32)],
        compiler_params=_params(("arbitrary", "arbitrary", "arbitrary"), est),
        name="hgrn2_mixer",
    )(z, z, z, z, lower_bounds.astype(f32), norm_g.reshape(1, D_HGRN).astype(f32))


def _xattn_body(h_ref, ss_ref, wq_ref, k_ref, v_ref, o_ref):
    D = h_ref.shape[1]
    inv_rms = lax.rsqrt(jnp.sum(ss_ref[...], axis=-1, keepdims=True) * (1.0 / D) + EPS)
    q = (jnp.dot(h_ref[...], wq_ref[...], preferred_element_type=f32) * inv_rms).astype(bf16)
    s = lax.dot_general(q, k_ref[...], (((1,), (1,)), ((), ())), preferred_element_type=f32)
    s = s * (X_HEAD_DIM ** -0.5)
    s = s - jnp.max(s, axis=-1, keepdims=True)
    p = jnp.exp(s)
    p = p / jnp.sum(p, axis=-1, keepdims=True)
    o_ref[...] = jnp.dot(p.astype(bf16), v_ref[...], preferred_element_type=f32).astype(o_ref.dtype)


def _xattn(hb, ss, wq, k, v, batch, seq, n_mem, tq=1024):
    M, D = hb.shape
    n_s = seq // tq
    dh = X_HEAD_DIM
    est = (2 * (tq * D * 2 + tq * ss.shape[1] * 4 + D * dh * 2 + 2 * n_mem * dh * 2 + tq * dh * 2)
           + 3 * tq * dh * 4)
    return pl.pallas_call(
        _xattn_body,
        grid=(X_HEADS, batch, n_s),
        in_specs=[pl.BlockSpec((tq, D), lambda h, b, s: (b * n_s + s, 0)),
                  pl.BlockSpec((tq, ss.shape[1]), lambda h, b, s: (b * n_s + s, 0)),
                  pl.BlockSpec((D, dh), lambda h, b, s: (0, h)),
                  pl.BlockSpec((n_mem, dh), lambda h, b, s: (b, h)),
                  pl.BlockSpec((n_mem, dh), lambda h, b, s: (b, h))],
        out_specs=pl.BlockSpec((tq, dh), lambda h, b, s: (b * n_s + s, h)),
        out_shape=jax.ShapeDtypeStruct((M, D), bf16),
        compiler_params=_params(("arbitrary", "arbitrary", "arbitrary"), est),
        name="xattn_qproj",
    )(hb, ss, wq, k, v)


u32 = jnp.uint32
ROUTE_GATE_LANE = 0
ROUTE_EID_LANE = 2


def _pack_bf16_pair(lo, hi):
    lo32 = lax.bitcast_convert_type(lo.astype(f32), u32) >> 16
    hi32 = lax.bitcast_convert_type(hi.astype(f32), u32) & jnp.uint32(0xFFFF0000)
    return lo32 | hi32


def _unpack_bf16_pair(w):
    lo = lax.bitcast_convert_type(w << 16, f32).astype(bf16)
    hi = lax.bitcast_convert_type(w & jnp.uint32(0xFFFF0000), f32).astype(bf16)
    return lo, hi


def _unpack_pair_f32(w):
    return (lax.bitcast_convert_type(w << 16, f32),
            lax.bitcast_convert_type(w & jnp.uint32(0xFFFF0000), f32))


def _first_lane_of(mask, lane):
    return jnp.min(jnp.where(mask, lane, V7X_LANES), axis=-1, keepdims=True)


def _router_body(h_ref, g_ref, wr_ref, br_ref, xm_ref, r_ref):
    x = h_ref[...]
    D = x.shape[1]
    ms = jnp.mean(x * x, axis=-1, keepdims=True)
    xn = x * lax.rsqrt(ms + EPS) * g_ref[...]
    xh, xm = _split2(xn)
    wh, wm = _split2(wr_ref[...])
    dot = functools.partial(jnp.dot, preferred_element_type=f32)
    nl = wh.shape[1]
    hi = dot(xh, jnp.concatenate([wh, wm], axis=1))
    logits = (dot(xm, wh) + hi[:, nl:]) + hi[:, :nl] + br_ref[...]

    xm_ref[...] = _pack_bf16_pair(xh[:, :D // 2], xh[:, D // 2:])

    lane = lax.broadcasted_iota(jnp.int32, logits.shape, 1)
    neg = jnp.float32(-jnp.inf)
    is_g = lane < N_GROUPS
    gl = jnp.where(is_g, logits, neg)
    gmax = jnp.max(gl, axis=-1, keepdims=True)
    grp = _first_lane_of(gl == gmax, lane)
    p_grp = 1.0 / jnp.sum(jnp.where(is_g, jnp.exp(logits - gmax), 0.0), axis=-1, keepdims=True)
    e0 = N_GROUPS + grp * EXPERTS_PER_GROUP
    el = jnp.where(jnp.logical_and(lane >= e0, lane < e0 + EXPERTS_PER_GROUP), logits, neg)
    m1 = jnp.max(el, axis=-1, keepdims=True)
    i1 = _first_lane_of(el == m1, lane)
    el2 = jnp.where(lane == i1, neg, el)
    m2 = jnp.max(el2, axis=-1, keepdims=True)
    i2 = _first_lane_of(el2 == m2, lane)
    t = jnp.exp(m2 - m1)
    g1 = p_grp * (1.0 / (1.0 + t))
    g2 = p_grp * (t / (1.0 + t))
    rec = jnp.where(lane == ROUTE_GATE_LANE, g1, 0.0)
    rec = jnp.where(lane == ROUTE_GATE_LANE + 1, g2, rec)
    rec = jnp.where(lane == ROUTE_EID_LANE, (i1 - N_GROUPS).astype(f32), rec)
    rec = jnp.where(lane == ROUTE_EID_LANE + 1, (i2 - N_GROUPS).astype(f32), rec)
    r_ref[...] = rec


def _router(h, n_moe, w_group, b_group, w_router, b_router, tm=512):
    M, D = h.shape
    n_log = N_GROUPS + N_EXPERTS
    wr = jnp.concatenate([w_group] + [w_router[g] for g in range(N_GROUPS)], axis=1)
    wr = jnp.pad(wr.astype(f32), ((0, 0), (0, V7X_LANES - n_log)))
    br = jnp.pad(jnp.concatenate([b_group, b_router.reshape(-1)]).astype(f32), (0, V7X_LANES - n_log))
    est = 2 * tm * D * 4 + 2 * D * V7X_LANES * 4 + 2 * tm * D * 2 + 6 * tm * D * 4
    return pl.pallas_call(
        _router_body,
        grid=(M // tm,),
        in_specs=[pl.BlockSpec((tm, D), lambda i: (i, 0)),
                  pl.BlockSpec((1, D), lambda i: (0, 0)),
                  pl.BlockSpec((D, V7X_LANES), lambda i: (0, 0)),
                  pl.BlockSpec((1, V7X_LANES), lambda i: (0, 0))],
        out_specs=[pl.BlockSpec((tm, D // 2), lambda i: (i, 0)),
                   pl.BlockSpec((tm, V7X_LANES), lambda i: (i, 0))],
        out_shape=[jax.ShapeDtypeStruct((M, D // 2), u32),
                   jax.ShapeDtypeStruct((M, V7X_LANES), f32)],
        compiler_params=_params(("arbitrary",), est),
        name="moe_router",
    )(h, n_moe.reshape(1, D).astype(f32), wr, br.reshape(1, V7X_LANES))


def _row_copy(src_hbm, row, dst, sem):
    return pltpu.make_async_copy(src_hbm.at[pl.ds(row, 1)], dst, sem)


PAD_BITS = tuple(1 << k for k in reversed(range((MOE_ROWS - 1).bit_length())))


def _dispatch_body(dest_ref, padend_ref, padlen_ref, xm_ref, xs_hbm, buf, zbuf, sem, zsem):
    R = GATHER_ROWS
    i = pl.program_id(0)
    n_steps = pl.num_programs(0)
    slot = i % 2

    def wait_slot(s):
        for _ in range(TOP_K):
            pltpu.make_async_copy(buf.at[s], xs_hbm.at[pl.ds(0, R)], sem.at[s]).wait()

    def pad_copies(fn):
        def per_expert(e, c):
            ln = padlen_ref[e]
            end = padend_ref[e]
            for bit in PAD_BITS:
                rows_per_copy = bit if bit >= V7X_SUBLANES else 1
                for j in range(bit // rows_per_copy):
                    @pl.when((ln & bit) != 0)
                    def _(bit=bit, end=end, j=j, n=rows_per_copy):
                        start = end - bit + j * n
                        if n > 1:
                            start = pl.multiple_of(start, V7X_SUBLANES)
                        fn(pltpu.make_async_copy(zbuf.at[pl.ds(0, n)], xs_hbm.at[pl.ds(start, n)], zsem))
                end = end - (ln & bit)
            return c
        lax.fori_loop(0, N_EXPERTS, per_expert, 0)

    @pl.when(i == 0)
    def _():
        zbuf[...] = jnp.zeros_like(zbuf)
        pad_copies(lambda cp: cp.start())

    @pl.when(i >= 2)
    def _():
        wait_slot(slot)

    buf[slot] = xm_ref[...]

    def issue(r, c):
        for k in range(TOP_K):
            d = dest_ref[(i * R + r) * TOP_K + k]
            pltpu.make_async_copy(buf.at[slot, pl.ds(r, 1)], xs_hbm.at[pl.ds(d, 1)],
                                  sem.at[slot]).start(priority=k % 2)
        return c

    lax.fori_loop(0, R, issue, 0, unroll=8)

    @pl.when(i == n_steps - 1)
    def _():
        if n_steps >= 2:
            wait_slot(1 - slot)
        wait_slot(slot)
        pad_copies(lambda cp: cp.wait())


def _moe_dispatch(xm, dest, pad_end, pad_len, P):
    T, Dh = xm.shape
    R = GATHER_ROWS
    est = 2 * R * Dh * 4 + 2 * R * Dh * 4 + PAD_BITS[0] * Dh * 4
    return pl.pallas_call(
        _dispatch_body,
        grid_spec=pltpu.PrefetchScalarGridSpec(
            num_scalar_prefetch=3,
            grid=(T // R,),
            in_specs=[pl.BlockSpec((R, Dh), lambda i, dest, pend, plen: (i, 0))],
            out_specs=pl.BlockSpec(memory_space=pl.ANY),
            scratch_shapes=[pltpu.VMEM((2, R, Dh), u32), pltpu.VMEM((PAD_BITS[0], Dh), u32),
                            pltpu.SemaphoreType.DMA((2,)), pltpu.SemaphoreType.DMA],
        ),
        out_shape=jax.ShapeDtypeStruct((P, Dh), u32),
        compiler_params=_params(("arbitrary",), est),
        name="moe_dispatch",
    )(dest, pad_end, pad_len, xm)


class _WeightStream:
    def __init__(self, w_hbms, wf_ref, wsem, col_tile):
        self.w_hbms, self.wf, self.wsem, self.ct = w_hbms, wf_ref, wsem, col_tile
        self.rows = wf_ref.shape[2]
        self.cr = self.rows // MOE_W_CHUNKS

    def _chunk(self, m, c, step, slot):
        e = step % N_EXPERTS
        t = step // N_EXPERTS
        r = pl.ds(pl.multiple_of(c * self.cr, self.cr), self.cr)
        cols = pl.ds(pl.multiple_of(t * self.ct, V7X_LANES), self.ct)
        return pltpu.make_async_copy(self.w_hbms[m].at[e, r, cols], self.wf.at[slot, m, r], self.wsem.at[slot])

    def start_chunk(self, c, step):
        for m in range(len(self.w_hbms)):
            self._chunk(m, c, step, step % 2).start()

    def start_chunks(self, c0, c1, step):
        lax.fori_loop(c0, c1, lambda c, carry: (self.start_chunk(c, step), carry)[1], 0)

    def wait(self, step):
        slot = step % 2
        for m in range(len(self.w_hbms)):
            for c in range(MOE_W_CHUNKS):
                self._chunk(m, c, step, slot).wait()

    def tile(self, m, step):
        return self.wf.at[step % 2, m]


def _expert_block_loop(row0, n_blocks, in_hbm, out_hbm, out_col0, ibuf, obuf, isem, osem, compute,
                       per_block):
    Bm = MOE_ROWS
    out_cols = obuf.shape[2]

    def rows(j):
        return pl.ds(pl.multiple_of(row0 + j * Bm, Bm), Bm)

    def in_copy(j, slot):
        return pltpu.make_async_copy(in_hbm.at[rows(j)], ibuf.at[slot], isem.at[slot])

    def out_copy(j, slot):
        return pltpu.make_async_copy(
            obuf.at[slot], out_hbm.at[rows(j), pl.ds(pl.multiple_of(out_col0, V7X_LANES), out_cols)],
            osem.at[slot])

    if per_block is None:
        @pl.when(n_blocks > 0)
        def _():
            in_copy(0, 0).start()
        return

    def block(j, carry):
        slot = j % 2

        @pl.when(j + 1 < n_blocks)
        def _():
            in_copy(j + 1, 1 - slot).start()

        per_block(j)
        in_copy(j, slot).wait()

        @pl.when(j >= 2)
        def _():
            out_copy(j - 2, slot).wait()

        obuf[slot] = compute(ibuf[slot])
        out_copy(j, slot).start()
        return carry

    lax.fori_loop(0, n_blocks, block, 0)

    @pl.when(n_blocks >= 2)
    def _():
        out_copy(n_blocks - 2, n_blocks % 2).wait()

    @pl.when(n_blocks >= 1)
    def _():
        out_copy(n_blocks - 1, (n_blocks - 1) % 2).wait()


def _expert_step(stream, n_blocks, cast, run_blocks):
    step = pl.program_id(0) * N_EXPERTS + pl.program_id(1)
    n_steps = pl.num_programs(0) * N_EXPERTS
    has_next = step + 1 < n_steps

    @pl.when(step == 0)
    def _():
        stream.start_chunks(0, MOE_W_CHUNKS, step)

    run_blocks(None)
    stream.wait(step)

    @pl.when(n_blocks > 0)
    def _():
        cast(step)

    spread = jnp.maximum((n_blocks + 1) // 2, 1)
    per = (MOE_W_CHUNKS + spread - 1) // spread

    @pl.when(jnp.logical_and(has_next, n_blocks == 0))
    def _():
        stream.start_chunks(0, MOE_W_CHUNKS, step + 1)

    def per_block(j):
        @pl.when(has_next)
        def _():
            stream.start_chunks(jnp.minimum(j * per, MOE_W_CHUNKS), jnp.minimum((j + 1) * per, MOE_W_CHUNKS),
                                step + 1)

    run_blocks(per_block)


def _moe_hidden_body(blk0_ref, nblk_ref, w1_hbm, w3_hbm, xs_hbm, hid_hbm,
                     wf_ref, w1b_ref, w3b_ref, ibuf, obuf, wsem, isem, osem):
    f = pl.program_id(0)
    e = pl.program_id(1)
    n_blocks = nblk_ref[e]
    half = ibuf.shape[2]
    stream = _WeightStream((w1_hbm, w3_hbm), wf_ref, wsem, MOE_F_TILE)

    def cast(step):
        _cast_tile(stream.tile(0, step), w1b_ref)
        _cast_tile(stream.tile(1, step), w3b_ref)

    def compute(xw):
        x_lo, x_hi = _unpack_bf16_pair(xw)

        def proj(wb_ref):
            return (jnp.dot(x_lo, wb_ref[:half, :], preferred_element_type=f32)
                    + jnp.dot(x_hi, wb_ref[half:, :], preferred_element_type=f32))

        return (jax.nn.silu(proj(w1b_ref)) * proj(w3b_ref)).astype(obuf.dtype)

    _expert_step(stream, n_blocks, cast, functools.partial(
        _expert_block_loop, blk0_ref[e] * MOE_ROWS, n_blocks, xs_hbm, hid_hbm, f * MOE_F_TILE,
        ibuf, obuf, isem, osem, compute))


def _moe_hidden(xs, w1, w3, blk0, nblk):
    P, Dh = xs.shape
    D = w1.shape[1]
    Bm, Ft = MOE_ROWS, MOE_F_TILE
    est = 2 * 2 * D * Ft * 4 + 2 * D * Ft * 2 + 2 * Bm * Dh * 4 + 2 * Bm * Ft * 2 + Bm * D * 2 + 3 * Bm * Ft * 4
    return pl.pallas_call(
        _moe_hidden_body,
        grid_spec=pltpu.PrefetchScalarGridSpec(
            num_scalar_prefetch=2,
            grid=(D_EXPERT // Ft, N_EXPERTS),
            in_specs=[pl.BlockSpec(memory_space=pl.ANY)] * 3,
            out_specs=pl.BlockSpec(memory_space=pl.ANY),
            scratch_shapes=[pltpu.VMEM((2, 2, D, Ft), f32),
                            pltpu.VMEM((D, Ft), bf16), pltpu.VMEM((D, Ft), bf16),
                            pltpu.VMEM((2, Bm, Dh), u32), pltpu.VMEM((2, Bm, Ft), bf16),
                            pltpu.SemaphoreType.DMA((2,)),
                            pltpu.SemaphoreType.DMA((2,)), pltpu.SemaphoreType.DMA((2,))],
        ),
        out_shape=jax.ShapeDtypeStruct((P, D_EXPERT), bf16),
        compiler_params=_params(("arbitrary", "arbitrary"), est),
        name="moe_expert_hidden",
    )(blk0, nblk, w1, w3, xs)


def _moe_out_body(blk0_ref, nblk_ref, w2_hbm, hid_hbm, y_hbm, wf_ref, w2b_ref, ibuf, obuf,
                  wsem, isem, osem):
    n = pl.program_id(0)
    e = pl.program_id(1)
    n_blocks = nblk_ref[e]
    stream = _WeightStream((w2_hbm,), wf_ref, wsem, MOE_N_TILE)

    def cast(step):
        _cast_tile(stream.tile(0, step), w2b_ref)

    def compute(h):
        y = jnp.dot(h, w2b_ref[...], preferred_element_type=f32)
        half = y.shape[1] // 2
        return _pack_bf16_pair(y[:, :half].astype(bf16), y[:, half:].astype(bf16))

    _expert_step(stream, n_blocks, cast, functools.partial(
        _expert_block_loop, blk0_ref[e] * MOE_ROWS, n_blocks, hid_hbm, y_hbm, n * (MOE_N_TILE // 2),
        ibuf, obuf, isem, osem, compute))


def _moe_out(hid, w2, blk0, nblk):
    P, F = hid.shape
    D = w2.shape[2]
    Bm, Nt = MOE_ROWS, MOE_N_TILE
    assert Nt == D, "the packed output pairs feature j with j + D/2: the column tile must span all of D"
    est = 2 * F * Nt * 4 + F * Nt * 2 + 2 * Bm * F * 2 + 2 * Bm * Nt * 2 + 2 * Bm * Nt * 4
    return pl.pallas_call(
        _moe_out_body,
        grid_spec=pltpu.PrefetchScalarGridSpec(
            num_scalar_prefetch=2,
            grid=(D // Nt, N_EXPERTS),
            in_specs=[pl.BlockSpec(memory_space=pl.ANY)] * 2,
            out_specs=pl.BlockSpec(memory_space=pl.ANY),
            scratch_shapes=[pltpu.VMEM((2, 1, F, Nt), f32), pltpu.VMEM((F, Nt), bf16),
                            pltpu.VMEM((2, Bm, F), bf16), pltpu.VMEM((2, Bm, Nt // 2), u32),
                            pltpu.SemaphoreType.DMA((2,)),
                            pltpu.SemaphoreType.DMA((2,)), pltpu.SemaphoreType.DMA((2,))],
        ),
        out_shape=jax.ShapeDtypeStruct((P, D // 2), u32),
        compiler_params=_params(("arbitrary", "arbitrary"), est),
        name="moe_expert_out",
    )(blk0, nblk, w2, hid)


def _combine_body(dest_ref, h_ref, gate_ref, g_ref, y_hbm, o_ref, buf, ss_s, sem):
    R = GATHER_ROWS
    i = pl.program_id(0)
    n_steps = pl.num_programs(0)

    def issue_rows(step, slot, r0, n):
        for r in range(n):
            for k in range(TOP_K):
                _row_copy(y_hbm, dest_ref[(step * R + r0 + r) * TOP_K + k],
                          buf.at[slot, k, pl.ds(r0 + r, 1)], sem.at[slot]).start(priority=k % 2)

    @pl.when(i == 0)
    def _():
        lax.fori_loop(0, R // V7X_SUBLANES,
                      lambda c, carry: (issue_rows(0, 0, c * V7X_SUBLANES, V7X_SUBLANES), carry)[1], 0)

    slot = i % 2
    for k in range(TOP_K):
        pltpu.make_async_copy(y_hbm.at[pl.ds(0, R)], buf.at[slot, k], sem.at[slot]).wait()
    g_fin = g_ref[...]
    D = h_ref.shape[1]
    RC = V7X_SUBLANES

    def chunk(c):
        return pl.ds(pl.multiple_of(c * RC, RC), RC)

    def passes(prefetch):
        def accumulate(c, carry):
            rs = chunk(c)
            gate = gate_ref[rs, :]
            g1 = gate[:, ROUTE_GATE_LANE:ROUTE_GATE_LANE + 1]
            g2 = gate[:, ROUTE_GATE_LANE + 1:ROUTE_GATE_LANE + 2]
            y1_lo, y1_hi = _unpack_pair_f32(buf[slot, 0, rs, :])
            y2_lo, y2_hi = _unpack_pair_f32(buf[slot, 1, rs, :])
            ss = None
            for cols, y1, y2 in ((slice(0, D // 2), y1_lo, y2_lo), (slice(D // 2, D), y1_hi, y2_hi)):
                h = h_ref[rs, cols] + y1 * g1 + y2 * g2
                o_ref[rs, cols] = h
                sq = h * h
                for j in range(sq.shape[1] // V7X_LANES):
                    part = sq[:, j * V7X_LANES:(j + 1) * V7X_LANES]
                    ss = part if ss is None else ss + part
            ss_s[rs, :] = ss
            if prefetch:
                issue_rows(i + 1, 1 - slot, c * RC, RC // 2)
            return carry

        lax.fori_loop(0, R // RC, accumulate, 0, unroll=4)

        def normalize(c, carry):
            rs = chunk(c)
            ms = jnp.sum(ss_s[rs, :], axis=-1, keepdims=True) * (1.0 / D)
            o_ref[rs, :] = o_ref[rs, :] * lax.rsqrt(ms + EPS) * g_fin
            if prefetch:
                issue_rows(i + 1, 1 - slot, c * RC + RC // 2, RC // 2)
            return carry

        lax.fori_loop(0, R // RC, normalize, 0, unroll=4)

    @pl.when(i + 1 < n_steps)
    def _():
        passes(True)

    @pl.when(i + 1 == n_steps)
    def _():
        passes(False)


def _moe_combine(h, y, dest, gate, n_final):
    T, D = h.shape
    R = GATHER_ROWS
    est = 2 * TOP_K * R * D * 4 + 4 * R * D * 4 + 3 * R * D * 4
    return pl.pallas_call(
        _combine_body,
        grid_spec=pltpu.PrefetchScalarGridSpec(
            num_scalar_prefetch=1,
            grid=(T // R,),
            in_specs=[pl.BlockSpec((R, D), lambda i, dest: (i, 0)),
                      pl.BlockSpec((R, V7X_LANES), lambda i, dest: (i, 0)),
                      pl.BlockSpec((1, D), lambda i, dest: (0, 0)),
                      pl.BlockSpec(memory_space=pl.ANY)],
            out_specs=pl.BlockSpec((R, D), lambda i, dest: (i, 0)),
            scratch_shapes=[pltpu.VMEM((2, TOP_K, R, D // 2), u32), pltpu.VMEM((R, V7X_LANES), f32),
                            pltpu.SemaphoreType.DMA((2,))],
        ),
        out_shape=jax.ShapeDtypeStruct((T, D), f32),
        compiler_params=_params(("arbitrary",), est),
        name="moe_combine_norm",
    )(dest, h, gate, n_final.reshape(1, D).astype(f32), y)


def _slot_layout(rec):
    T = rec.shape[0]
    A = T * TOP_K
    eid = rec[:, ROUTE_EID_LANE:ROUTE_EID_LANE + TOP_K].astype(jnp.int32).reshape(A)
    onehot = (eid[:, None] == jnp.arange(N_EXPERTS, dtype=jnp.int32)[None, :]).astype(jnp.int32)
    csum = jnp.cumsum(onehot, axis=0)
    rank = jnp.sum(onehot * csum, axis=1) - 1
    counts = csum[-1]
    pcounts = (counts + MOE_ROWS - 1) // MOE_ROWS * MOE_ROWS
    pend = jnp.cumsum(pcounts)
    pstart = pend - pcounts
    dest = (pstart[eid] + rank).astype(jnp.int32)
    n_blocks = -(-(A + N_EXPERTS * (MOE_ROWS - 1)) // MOE_ROWS)
    P = n_blocks * MOE_ROWS
    blk0 = (pstart // MOE_ROWS).astype(jnp.int32)
    nblk = (pcounts // MOE_ROWS).astype(jnp.int32)
    pad_end = pend.astype(jnp.int32)
    pad_len = (pcounts - counts).astype(jnp.int32)
    return dest, blk0, nblk, pad_end, pad_len, P


def kernel(x, mem, n_mix, w_in, gmlp_ln_g, gmlp_ln_b, gmlp_w_s, gmlp_b_s, hgrn_lower_bounds,
           hgrn_norm_g, w_out, n_cross, n_mem, w_q_x, w_k_x, w_v_x, w_o_x, n_moe, w_group,
           b_group, w_router, b_router, w1_e, w3_e, w2_e, n_final):
    B, S, D = x.shape
    n_mem_tok = mem.shape[1]
    T = B * S
    xt = x.reshape(T, D)

    hn = _rmsnorm(xt, n_mix[0], bf16)
    z = _matmul(hn, w_in[0], bf16, 1024, 768, "mm_w_in")
    ya = _gmlp(z, gmlp_ln_g[0], gmlp_ln_b[0], gmlp_w_s[0].astype(f32), gmlp_b_s[0].astype(f32))
    yb = _hgrn(z, hgrn_lower_bounds, hgrn_norm_g[0], B, S)
    h1, h1b, h1ss = _matmul_res(xt, [ya, yb], w_out[0], 1024, 512, "mm_w_out", with_norm_inputs=True)

    mn = _rmsnorm(mem.reshape(B * n_mem_tok, D), n_mem[0], bf16)
    kx = _matmul(mn, w_k_x[0], bf16, B * n_mem_tok, 512, "mm_w_k")
    vx = _matmul(mn, w_v_x[0], bf16, B * n_mem_tok, 512, "mm_w_v")
    wq_gained = (w_q_x[0] * n_cross[0].astype(f32)[:, None]).astype(bf16)
    ox = _xattn(h1b, h1ss, wq_gained, kx, vx, B, S, n_mem_tok)
    h2 = _matmul_res(h1, [ox], w_o_x[0], 1024, 512, "mm_w_o")

    xm, rec = _router(h2, n_moe[0], w_group[0], b_group[0], w_router[0], b_router[0])
    dest, blk0, nblk, pad_end, pad_len, P = _slot_layout(rec)
    xs = _moe_dispatch(xm, dest, pad_end, pad_len, P)
    hid = _moe_hidden(xs, w1_e[0], w3_e[0], blk0, nblk)
    y = _moe_out(hid, w2_e[0], blk0, nblk)
    out = _moe_combine(h2, y, dest, rec, n_final)
    return out.reshape(B, S, D)
```
